```python
import jax, jax.numpy as jnp
from jax import lax
import numpy as np

D_MODEL = 1024
BATCH = 8
SEQ = 2048
DEPTH = 4
DEC_BATCH = 128
DEC_SEQ = 4
PAST_LEN = 8192
PAGE_SIZE = 128

N_MIXERS = 4
N_RET = len(range(0, DEPTH, N_MIXERS))
N_CMLP = len(range(1, DEPTH, N_MIXERS))
N_MLA = len(range(2, DEPTH, N_MIXERS))
N_POOL = len(range(3, DEPTH, N_MIXERS))

NORM_EPS = 1e-6
ROPE_BASE = 10000.0

RET_HEADS = 4
RET_QK_DIM = D_MODEL // RET_HEADS
RET_V_DIM = 2 * D_MODEL // RET_HEADS
RET_CHUNK = 128

CM_CHUNK = 128
CM_GROUPS = 4
CM_WIDTH = D_MODEL
CM_GROUP_DIM = CM_WIDTH // CM_GROUPS

MLA_HEADS = 8
MLA_Q_LORA = 384
MLA_KV_LORA = 256
MLA_NOPE = 128
MLA_ROPE = 64
MLA_V = 128
MLA_QBLOCK = 128
MLA_SCALE = (MLA_NOPE + MLA_ROPE) ** -0.5

POOL_WINDOWS = (2, 4, 8, 16)
POOL_GROUPS = len(POOL_WINDOWS)
POOL_GROUP_DIM = D_MODEL // POOL_GROUPS
POOL_BUF = max(POOL_WINDOWS) - 1

FFN_HIDDEN = -(-8 * D_MODEL // (3 * 256)) * 256

kernel_name = 'hybrid_ret_cmlp_mla_pool_decode_step'


def rms_norm(x, g):
    xf = x.astype(jnp.float32)
    y = xf * lax.rsqrt(jnp.mean(xf * xf, -1, keepdims=True) + NORM_EPS)
    return (y * g.astype(jnp.float32)).astype(x.dtype)


def rope(x, pos):
    half = x.shape[-1] // 2
    inv = ROPE_BASE ** (-jnp.arange(half, dtype=jnp.float32) / half)
    ang = pos.astype(jnp.float32)[:, None] * inv[None, :]
    cos = jnp.cos(ang)[None, :, None, :]
    sin = jnp.sin(ang)[None, :, None, :]
    xf = x.astype(jnp.float32)
    x1, x2 = xf[..., :half], xf[..., half:]
    return jnp.concatenate([x1 * cos - x2 * sin, x1 * sin + x2 * cos], -1).astype(x.dtype)


def swiglu(h, w_gate_up, w_down):
    g, u = jnp.split(h @ w_gate_up, [FFN_HIDDEN], axis=-1)
    return (jax.nn.silu(g) * u) @ w_down


def retention_log_decay():
    return jnp.log1p(-jnp.exp2(-5.0 - jnp.arange(RET_HEADS, dtype=jnp.float32)))


def retention_chunk(s, q, k, v, log_g):
    L = q.shape[2]
    idx = jnp.arange(L, dtype=jnp.float32)
    rel = idx[:, None] - idx[None, :]
    decay = jnp.where(rel >= 0, jnp.exp(jnp.maximum(rel, 0.0) * log_g[:, None, None]), 0.0)
    scores = jnp.einsum('bhid,bhjd->bhij', q, k) * decay
    q_dec = q * jnp.exp((idx + 1.0)[None, :] * log_g[:, None])[..., None]
    k_dec = k * jnp.exp((L - 1.0 - idx)[None, :] * log_g[:, None])[..., None]
    o = jnp.einsum('bhij,bhjv->bhiv', scores, v) + jnp.einsum('bhid,bhdv->bhiv', q_dec, s)
    s_new = jnp.exp(L * log_g)[:, None, None] * s + jnp.einsum('bhjd,bhjv->bhdv', k_dec, v)
    return s_new, o


def retention_mixer(h, pos, s0, w_in, gn_gain, w_out):
    b, t, _ = h.shape
    q, k, v, g = jnp.split(h @ w_in, [D_MODEL, 2 * D_MODEL, 4 * D_MODEL], axis=-1)
    q = rope(q.reshape(b, t, RET_HEADS, RET_QK_DIM), pos)
    k = rope(k.reshape(b, t, RET_HEADS, RET_QK_DIM), pos)
    qf = q.astype(jnp.float32).transpose(0, 2, 1, 3) * (RET_QK_DIM ** -0.5)
    kf = k.astype(jnp.float32).transpose(0, 2, 1, 3)
    vf = v.reshape(b, t, RET_HEADS, RET_V_DIM).astype(jnp.float32).transpose(0, 2, 1, 3)
    log_g = retention_log_decay()
    if t > RET_CHUNK and t % RET_CHUNK == 0:
        nc = t // RET_CHUNK
        def blocks(a):
            return a.reshape(b, RET_HEADS, nc, RET_CHUNK, a.shape[-1]).transpose(2, 0, 1, 3, 4)
        s_fin, o = lax.scan(lambda s, xs: retention_chunk(s, xs[0], xs[1], xs[2], log_g),
                            s0, (blocks(qf), blocks(kf), blocks(vf)))
        o = o.transpose(1, 2, 0, 3, 4).reshape(b, RET_HEADS, t, RET_V_DIM)
    else:
        s_fin, o = retention_chunk(s0, qf, kf, vf, log_g)
    mu = jnp.mean(o, -1, keepdims=True)
    var = jnp.mean(jnp.square(o - mu), -1, keepdims=True)
    o = ((o - mu) * lax.rsqrt(var + NORM_EPS)).transpose(0, 2, 1, 3).reshape(b, t, RET_HEADS * RET_V_DIM)
    y = (jax.nn.silu(g.astype(jnp.float32)) * o * gn_gain.astype(jnp.float32)).astype(h.dtype)
    return y @ w_out, s_fin.astype(h.dtype)


def spatial_gate(v, w_s, b_s):
    b, t = v.shape[:2]
    nc = -(-t // CM_CHUNK)
    pad = nc * CM_CHUNK - t
    vp = jnp.pad(v, ((0, 0), (0, pad), (0, 0), (0, 0))).reshape(b, nc, CM_CHUNK, CM_GROUPS, CM_GROUP_DIM)
    w = jnp.where(jnp.tril(jnp.ones((CM_CHUNK, CM_CHUNK), bool))[None], w_s, 0)
    out = jnp.einsum('gts,bnsgd->bntgd', w, vp) + jnp.transpose(b_s)[None, None, :, :, None]
    return out.reshape(b, nc * CM_CHUNK, CM_GROUPS, CM_GROUP_DIM)[:, :t]


def chunk_mlp_mixer(h, w_in, ln_gain, w_s, b_s, w_out):
    b, t, _ = h.shape
    u, v = jnp.split(jax.nn.gelu(h @ w_in), [CM_WIDTH], axis=-1)
    vf = v.astype(jnp.float32)
    mu = jnp.mean(vf, -1, keepdims=True)
    var = jnp.mean(jnp.square(vf - mu), -1, keepdims=True)
    v = ((vf - mu) * lax.rsqrt(var + NORM_EPS) * ln_gain.astype(jnp.float32)).astype(h.dtype)
    mixed = spatial_gate(v.reshape(b, t, CM_GROUPS, CM_GROUP_DIM), w_s, b_s).reshape(b, t, CM_WIDTH)
    return (u * mixed) @ w_out, v


def mla_project(h, pos, w_down, g_q, g_kv, w_uq, w_uk):
    b, t, _ = h.shape
    d = h @ w_down
    c_q = rms_norm(d[..., :MLA_Q_LORA], g_q)
    c_kv = rms_norm(d[..., MLA_Q_LORA:MLA_Q_LORA + MLA_KV_LORA], g_kv)
    k_rope = rope(d[..., MLA_Q_LORA + MLA_KV_LORA:][:, :, None, :], pos)[:, :, 0, :]
    q = (c_q @ w_uq).reshape(b, t, MLA_HEADS, MLA_NOPE + MLA_ROPE)
    q_rope = rope(q[..., MLA_NOPE:], pos)
    q_lat = jnp.einsum('bthn,khn->bthk', q[..., :MLA_NOPE], w_uk)
    return q_lat, q_rope, c_kv, k_rope


def mla_scores(q_lat, q_rope, c, kr):
    s = jnp.einsum('bthk,bsk->bhts', q_lat, c) + jnp.einsum('bthr,bsr->bhts', q_rope, kr)
    return s.astype(jnp.float32) * MLA_SCALE


def mla_output(o_lat, w_uv, w_out):
    b, t = o_lat.shape[:2]
    o = jnp.einsum('bthk,khv->bthv', o_lat, w_uv)
    return o.reshape(b, t, MLA_HEADS * MLA_V) @ w_out


def mla_prompt(h, w_down, g_q, g_kv, w_uq, w_uk, w_uv, w_out):
    b, t, _ = h.shape
    pos = jnp.arange(t)
    q_lat, q_rope, c, kr = mla_project(h, pos, w_down, g_q, g_kv, w_uq, w_uk)
    nb = t // MLA_QBLOCK
    qlb = q_lat.reshape(b, nb, MLA_QBLOCK, MLA_HEADS, MLA_KV_LORA).transpose(1, 0, 2, 3, 4)
    qrb = q_rope.reshape(b, nb, MLA_QBLOCK, MLA_HEADS, MLA_ROPE).transpose(1, 0, 2, 3, 4)
    qpos = pos.reshape(nb, MLA_QBLOCK)

    def block(args):
        ql, qr, qp = args
        s = mla_scores(ql, qr, c, kr)
        s = jnp.where((pos[None, :] <= qp[:, None])[None, None], s, -jnp.inf)
        p = jax.nn.softmax(s, axis=-1).astype(c.dtype)
        return jnp.einsum('bhts,bsk->bthk', p, c)

    o_lat = lax.map(block, (qlb, qrb, qpos)).transpose(1, 0, 2, 3, 4).reshape(b, t, MLA_HEADS, MLA_KV_LORA)
    return mla_output(o_lat, w_uv, w_out), c, kr


def mla_sample(h, cache_lat, cache_kr, page_table, w_down, g_q, g_kv, w_uq, w_uk, w_uv, w_out):
    b, t, _ = h.shape
    pos = PAST_LEN + jnp.arange(t)
    q_lat, q_rope, c_new, kr_new = mla_project(h, pos, w_down, g_q, g_kv, w_uq, w_uk)
    c_past = cache_lat[page_table].reshape(b, -1, MLA_KV_LORA)
    kr_past = cache_kr[page_table].reshape(b, -1, MLA_ROPE)
    n_past = c_past.shape[1]
    s_past = mla_scores(q_lat, q_rope, c_past, kr_past)
    s_new = jnp.where(jnp.tril(jnp.ones((t, t), bool))[None, None],
                      mla_scores(q_lat, q_rope, c_new, kr_new), -jnp.inf)
    p = jax.nn.softmax(jnp.concatenate([s_past, s_new], -1), axis=-1).astype(c_new.dtype)
    o_lat = (jnp.einsum('bhts,bsk->bthk', p[..., :n_past], c_past)
             + jnp.einsum('bhts,bsk->bthk', p[..., n_past:], c_new))
    return mla_output(o_lat, w_uv, w_out), c_new, kr_new


def pool_mixer(h, prefix, pos0, w_pool, scale):
    b, t, _ = h.shape
    ext = jnp.concatenate([prefix.astype(h.dtype), h], axis=1)
    cs = lax.cumsum(ext.astype(jnp.float32), axis=1)
    cs = jnp.concatenate([jnp.zeros((b, 1, D_MODEL), jnp.float32), cs], axis=1)
    end = cs[:, POOL_BUF + 1:]
    pos = pos0 + jnp.arange(t, dtype=jnp.float32)
    means = []
    for gi, w in enumerate(POOL_WINDOWS):
        sl = slice(gi * POOL_GROUP_DIM, (gi + 1) * POOL_GROUP_DIM)
        start = cs[:, POOL_BUF + 1 - w:POOL_BUF + 1 - w + t, sl]
        cnt = jnp.minimum(pos + 1.0, float(w))[None, :, None]
        means.append((end[..., sl] - start) / cnt)
    pooled = (jnp.concatenate(means, -1) - h.astype(jnp.float32)).astype(h.dtype)
    pooled = pooled.reshape(b, t, POOL_GROUPS, POOL_GROUP_DIM)
    out = jnp.einsum('btgc,gcd->btgd', pooled, w_pool).reshape(b, t, D_MODEL) * scale
    return out, ext[:, -POOL_BUF:]


def setup_inputs(seed: int = 0) -> dict:
    key = jax.random.key(seed)
    ks = jax.random.split(key, 40)
    cnt = [0]

    def nxt():
        k = ks[cnt[0]]
        cnt[0] += 1
        return k

    def nrm(shape, scale=1.0):
        return jax.random.normal(nxt(), shape, jnp.float32) * scale

    def gain(shape):
        return 1.0 + nrm(shape, 0.1)

    n_pages = PAST_LEN // PAGE_SIZE
    n_used = DEC_BATCH * n_pages
    n_phys = n_used + n_used // 4
    page_table = jax.random.permutation(nxt(), n_phys)[:n_used].reshape(DEC_BATCH, n_pages).astype(jnp.int32)
    return {
        'x_prompt': nrm((BATCH, SEQ, D_MODEL)),
        'x_sample': nrm((DEC_BATCH, DEC_SEQ, D_MODEL)),
        'state_ret': nrm((N_RET, DEC_BATCH, RET_HEADS, RET_QK_DIM, RET_V_DIM)),
        'cache_mla_latent': nrm((N_MLA, n_phys, PAGE_SIZE, MLA_KV_LORA)),
        'cache_mla_krope': nrm((N_MLA, n_phys, PAGE_SIZE, MLA_ROPE)),
        'page_table': page_table,
        'state_pool': nrm((N_POOL, DEC_BATCH, POOL_BUF, D_MODEL)),
        'norm_mix': gain((DEPTH, D_MODEL)),
        'norm_ffn': gain((DEPTH, D_MODEL)),
        'norm_final': gain((D_MODEL,)),
        'ret_w_in': nrm((N_RET, D_MODEL, 6 * D_MODEL), D_MODEL ** -0.5),
        'ret_gn_gain': gain((N_RET, RET_HEADS * RET_V_DIM)),
        'ret_w_out': nrm((N_RET, RET_HEADS * RET_V_DIM, D_MODEL), (RET_HEADS * RET_V_DIM) ** -0.5),
        'cm_w_in': nrm((N_CMLP, D_MODEL, 2 * CM_WIDTH), D_MODEL ** -0.5),
        'cm_ln_gain': gain((N_CMLP, CM_WIDTH)),
        'cm_w_spatial': nrm((N_CMLP, CM_GROUPS, CM_CHUNK, CM_CHUNK), CM_CHUNK ** -0.5),
        'cm_b_spatial': gain((N_CMLP, CM_GROUPS, CM_CHUNK)),
        'cm_w_out': nrm((N_CMLP, CM_WIDTH, D_MODEL), CM_WIDTH ** -0.5),
        'mla_w_down': nrm((N_MLA, D_MODEL, MLA_Q_LORA + MLA_KV_LORA + MLA_ROPE), D_MODEL ** -0.5),
        'mla_q_norm': gain((N_MLA, MLA_Q_LORA)),
        'mla_kv_norm': gain((N_MLA, MLA_KV_LORA)),
        'mla_w_uq': nrm((N_MLA, MLA_Q_LORA, MLA_HEADS * (MLA_NOPE + MLA_ROPE)), MLA_Q_LORA ** -0.5),
        'mla_w_uk': nrm((N_MLA, MLA_KV_LORA, MLA_HEADS, MLA_NOPE), MLA_KV_LORA ** -0.5),
        'mla_w_uv': nrm((N_MLA, MLA_KV_LORA, MLA_HEADS, MLA_V), MLA_KV_LORA ** -0.5),
        'mla_w_out': nrm((N_MLA, MLA_HEADS * MLA_V, D_MODEL), (MLA_HEADS * MLA_V) ** -0.5),
        'pool_w': nrm((N_POOL, POOL_GROUPS, POOL_GROUP_DIM, POOL_GROUP_DIM), POOL_GROUP_DIM ** -0.5),
        'pool_scale': gain((N_POOL, D_MODEL)),
        'ffn_w_gate_up': nrm((DEPTH, D_MODEL, 2 * FFN_HIDDEN), D_MODEL ** -0.5),
        'ffn_w_down': nrm((DEPTH, FFN_HIDDEN, D_MODEL), FFN_HIDDEN ** -0.5),
    }


def reference(x_prompt, x_sample, state_ret, cache_mla_latent, cache_mla_krope, page_table, state_pool,
              norm_mix, norm_ffn, norm_final,
              ret_w_in, ret_gn_gain, ret_w_out,
              cm_w_in, cm_ln_gain, cm_w_spatial, cm_b_spatial, cm_w_out,
              mla_w_down, mla_q_norm, mla_kv_norm, mla_w_uq, mla_w_uk, mla_w_uv, mla_w_out,
              pool_w, pool_scale,
              ffn_w_gate_up, ffn_w_down):
    xp, xs = x_prompt, x_sample
    bp, tp = xp.shape[:2]
    bs, ts = xs.shape[:2]
    pos_p = jnp.arange(tp)
    pos_s = PAST_LEN + jnp.arange(ts)
    ret_p, ret_s, cm_p, cm_s = [], [], [], []
    lat_p, kr_p, lat_s, kr_s, pool_p, pool_s = [], [], [], [], [], []
    for i in range(DEPTH):
        kind, j = i % N_MIXERS, i // N_MIXERS
        hp = rms_norm(xp, norm_mix[i])
        hs = rms_norm(xs, norm_mix[i])
        if kind == 0:
            s0 = jnp.zeros((bp, RET_HEADS, RET_QK_DIM, RET_V_DIM), jnp.float32)
            mp, sp = retention_mixer(hp, pos_p, s0, ret_w_in[j], ret_gn_gain[j], ret_w_out[j])
            ms, ss = retention_mixer(hs, pos_s, state_ret[j].astype(jnp.float32), ret_w_in[j], ret_gn_gain[j], ret_w_out[j])
            ret_p.append(sp)
            ret_s.append(ss)
        elif kind == 1:
            mp, vp = chunk_mlp_mixer(hp, cm_w_in[j], cm_ln_gain[j], cm_w_spatial[j], cm_b_spatial[j], cm_w_out[j])
            ms, vs = chunk_mlp_mixer(hs, cm_w_in[j], cm_ln_gain[j], cm_w_spatial[j], cm_b_spatial[j], cm_w_out[j])
            cm_p.append(vp[:, ((tp - 1) // CM_CHUNK) * CM_CHUNK:])
            cm_s.append(vs)
        elif kind == 2:
            mp, cp, rp = mla_prompt(hp, mla_w_down[j], mla_q_norm[j], mla_kv_norm[j], mla_w_uq[j], mla_w_uk[j], mla_w_uv[j], mla_w_out[j])
            ms, cs_, rs = mla_sample(hs, cache_mla_latent[j], cache_mla_krope[j], page_table, mla_w_down[j], mla_q_norm[j], mla_kv_norm[j], mla_w_uq[j], mla_w_uk[j], mla_w_uv[j], mla_w_out[j])
            lat_p.append(cp)
            kr_p.append(rp)
            lat_s.append(cs_)
            kr_s.append(rs)
        else:
            mp, bufp = pool_mixer(hp, jnp.zeros((bp, POOL_BUF, D_MODEL), hp.dtype), 0, pool_w[j], pool_scale[j])
            ms, bufs = pool_mixer(hs, state_pool[j], PAST_LEN, pool_w[j], pool_scale[j])
            pool_p.append(bufp)
            pool_s.append(bufs)
        xp = xp + mp
        xs = xs + ms
        xp = xp + swiglu(rms_norm(xp, norm_ffn[i]), ffn_w_gate_up[i], ffn_w_down[i])
        xs = xs + swiglu(rms_norm(xs, norm_ffn[i]), ffn_w_gate_up[i], ffn_w_down[i])
    y_prompt = rms_norm(xp, norm_final)
    y_sample = rms_norm(xs, norm_final)
    return (y_prompt, y_sample, jnp.stack(ret_p), jnp.stack(ret_s), jnp.stack(cm_p), jnp.stack(cm_s),
            jnp.stack(lat_p), jnp.stack(kr_p), jnp.stack(lat_s), jnp.stack(kr_s),
            jnp.stack(pool_p), jnp.stack(pool_s))
```

```python
import functools

import jax
import jax.numpy as jnp
from jax import lax
from jax.experimental import pallas as pl
from jax.experimental.pallas import tpu as pltpu

F32 = jnp.float32
BF16 = jnp.bfloat16

D_MODEL = 1024
NORM_EPS = 1e-6
ROPE_BASE = 10000.0
PAST_LEN = 8192
PAGE_SIZE = 128

RET_HEADS = 4
RET_QK_DIM = 256
RET_V_DIM = 512
RET_CHUNK = 128

CM_CHUNK = 128
CM_GROUPS = 4
CM_GROUP_DIM = 256

MLA_HEADS = 8
MLA_Q_LORA = 384
MLA_KV_LORA = 256
MLA_NOPE = 128
MLA_ROPE = 64
MLA_V = 128
MLA_SCALE = (MLA_NOPE + MLA_ROPE) ** -0.5
MLA_KEY_DIM = MLA_KV_LORA + 128
MLA_PAGES_PER_STEP = 8

POOL_WINDOWS = (2, 4, 8, 16)
POOL_GROUP_DIM = 256
POOL_BUF = 15
POOL_PREV = 16

FFN_HIDDEN = 2816
FFN_CHUNK = 256

V7X_VMEM_BYTES = 64 * 1024 * 1024
MIB = 1024 * 1024


def _params(semantics, vmem_mib):
    assert vmem_mib * MIB < V7X_VMEM_BYTES
    return pltpu.CompilerParams(dimension_semantics=semantics,
                                vmem_limit_bytes=vmem_mib * MIB)


def _resident(shape):
    nd = len(shape)
    return pl.BlockSpec(shape, lambda *_: (0,) * nd, pipeline_mode=pl.Buffered(1))


def _rms(x, g):
    return x * lax.rsqrt(jnp.mean(x * x, axis=-1, keepdims=True) + NORM_EPS) * g


def _dot(a, b):
    return jnp.dot(a, b, preferred_element_type=F32)


def _dot_nt(a, b):
    return lax.dot_general(a, b, (((1,), (1,)), ((), ())), preferred_element_type=F32)


def _ffn_kernel(x_ref, g_ref, wg_ref, wu_ref, wd_ref, gf_ref, o_ref, *, final):
    x = x_ref[...]
    h = _rms(x, g_ref[...]).astype(BF16)
    o_ref[...] = x

    def body(j, carry):
        g = _dot(h, wg_ref[j])
        u = _dot(h, wu_ref[j])
        a = (g * jax.nn.sigmoid(g)) * u
        o_ref[...] += _dot(a.astype(BF16), wd_ref[j])
        return carry

    lax.fori_loop(0, wg_ref.shape[0], body, 0)
    if final:
        o_ref[...] = _rms(o_ref[...], gf_ref[...])


def _ffn(x, gain, wg, wu, wd, final_gain, final):
    n = x.shape[0]
    tm = min(n, 512)
    assert n % tm == 0
    return pl.pallas_call(
        functools.partial(_ffn_kernel, final=final),
        grid=(n // tm,),
        in_specs=[pl.BlockSpec((tm, D_MODEL), lambda i: (i, 0)),
                  _resident(gain.shape), _resident(wg.shape), _resident(wu.shape),
                  _resident(wd.shape), _resident(final_gain.shape)],
        out_specs=pl.BlockSpec((tm, D_MODEL), lambda i: (i, 0)),
        out_shape=jax.ShapeDtypeStruct(x.shape, F32),
        compiler_params=_params(("parallel",), 48),
        name="ffn",
    )(x, gain, wg, wu, wd, final_gain)


def _norm_matmul_kernel(x_ref, g_ref, w_ref, o_ref):
    h = _rms(x_ref[...], g_ref[...]).astype(BF16)
    o_ref[...] = _dot(h, w_ref[...])


def _norm_matmul(x, gain, w, tn):
    n, dout = x.shape[0], w.shape[1]
    assert dout % tn == 0
    return pl.pallas_call(
        _norm_matmul_kernel,
        grid=(dout // tn,),
        in_specs=[pl.BlockSpec((n, D_MODEL), lambda j: (0, 0)),
                  pl.BlockSpec(gain.shape, lambda j: (0, 0)),
                  pl.BlockSpec((D_MODEL, tn), lambda j: (0, j))],
        out_specs=pl.BlockSpec((n, tn), lambda j: (0, j)),
        out_shape=jax.ShapeDtypeStruct((n, dout), F32),
        compiler_params=_params(("parallel",), 32),
        name="norm_matmul",
    )(x, gain, w)


def _matmul_residual_kernel(y_ref, w_ref, x_ref, o_ref):
    o_ref[...] = x_ref[...] + _dot(y_ref[...].astype(BF16), w_ref[...])


def _matmul_residual(y, w, x):
    n = x.shape[0]
    tm = min(n, 512)
    assert n % tm == 0
    return pl.pallas_call(
        _matmul_residual_kernel,
        grid=(n // tm,),
        in_specs=[pl.BlockSpec((tm, y.shape[1]), lambda i: (i, 0)),
                  _resident(w.shape),
                  pl.BlockSpec((tm, D_MODEL), lambda i: (i, 0))],
        out_specs=pl.BlockSpec((tm, D_MODEL), lambda i: (i, 0)),
        out_shape=jax.ShapeDtypeStruct(x.shape, F32),
        compiler_params=_params(("parallel",), 32),
        name="matmul_residual",
    )(y, w, x)


def _rope_halves(x, cos, sin):
    half = cos.shape[-1]
    x1, x2 = x[:, :half], x[:, half:]
    return jnp.concatenate([x1 * cos - x2 * sin, x1 * sin + x2 * cos], axis=-1)


def _ret_chunk(q, k, v, s, dmat, qdec, kdec, sdec):
    vb = v.astype(BF16)
    scores = _dot_nt(q.astype(BF16), k.astype(BF16)) * dmat
    o = _dot(scores.astype(BF16), vb) + _dot((q * qdec).astype(BF16), s.astype(BF16))
    s_new = sdec * s + _dot((k * kdec).T.astype(BF16), vb)
    return o, s_new


def _group_norm_gate(o, gate, gain):
    mu = jnp.mean(o, axis=-1, keepdims=True)
    oc = o - mu
    var = jnp.mean(oc * oc, axis=-1, keepdims=True)
    on = oc * lax.rsqrt(var + NORM_EPS)
    return (gate * jax.nn.sigmoid(gate)) * on * gain


def _ret_prompt_kernel(x_ref, g_ref, win_ref, cos_ref, sin_ref, dmat_ref, qdec_ref,
                       kdec_ref, sdec_ref, gng_ref, wout_ref, o_ref, st_ref,
                       s_ref, y_ref):
    j = pl.program_id(1)

    @pl.when(j == 0)
    def _():
        s_ref[...] = jnp.zeros_like(s_ref)

    x = x_ref[0]
    tm = x.shape[0]
    h = _rms(x, g_ref[...]).astype(BF16)
    for hh in range(RET_HEADS):
        qs = slice(hh * RET_QK_DIM, (hh + 1) * RET_QK_DIM)
        ks = slice(D_MODEL + hh * RET_QK_DIM, D_MODEL + (hh + 1) * RET_QK_DIM)
        vs = slice(2 * D_MODEL + hh * RET_V_DIM, 2 * D_MODEL + (hh + 1) * RET_V_DIM)
        gs = slice(4 * D_MODEL + hh * RET_V_DIM, 4 * D_MODEL + (hh + 1) * RET_V_DIM)
        q = _rope_halves(_dot(h, win_ref[:, qs]), cos_ref[...], sin_ref[...]) * (RET_QK_DIM ** -0.5)
        k = _rope_halves(_dot(h, win_ref[:, ks]), cos_ref[...], sin_ref[...])
        v = _dot(h, win_ref[:, vs])
        gate = _dot(h, win_ref[:, gs])
        gain = gng_ref[:, hh * RET_V_DIM:(hh + 1) * RET_V_DIM]
        for c in range(tm // RET_CHUNK):
            rows = slice(c * RET_CHUNK, (c + 1) * RET_CHUNK)
            o, s_new = _ret_chunk(q[rows], k[rows], v[rows], s_ref[hh], dmat_ref[hh],
                                  qdec_ref[hh], kdec_ref[hh], sdec_ref[hh])
            s_ref[hh] = s_new
            y_ref[rows, hh * RET_V_DIM:(hh + 1) * RET_V_DIM] = (
                _group_norm_gate(o, gate[rows], gain).astype(BF16))
    o_ref[0] = x + _dot(y_ref[...], wout_ref[...])

    @pl.when(j == pl.num_programs(1) - 1)
    def _():
        st_ref[0] = s_ref[...]


def _ret_tables(length, log_g):
    idx = jnp.arange(RET_CHUNK, dtype=F32)
    valid = idx < length
    rel = idx[:, None] - idx[None, :]
    ok = (rel >= 0) & valid[:, None] & valid[None, :]
    dmat = jnp.where(ok[None], jnp.exp(jnp.maximum(rel, 0.0)[None] * log_g[:, None, None]), 0.0)
    qd = jnp.where(valid[None], jnp.exp((idx + 1.0)[None, :] * log_g[:, None]), 0.0)
    kd = jnp.where(valid[None], jnp.exp((length - 1.0 - idx)[None, :] * log_g[:, None]), 0.0)
    qdec = jnp.broadcast_to(qd[:, :, None], (RET_HEADS, RET_CHUNK, RET_QK_DIM))
    kdec = jnp.broadcast_to(kd[:, :, None], (RET_HEADS, RET_CHUNK, RET_QK_DIM))
    sdec = jnp.broadcast_to(jnp.exp(length * log_g)[:, None, None], (RET_HEADS, 1, RET_V_DIM))
    return dmat, qdec, kdec, sdec


def _ret_prompt(x, gain, w_in, cos, sin, tables, gn_gain, w_out):
    b, t, _ = x.shape
    tm = 256
    assert t % tm == 0
    dmat, qdec, kdec, sdec = tables
    return pl.pallas_call(
        _ret_prompt_kernel,
        grid=(b, t // tm),
        in_specs=[pl.BlockSpec((1, tm, D_MODEL), lambda i, j: (i, j, 0)),
                  _resident(gain.shape), _resident(w_in.shape),
                  pl.BlockSpec((tm, RET_QK_DIM // 2), lambda i, j: (j, 0)),
                  pl.BlockSpec((tm, RET_QK_DIM // 2), lambda i, j: (j, 0)),
                  _resident(dmat.shape), _resident(qdec.shape), _resident(kdec.shape),
                  _resident(sdec.shape), _resident(gn_gain.shape), _resident(w_out.shape)],
        out_specs=[pl.BlockSpec((1, tm, D_MODEL), lambda i, j: (i, j, 0)),
                   pl.BlockSpec((1, RET_HEADS, RET_QK_DIM, RET_V_DIM), lambda i, j: (i, 0, 0, 0))],
        out_shape=[jax.ShapeDtypeStruct(x.shape, F32),
                   jax.ShapeDtypeStruct((b, RET_HEADS, RET_QK_DIM, RET_V_DIM), F32)],
        scratch_shapes=[pltpu.VMEM((RET_HEADS, RET_QK_DIM, RET_V_DIM), F32),
                        pltpu.VMEM((tm, RET_HEADS * RET_V_DIM), BF16)],
        compiler_params=_params(("parallel", "arbitrary"), 56),
        name="ret_prompt",
    )(x, gain, w_in, cos, sin, dmat, qdec, kdec, sdec, gn_gain, w_out)


def _ret_sample_kernel(p_ref, s0_ref, cos_ref, sin_ref, dmat_ref, qdec_ref, kdec_ref,
                       sdec_ref, gng_ref, y_ref, st_ref, qp_ref, kp_ref, vp_ref):
    @pl.when(pl.program_id(0) == 0)
    def _():
        qp_ref[...] = jnp.zeros_like(qp_ref)
        kp_ref[...] = jnp.zeros_like(kp_ref)
        vp_ref[...] = jnp.zeros_like(vp_ref)

    t = p_ref.shape[1]
    for hh in range(RET_HEADS):
        qs = slice(hh * RET_QK_DIM, (hh + 1) * RET_QK_DIM)
        ks = slice(D_MODEL + hh * RET_QK_DIM, D_MODEL + (hh + 1) * RET_QK_DIM)
        vs = slice(2 * D_MODEL + hh * RET_V_DIM, 2 * D_MODEL + (hh + 1) * RET_V_DIM)
        gs = slice(4 * D_MODEL + hh * RET_V_DIM, 4 * D_MODEL + (hh + 1) * RET_V_DIM)
        qp_ref[0:t, :] = _rope_halves(p_ref[0, :, qs], cos_ref[...], sin_ref[...]) * (RET_QK_DIM ** -0.5)
        kp_ref[0:t, :] = _rope_halves(p_ref[0, :, ks], cos_ref[...], sin_ref[...])
        vp_ref[0:t, :] = p_ref[0, :, vs]
        o, s_new = _ret_chunk(qp_ref[...], kp_ref[...], vp_ref[...], s0_ref[0, hh], dmat_ref[hh],
                              qdec_ref[hh], kdec_ref[hh], sdec_ref[hh])
        st_ref[0, hh] = s_new
        gain = gng_ref[:, hh * RET_V_DIM:(hh + 1) * RET_V_DIM]
        y_ref[0, :, hh * RET_V_DIM:(hh + 1) * RET_V_DIM] = _group_norm_gate(o[0:t], p_ref[0, :, gs], gain)


def _ret_sample(proj, s0, cos, sin, tables, gn_gain):
    b, t, _ = proj.shape
    dmat, qdec, kdec, sdec = tables
    state_spec = pl.BlockSpec((1, RET_HEADS, RET_QK_DIM, RET_V_DIM), lambda i: (i, 0, 0, 0))
    return pl.pallas_call(
        _ret_sample_kernel,
        grid=(b,),
        in_specs=[pl.BlockSpec((1, t, proj.shape[2]), lambda i: (i, 0, 0)),
                  state_spec,
                  _resident(cos.shape), _resident(sin.shape),
                  _resident(dmat.shape), _resident(qdec.shape), _resident(kdec.shape),
                  _resident(sdec.shape), _resident(gn_gain.shape)],
        out_specs=[pl.BlockSpec((1, t, RET_HEADS * RET_V_DIM), lambda i: (i, 0, 0)),
                   state_spec],
        out_shape=[jax.ShapeDtypeStruct((b, t, RET_HEADS * RET_V_DIM), F32),
                   jax.ShapeDtypeStruct(s0.shape, F32)],
        scratch_shapes=[pltpu.VMEM((RET_CHUNK, RET_QK_DIM), F32),
                        pltpu.VMEM((RET_CHUNK, RET_QK_DIM), F32),
                        pltpu.VMEM((RET_CHUNK, RET_V_DIM), F32)],
        compiler_params=_params(("arbitrary",), 32),
        name="ret_sample",
    )(proj, s0, cos, sin, dmat, qdec, kdec, sdec, gn_gain)


def _gelu_tanh(x):
    return x * (0.5 * (1.0 + jnp.tanh(0.7978845608028654 * (x + 0.044715 * (x * x * x)))))


def _layer_norm(v, gain):
    mu = jnp.mean(v, axis=-1, keepdims=True)
    vc = v - mu
    var = jnp.mean(vc * vc, axis=-1, keepdims=True)
    return vc * lax.rsqrt(var + NORM_EPS) * gain


def _cm_prompt_kernel(x_ref, g_ref, win_ref, lng_ref, ws_ref, bias_ref, wout_ref,
                      o_ref, v_ref, z_ref):
    x = x_ref[...]
    tm = x.shape[0]
    h = _rms(x, g_ref[...]).astype(BF16)
    u = _gelu_tanh(_dot(h, win_ref[:, :D_MODEL]))
    v = _layer_norm(_gelu_tanh(_dot(h, win_ref[:, D_MODEL:])), lng_ref[...])
    v_ref[...] = v[tm - CM_CHUNK:]
    vb = v.astype(BF16)
    row = lax.broadcasted_iota(jnp.int32, (CM_CHUNK, CM_CHUNK), 0)
    col = lax.broadcasted_iota(jnp.int32, (CM_CHUNK, CM_CHUNK), 1)
    for gi in range(CM_GROUPS):
        lanes = slice(gi * CM_GROUP_DIM, (gi + 1) * CM_GROUP_DIM)
        w = jnp.where(row >= col, ws_ref[gi], 0.0).astype(BF16)
        for c in range(tm // CM_CHUNK):
            rows = slice(c * CM_CHUNK, (c + 1) * CM_CHUNK)
            mixed = _dot(w, vb[rows, lanes]) + bias_ref[:, lanes]
            z_ref[rows, lanes] = (u[rows, lanes] * mixed).astype(BF16)
    o_ref[...] = x + _dot(z_ref[...], wout_ref[...])


def _cm_prompt(x, gain, w_in, ln_gain, w_s, bias_full, w_out, seq):
    n = x.shape[0]
    tm = 512
    assert seq % tm == 0 and n % seq == 0
    per_seq = seq // tm
    return pl.pallas_call(
        _cm_prompt_kernel,
        grid=(n // tm,),
        in_specs=[pl.BlockSpec((tm, D_MODEL), lambda i: (i, 0)),
                  _resident(gain.shape), _resident(w_in.shape), _resident(ln_gain.shape),
                  _resident(w_s.shape), _resident(bias_full.shape), _resident(w_out.shape)],
        out_specs=[pl.BlockSpec((tm, D_MODEL), lambda i: (i, 0)),
                   pl.BlockSpec((CM_CHUNK, D_MODEL), lambda i: (i // per_seq, 0))],
        out_shape=[jax.ShapeDtypeStruct(x.shape, F32),
                   jax.ShapeDtypeStruct((n // seq * CM_CHUNK, D_MODEL), F32)],
        scratch_shapes=[pltpu.VMEM((tm, D_MODEL), BF16)],
        compiler_params=_params(("arbitrary",), 40),
        name="cm_prompt",
    )(x, gain, w_in, ln_gain, w_s, bias_full, w_out)


def _cm_sample_kernel(ws_ref, bs_ref, x_ref, g_ref, win_ref, lng_ref, wout_ref, o_ref, v_ref):
    t_len = x_ref.shape[0]
    us, vs = [], []
    for t in range(t_len):
        h = _rms(x_ref[t], g_ref[...]).astype(BF16)
        us.append(_gelu_tanh(_dot(h, win_ref[:, :D_MODEL])))
        v = _layer_norm(_gelu_tanh(_dot(h, win_ref[:, D_MODEL:])), lng_ref[...])
        v_ref[t] = v
        vs.append(v)
    for t in range(t_len):
        parts = []
        for gi in range(CM_GROUPS):
            lanes = slice(gi * CM_GROUP_DIM, (gi + 1) * CM_GROUP_DIM)
            mixed = jnp.full_like(vs[t][:, lanes], bs_ref[gi, t])
            for s in range(t + 1):
                mixed = mixed + ws_ref[gi, t * t_len + s] * vs[s][:, lanes]
            parts.append(us[t][:, lanes] * mixed)
        z = jnp.concatenate(parts, axis=-1).astype(BF16)
        o_ref[t] = x_ref[t] + _dot(z, wout_ref[...])


def _cm_sample(x_t, gain, w_in, ln_gain, w_s_small, b_s_small, w_out):
    smem = pl.BlockSpec(memory_space=pltpu.SMEM)
    vmem = pl.BlockSpec(memory_space=pltpu.VMEM)
    return pl.pallas_call(
        _cm_sample_kernel,
        in_specs=[smem, smem, vmem, vmem, vmem, vmem, vmem],
        out_specs=[vmem, vmem],
        out_shape=[jax.ShapeDtypeStruct(x_t.shape, F32), jax.ShapeDtypeStruct(x_t.shape, F32)],
        compiler_params=_params(None, 32),
        name="cm_sample",
    )(w_s_small, b_s_small, x_t, gain, w_in, ln_gain, w_out)


def _mla_q_kernel(x_ref, g_ref, wd_ref, gq_ref, gkv_ref, wuq_ref, wuk_ref, cos_ref, sin_ref,
                  q_ref, c_ref, kr_ref, kext_ref):
    h = _rms(x_ref[0], g_ref[...]).astype(BF16)
    d = _dot(h, wd_ref[...])
    cq = _rms(d[:, :MLA_Q_LORA], gq_ref[...]).astype(BF16)
    ckv = _rms(d[:, MLA_Q_LORA:MLA_Q_LORA + MLA_KV_LORA], gkv_ref[...])
    cos, sin = cos_ref[...], sin_ref[...]
    base = MLA_Q_LORA + MLA_KV_LORA
    krp = d[:, base:base + 128] * cos + d[:, base + 128:base + 256] * sin
    c_ref[0] = ckv
    kr_ref[0] = krp[:, :MLA_ROPE]
    kext_ref[0] = jnp.concatenate([ckv.astype(BF16), krp.astype(BF16)], axis=-1)
    nope_w = MLA_HEADS * MLA_NOPE
    for hh in range(MLA_HEADS):
        qn = _dot(cq, wuq_ref[:, hh * MLA_NOPE:(hh + 1) * MLA_NOPE])
        raw = _dot(cq, wuq_ref[:, nope_w + hh * 128:nope_w + (hh + 1) * 128])
        rot = _dot(cq, wuq_ref[:, 2 * nope_w + hh * 128:2 * nope_w + (hh + 1) * 128])
        ql = _dot(qn.astype(BF16), wuk_ref[hh])
        qr = raw * cos + rot * sin
        q_ref[0, hh] = jnp.concatenate([ql.astype(BF16), qr.astype(BF16)], axis=-1)


def _mla_q(x, gain, wd_ext, g_q, g_kv, wuq_ext, wuk_t, cos2, sin2, tm):
    nb, t, _ = x.shape
    assert t % tm == 0
    return pl.pallas_call(
        _mla_q_kernel,
        grid=(nb, t // tm),
        in_specs=[pl.BlockSpec((1, tm, D_MODEL), lambda i, j: (i, j, 0)),
                  _resident(gain.shape), _resident(wd_ext.shape), _resident(g_q.shape),
                  _resident(g_kv.shape), _resident(wuq_ext.shape), _resident(wuk_t.shape),
                  pl.BlockSpec((tm, 128), lambda i, j: (j, 0)),
                  pl.BlockSpec((tm, 128), lambda i, j: (j, 0))],
        out_specs=[pl.BlockSpec((1, MLA_HEADS, tm, MLA_KEY_DIM), lambda i, j: (i, 0, j, 0)),
                   pl.BlockSpec((1, tm, MLA_KV_LORA), lambda i, j: (i, j, 0)),
                   pl.BlockSpec((1, tm, MLA_ROPE), lambda i, j: (i, j, 0)),
                   pl.BlockSpec((1, tm, MLA_KEY_DIM), lambda i, j: (i, j, 0))],
        out_shape=[jax.ShapeDtypeStruct((nb, MLA_HEADS, t, MLA_KEY_DIM), BF16),
                   jax.ShapeDtypeStruct((nb, t, MLA_KV_LORA), F32),
                   jax.ShapeDtypeStruct((nb, t, MLA_ROPE), F32),
                   jax.ShapeDtypeStruct((nb, t, MLA_KEY_DIM), BF16)],
        compiler_params=_params(("parallel", "parallel"), 40),
        name="mla_q",
    )(x, gain, wd_ext, g_q, g_kv, wuq_ext, wuk_t, cos2, sin2)


def _softmax_update(s, v, m_ref, l_ref, acc_ref):
    m_old = m_ref[...]
    m_new = jnp.maximum(m_old, jnp.max(s, axis=-1, keepdims=True))
    p = jnp.exp(s - m_new)
    alpha = jnp.exp(m_old - m_new)
    l_ref[...] = alpha * l_ref[...] + jnp.sum(p, axis=-1, keepdims=True)
    acc_ref[...] = alpha * acc_ref[...] + _dot(p.astype(BF16), v)
    m_ref[...] = m_new


def _mla_attn_prompt_kernel(q_ref, k_ref, o_ref, m_ref, l_ref, acc_ref, *, tq):
    qi = pl.program_id(1)
    q = q_ref[0].reshape(MLA_HEADS * tq, MLA_KEY_DIM)
    m_ref[...] = jnp.full_like(m_ref, -jnp.inf)
    l_ref[...] = jnp.zeros_like(l_ref)
    acc_ref[...] = jnp.zeros_like(acc_ref)
    qpos = qi * tq + (lax.broadcasted_iota(jnp.int32, (MLA_HEADS * tq, tq), 0) & (tq - 1))
    kofs = lax.broadcasted_iota(jnp.int32, (MLA_HEADS * tq, tq), 1)

    def body(kb, carry):
        k = k_ref[0, pl.ds(pl.multiple_of(kb * tq, tq), tq), :]
        s = _dot_nt(q, k) * MLA_SCALE
        s = jnp.where(kb * tq + kofs <= qpos, s, -jnp.inf)
        _softmax_update(s, k[:, :MLA_KV_LORA], m_ref, l_ref, acc_ref)
        return carry

    lax.fori_loop(0, qi + 1, body, 0)
    o = acc_ref[...] / l_ref[...]
    o_ref[0] = o.reshape(MLA_HEADS, tq, MLA_KV_LORA).astype(BF16)


def _mla_attn_prompt(q, kext):
    b, _, t, _ = q.shape
    tq = 256
    assert t % tq == 0 and tq & (tq - 1) == 0
    rows = MLA_HEADS * tq
    return pl.pallas_call(
        functools.partial(_mla_attn_prompt_kernel, tq=tq),
        grid=(b, t // tq),
        in_specs=[pl.BlockSpec((1, MLA_HEADS, tq, MLA_KEY_DIM), lambda i, j: (i, 0, j, 0)),
                  pl.BlockSpec((1, t, MLA_KEY_DIM), lambda i, j: (i, 0, 0))],
        out_specs=pl.BlockSpec((1, MLA_HEADS, tq, MLA_KV_LORA), lambda i, j: (i, 0, j, 0)),
        out_shape=jax.ShapeDtypeStruct((b, MLA_HEADS, t, MLA_KV_LORA), BF16),
        scratch_shapes=[pltpu.VMEM((rows, 1), F32), pltpu.VMEM((rows, 1), F32),
                        pltpu.VMEM((rows, MLA_KV_LORA), F32)],
        compiler_params=_params(("parallel", "arbitrary"), 40),
        name="mla_attn_prompt",
    )(q, kext)


def _mla_attn_sample_kernel(pt_ref, q_ref, *refs, t_new):
    np_ = MLA_PAGES_PER_STEP
    lat_refs, kr_refs = refs[:np_], refs[np_:2 * np_]
    knew_ref, o_ref, m_ref, l_ref, acc_ref = refs[2 * np_:]
    j = pl.program_id(1)

    @pl.when(j == 0)
    def _():
        m_ref[...] = jnp.full_like(m_ref, -jnp.inf)
        l_ref[...] = jnp.zeros_like(l_ref)
        acc_ref[...] = jnp.zeros_like(acc_ref)

    q = q_ref[0]
    ql = q[:, :MLA_KV_LORA]
    qr = q[:, MLA_KV_LORA:MLA_KV_LORA + MLA_ROPE]
    for p in range(np_):
        c = lat_refs[p][0].astype(BF16)
        r = kr_refs[p][0].astype(BF16)
        s = (_dot_nt(ql, c) + _dot_nt(qr, r)) * MLA_SCALE
        _softmax_update(s, c, m_ref, l_ref, acc_ref)

    @pl.when(j == pl.num_programs(1) - 1)
    def _():
        kn = knew_ref[0]
        s = _dot_nt(q, kn) * MLA_SCALE
        row_t = lax.broadcasted_iota(jnp.int32, s.shape, 0) % t_new
        key = lax.broadcasted_iota(jnp.int32, s.shape, 1)
        s = jnp.where((key < t_new) & (key <= row_t), s, -jnp.inf)
        _softmax_update(s, kn[:, :MLA_KV_LORA], m_ref, l_ref, acc_ref)
        o_ref[0] = acc_ref[...] / l_ref[...]


def _mla_attn_sample(page_table, q, cache_lat, cache_kr, knew, t_new):
    b, rows, _ = q.shape
    n_pages = page_table.shape[1]
    np_ = MLA_PAGES_PER_STEP
    assert n_pages % np_ == 0
    pt_flat = page_table.reshape(-1)

    def page_map(p):
        return lambda i, j, pt: (pt[i * n_pages + j * np_ + p], 0, 0)

    lat_specs = [pl.BlockSpec((1, PAGE_SIZE, MLA_KV_LORA), page_map(p)) for p in range(np_)]
    kr_specs = [pl.BlockSpec((1, PAGE_SIZE, MLA_ROPE), page_map(p)) for p in range(np_)]
    grid_spec = pltpu.PrefetchScalarGridSpec(
        num_scalar_prefetch=1,
        grid=(b, n_pages // np_),
        in_specs=[pl.BlockSpec((1, rows, MLA_KEY_DIM), lambda i, j, pt: (i, 0, 0))]
        + lat_specs + kr_specs
        + [pl.BlockSpec((1, PAGE_SIZE, MLA_KEY_DIM), lambda i, j, pt: (i, 0, 0))],
        out_specs=pl.BlockSpec((1, rows, MLA_KV_LORA), lambda i, j, pt: (i, 0, 0)),
        scratch_shapes=[pltpu.VMEM((rows, 1), F32), pltpu.VMEM((rows, 1), F32),
                        pltpu.VMEM((rows, MLA_KV_LORA), F32)],
    )
    return pl.pallas_call(
        functools.partial(_mla_attn_sample_kernel, t_new=t_new),
        grid_spec=grid_spec,
        out_shape=jax.ShapeDtypeStruct((b, rows, MLA_KV_LORA), F32),
        compiler_params=_params(("parallel", "arbitrary"), 32),
        name="mla_attn_sample",
    )(pt_flat, q, *([cache_lat] * np_), *([cache_kr] * np_), knew)


def _mla_out_kernel(o_ref, wuv_ref, wout_ref, x_ref, y_ref, cat_ref):
    for hh in range(MLA_HEADS):
        cat_ref[:, hh * MLA_V:(hh + 1) * MLA_V] = _dot(o_ref[0, hh], wuv_ref[hh]).astype(BF16)
    y_ref[0] = x_ref[0] + _dot(cat_ref[...], wout_ref[...])


def _mla_out(o_lat, wuv, w_out, x, tm):
    nb, t, _ = x.shape
    assert t % tm == 0
    return pl.pallas_call(
        _mla_out_kernel,
        grid=(nb, t // tm),
        in_specs=[pl.BlockSpec((1, MLA_HEADS, tm, MLA_KV_LORA), lambda i, j: (i, 0, j, 0)),
                  _resident(wuv.shape), _resident(w_out.shape),
                  pl.BlockSpec((1, tm, D_MODEL), lambda i, j: (i, j, 0))],
        out_specs=pl.BlockSpec((1, tm, D_MODEL), lambda i, j: (i, j, 0)),
        out_shape=jax.ShapeDtypeStruct(x.shape, F32),
        scratch_shapes=[pltpu.VMEM((tm, MLA_HEADS * MLA_V), BF16)],
        compiler_params=_params(("parallel", "parallel"), 32),
        name="mla_out",
    )(o_lat, wuv, w_out, x)


def _pool_prompt_kernel(x_ref, xp_ref, g_ref, w_ref, sc_ref, o_ref, hl_ref, ext_ref):
    j = pl.program_id(1)
    x = x_ref[0]
    tm = x.shape[0]
    h = _rms(x, g_ref[...])
    hp = _rms(xp_ref[0], g_ref[...])
    ext_ref[0:POOL_PREV, :] = jnp.where(j == 0, 0.0, hp)
    ext_ref[POOL_PREV:, :] = h
    pos = (j * tm + lax.broadcasted_iota(jnp.int32, (tm, 1), 0)).astype(F32)
    outs = []
    for gi, w in enumerate(POOL_WINDOWS):
        lanes = slice(gi * POOL_GROUP_DIM, (gi + 1) * POOL_GROUP_DIM)
        acc = h[:, lanes]
        for k in range(1, w):
            acc = acc + ext_ref[POOL_PREV - k:POOL_PREV - k + tm, lanes]
        pooled = acc / jnp.minimum(pos + 1.0, float(w)) - h[:, lanes]
        outs.append(_dot(pooled.astype(BF16), w_ref[gi]))
    o_ref[0] = x + jnp.concatenate(outs, axis=-1) * sc_ref[...]

    @pl.when(j == pl.num_programs(1) - 1)
    def _():
        hl_ref[0] = h[tm - POOL_PREV:]


def _pool_prompt(x, gain, w_pool, scale):
    b, t, _ = x.shape
    tm = 512
    assert t % tm == 0 and tm % POOL_PREV == 0
    per_tile = tm // POOL_PREV
    return pl.pallas_call(
        _pool_prompt_kernel,
        grid=(b, t // tm),
        in_specs=[pl.BlockSpec((1, tm, D_MODEL), lambda i, j: (i, j, 0)),
                  pl.BlockSpec((1, POOL_PREV, D_MODEL),
                               lambda i, j: (i, jnp.maximum(j * per_tile - 1, 0), 0)),
                  _resident(gain.shape), _resident(w_pool.shape), _resident(scale.shape)],
        out_specs=[pl.BlockSpec((1, tm, D_MODEL), lambda i, j: (i, j, 0)),
                   pl.BlockSpec((1, POOL_PREV, D_MODEL), lambda i, j: (i, 0, 0))],
        out_shape=[jax.ShapeDtypeStruct(x.shape, F32),
                   jax.ShapeDtypeStruct((b, POOL_PREV, D_MODEL), F32)],
        scratch_shapes=[pltpu.VMEM((POOL_PREV + tm, D_MODEL), F32)],
        compiler_params=_params(("parallel", "arbitrary"), 32),
        name="pool_prompt",
    )(x, x, gain, w_pool, scale)


def _pool_sample_kernel(x_ref, pre_ref, g_ref, w_ref, sc_ref, o_ref, h_ref, *, pos0):
    t_len = x_ref.shape[0]
    hs = [_rms(x_ref[t], g_ref[...]) for t in range(t_len)]
    for t in range(t_len):
        h_ref[t] = hs[t]

    def ext(e, lanes):
        return pre_ref[e, :, lanes] if e < POOL_BUF else hs[e - POOL_BUF][:, lanes]

    for t in range(t_len):
        outs = []
        for gi, w in enumerate(POOL_WINDOWS):
            lanes = slice(gi * POOL_GROUP_DIM, (gi + 1) * POOL_GROUP_DIM)
            acc = hs[t][:, lanes]
            for k in range(1, w):
                acc = acc + ext(POOL_BUF + t - k, lanes)
            pooled = acc / min(pos0 + t + 1.0, float(w)) - hs[t][:, lanes]
            outs.append(_dot(pooled.astype(BF16), w_ref[gi]))
        o_ref[t] = x_ref[t] + jnp.concatenate(outs, axis=-1) * sc_ref[...]


def _pool_sample(x_t, prefix_t, gain, w_pool, scale, pos0):
    vmem = pl.BlockSpec(memory_space=pltpu.VMEM)
    return pl.pallas_call(
        functools.partial(_pool_sample_kernel, pos0=pos0),
        in_specs=[vmem] * 5,
        out_specs=[vmem, vmem],
        out_shape=[jax.ShapeDtypeStruct(x_t.shape, F32), jax.ShapeDtypeStruct(x_t.shape, F32)],
        compiler_params=_params(None, 32),
        name="pool_sample",
    )(x_t, prefix_t, gain, w_pool, scale)


def _rope_tables(pos, half):
    inv = ROPE_BASE ** (-jnp.arange(half, dtype=F32) / half)
    ang = pos.astype(F32)[:, None] * inv[None, :]
    return jnp.cos(ang), jnp.sin(ang)


def _mla_rope_tables(pos):
    cos, sin = _rope_tables(pos, MLA_ROPE // 2)
    zeros = jnp.zeros((pos.shape[0], 128 - MLA_ROPE), F32)
    return (jnp.concatenate([cos, cos, zeros], axis=-1),
            jnp.concatenate([-sin, sin, zeros], axis=-1))


def _swap_halves(w):
    half = w.shape[-1] // 2
    return jnp.concatenate([w[..., half:], w[..., :half]], axis=-1)


def _pad_lanes(w, width):
    return jnp.pad(w, [(0, 0)] * (w.ndim - 1) + [(0, width - w.shape[-1])])


def _row(v):
    return v.reshape(1, -1).astype(F32)


def kernel(x_prompt, x_sample, state_ret, cache_mla_latent, cache_mla_krope, page_table, state_pool,
           norm_mix, norm_ffn, norm_final,
           ret_w_in, ret_gn_gain, ret_w_out,
           cm_w_in, cm_ln_gain, cm_w_spatial, cm_b_spatial, cm_w_out,
           mla_w_down, mla_q_norm, mla_kv_norm, mla_w_uq, mla_w_uk, mla_w_uv, mla_w_out,
           pool_w, pool_scale,
           ffn_w_gate_up, ffn_w_down):
    bp, tp, _ = x_prompt.shape
    bs, ts, _ = x_sample.shape
    xp = x_prompt
    xs = x_sample.reshape(bs * ts, D_MODEL)
    pos_p = jnp.arange(tp)
    pos_s = PAST_LEN + jnp.arange(ts)

    def ffn(x, i, final):
        n_chunks = FFN_HIDDEN // FFN_CHUNK
        wgu = ffn_w_gate_up[i].astype(BF16)
        wg = wgu[:, :FFN_HIDDEN].reshape(D_MODEL, n_chunks, FFN_CHUNK).transpose(1, 0, 2)
        wu = wgu[:, FFN_HIDDEN:].reshape(D_MODEL, n_chunks, FFN_CHUNK).transpose(1, 0, 2)
        wd = ffn_w_down[i].astype(BF16).reshape(n_chunks, FFN_CHUNK, D_MODEL)
        shape = x.shape
        y = _ffn(x.reshape(-1, D_MODEL), _row(norm_ffn[i]), wg, wu, wd, _row(norm_final), final)
        return y.reshape(shape)

    log_g = jnp.log1p(-jnp.exp2(-5.0 - jnp.arange(RET_HEADS, dtype=F32)))
    w_in = ret_w_in[0].astype(BF16)
    w_out = ret_w_out[0].astype(BF16)
    gn_gain = _row(ret_gn_gain[0])
    cos_p, sin_p = _rope_tables(pos_p, RET_QK_DIM // 2)
    cos_s, sin_s = _rope_tables(pos_s, RET_QK_DIM // 2)
    xp, ret_state_p = _ret_prompt(xp, _row(norm_mix[0]), w_in, cos_p, sin_p,
                                  _ret_tables(float(RET_CHUNK), log_g), gn_gain, w_out)
    proj_s = _norm_matmul(xs, _row(norm_mix[0]), w_in, 1024).reshape(bs, ts, -1)
    y_s, ret_state_s = _ret_sample(proj_s, state_ret[0], cos_s, sin_s,
                                   _ret_tables(float(ts), log_g), gn_gain)
    xs = _matmul_residual(y_s.reshape(bs * ts, -1), w_out, xs)
    xp, xs = ffn(xp, 0, False), ffn(xs, 0, False)

    cw_in = cm_w_in[0].astype(BF16)
    cw_out = cm_w_out[0].astype(BF16)
    bias_full = jnp.repeat(jnp.transpose(cm_b_spatial[0]), CM_GROUP_DIM, axis=1)
    xp2, cm_v_p = _cm_prompt(xp.reshape(bp * tp, D_MODEL), _row(norm_mix[1]), cw_in,
                             _row(cm_ln_gain[0]), cm_w_spatial[0], bias_full, cw_out, tp)
    xp = xp2.reshape(bp, tp, D_MODEL)
    xs_t, cm_v_s_t = _cm_sample(xs.reshape(bs, ts, D_MODEL).transpose(1, 0, 2), _row(norm_mix[1]), cw_in,
                                _row(cm_ln_gain[0]),
                                cm_w_spatial[0][:, :ts, :ts].reshape(CM_GROUPS, ts * ts),
                                cm_b_spatial[0][:, :ts], cw_out)
    xs = xs_t.transpose(1, 0, 2).reshape(bs * ts, D_MODEL)
    cm_v_s = cm_v_s_t.transpose(1, 0, 2)
    xp, xs = ffn(xp, 1, False), ffn(xs, 1, False)

    wd = mla_w_down[0]
    kr_w = wd[:, MLA_Q_LORA + MLA_KV_LORA:]
    wd_ext = jnp.concatenate([wd[:, :MLA_Q_LORA + MLA_KV_LORA], _pad_lanes(kr_w, 128),
                              _pad_lanes(_swap_halves(kr_w), 128)], axis=-1).astype(BF16)
    wuq = mla_w_uq[0].reshape(MLA_Q_LORA, MLA_HEADS, MLA_NOPE + MLA_ROPE)
    wuq_rope = wuq[:, :, MLA_NOPE:]
    wuq_ext = jnp.concatenate([
        wuq[:, :, :MLA_NOPE].reshape(MLA_Q_LORA, -1),
        _pad_lanes(wuq_rope, 128).reshape(MLA_Q_LORA, -1),
        _pad_lanes(_swap_halves(wuq_rope), 128).reshape(MLA_Q_LORA, -1)], axis=-1).astype(BF16)
    wuk_t = mla_w_uk[0].transpose(1, 2, 0).astype(BF16)
    wuv = mla_w_uv[0].transpose(1, 0, 2).astype(BF16)
    mw_out = mla_w_out[0].astype(BF16)
    cos2_p, sin2_p = _mla_rope_tables(pos_p)
    cos2_s, sin2_s = _mla_rope_tables(jnp.tile(pos_s, bs))
    gq, gkv = _row(mla_q_norm[0]), _row(mla_kv_norm[0])
    q_p, lat_p, kr_p, kext_p = _mla_q(xp, _row(norm_mix[2]), wd_ext, gq, gkv, wuq_ext, wuk_t,
                                      cos2_p, sin2_p, 256)
    o_p = _mla_attn_prompt(q_p, kext_p)
    xp = _mla_out(o_p, wuv, mw_out, xp, 256)
    ns = bs * ts
    q_s, lat_s, kr_s, kext_s = _mla_q(xs.reshape(1, ns, D_MODEL), _row(norm_mix[2]), wd_ext, gq, gkv,
                                      wuq_ext, wuk_t, cos2_s, sin2_s, ns)
    q_s = q_s.reshape(MLA_HEADS, bs, ts, MLA_KEY_DIM).transpose(1, 0, 2, 3).reshape(bs, MLA_HEADS * ts, MLA_KEY_DIM)
    knew = jnp.pad(kext_s.reshape(bs, ts, MLA_KEY_DIM), ((0, 0), (0, PAGE_SIZE - ts), (0, 0)))
    o_s = _mla_attn_sample(page_table, q_s, cache_mla_latent[0], cache_mla_krope[0], knew, ts)
    o_s = o_s.reshape(bs, MLA_HEADS, ts, MLA_KV_LORA).transpose(1, 0, 2, 3).reshape(1, MLA_HEADS, ns, MLA_KV_LORA)
    xs = _mla_out(o_s.astype(BF16), wuv, mw_out, xs.reshape(1, ns, D_MODEL), ns).reshape(ns, D_MODEL)
    lat_s = lat_s.reshape(bs, ts, MLA_KV_LORA)
    kr_s = kr_s.reshape(bs, ts, MLA_ROPE)
    xp, xs = ffn(xp, 2, False), ffn(xs, 2, False)

    pw = pool_w[0].astype(BF16)
    xp, h_last = _pool_prompt(xp, _row(norm_mix[3]), pw, _row(pool_scale[0]))
    pool_state_p = h_last[:, POOL_PREV - POOL_BUF:]
    xs_t, h_s_t = _pool_sample(xs.reshape(bs, ts, D_MODEL).transpose(1, 0, 2),
                               state_pool[0].transpose(1, 0, 2), _row(norm_mix[3]), pw,
                               _row(pool_scale[0]), float(PAST_LEN))
    xs = xs_t.transpose(1, 0, 2).reshape(bs * ts, D_MODEL)
    pool_state_s = jnp.concatenate([state_pool[0], h_s_t.transpose(1, 0, 2)], axis=1)[:, -POOL_BUF:]
    xp, xs = ffn(xp, 3, True), ffn(xs, 3, True)

    return (xp, xs.reshape(bs, ts, D_MODEL),
            ret_state_p[None], ret_state_s[None],
            cm_v_p.reshape(bp, CM_CHUNK, D_MODEL)[None], cm_v_s[None],
            lat_p[None], kr_p[None], lat_s[None], kr_s[None],
            pool_state_p[None], pool_state_s[None])
```

```python
import functools

import jax
import jax.numpy as jnp
from jax import lax
from jax.experimental import pallas as pl
from jax.experimental.pallas import tpu as pltpu

F32 = jnp.float32
BF16 = jnp.bfloat16

D_MODEL = 1024
NORM_EPS = 1e-6
ROPE_BASE = 10000.0
PAST_LEN = 8192
PAGE_SIZE = 128

RET_HEADS = 4
RET_QK_DIM = 256
RET_V_DIM = 512
RET_CHUNK = 128

CM_CHUNK = 128
CM_GROUPS = 4
CM_GROUP_DIM = 256

MLA_HEADS = 8
MLA_Q_LORA = 384
MLA_KV_LORA = 256
MLA_NOPE = 128
MLA_ROPE = 64
MLA_V = 128
MLA_SCALE = (MLA_NOPE + MLA_ROPE) ** -0.5
MLA_KEY_DIM = MLA_KV_LORA + 128
MLA_PAGES_PER_STEP = 8

POOL_WINDOWS = (2, 4, 8, 16)
POOL_GROUP_DIM = 256
POOL_BUF = 15
POOL_PREV = 16

FFN_HIDDEN = 2816
FFN_CHUNK = 256

V7X_VMEM_BYTES = 64 * 1024 * 1024
MIB = 1024 * 1024


def _params(semantics, vmem_mib):
    assert vmem_mib * MIB < V7X_VMEM_BYTES
    return pltpu.CompilerParams(dimension_semantics=semantics,
                                vmem_limit_bytes=vmem_mib * MIB)


def _resident(shape):
    nd = len(shape)
    return pl.BlockSpec(shape, lambda *_: (0,) * nd, pipeline_mode=pl.Buffered(1))


def _rms(x, g):
    return x * lax.rsqrt(jnp.mean(x * x, axis=-1, keepdims=True) + NORM_EPS) * g


def _dot(a, b):
    return jnp.dot(a, b, preferred_element_type=F32)


def _dot_nt(a, b):
    return lax.dot_general(a, b, (((1,), (1,)), ((), ())), preferred_element_type=F32)


def _ffn_kernel(x_ref, g_ref, wg_ref, wu_ref, wd_ref, gf_ref, o_ref, *, final):
    x = x_ref[...]
    h = _rms(x, g_ref[...]).astype(BF16)
    o_ref[...] = x

    def body(j, carry):
        g = _dot(h, wg_ref[j])
        u = _dot(h, wu_ref[j])
        a = (g * jax.nn.sigmoid(g)) * u
        o_ref[...] += _dot(a.astype(BF16), wd_ref[j])
        return carry

    lax.fori_loop(0, wg_ref.shape[0], body, 0)
    if final:
        o_ref[...] = _rms(o_ref[...], gf_ref[...])


def _ffn(x, gain, wg, wu, wd, final_gain, final):
    n = x.shape[0]
    tm = min(n, 512)
    assert n % tm == 0
    return pl.pallas_call(
        functools.partial(_ffn_kernel, final=final),
        grid=(n // tm,),
        in_specs=[pl.BlockSpec((tm, D_MODEL), lambda i: (i, 0)),
                  _resident(gain.shape), _resident(wg.shape), _resident(wu.shape),
                  _resident(wd.shape), _resident(final_gain.shape)],
        out_specs=pl.BlockSpec((tm, D_MODEL), lambda i: (i, 0)),
        out_shape=jax.ShapeDtypeStruct(x.shape, F32),
        compiler_params=_params(("parallel",), 48),
        name="ffn",
    )(x, gain, wg, wu, wd, final_gain)


def _norm_matmul_kernel(x_ref, g_ref, w_ref, o_ref):
    h = _rms(x_ref[...], g_ref[...]).astype(BF16)
    o_ref[...] = _dot(h, w_ref[...])


def _norm_matmul(x, gain, w, tn):
    n, dout = x.shape[0], w.shape[1]
    assert dout % tn == 0
    return pl.pallas_call(
        _norm_matmul_kernel,
        grid=(dout // tn,),
        in_specs=[pl.BlockSpec((n, D_MODEL), lambda j: (0, 0)),
                  pl.BlockSpec(gain.shape, lambda j: (0, 0)),
                  pl.BlockSpec((D_MODEL, tn), lambda j: (0, j))],
        out_specs=pl.BlockSpec((n, tn), lambda j: (0, j)),
        out_shape=jax.ShapeDtypeStruct((n, dout), F32),
        compiler_params=_params(("parallel",), 32),
        name="norm_matmul",
    )(x, gain, w)


def _matmul_residual_kernel(y_ref, w_ref, x_ref, o_ref):
    o_ref[...] = x_ref[...] + _dot(y_ref[...].astype(BF16), w_ref[...])


def _matmul_residual(y, w, x):
    n = x.shape[0]
    tm = min(n, 512)
    assert n % tm == 0
    return pl.pallas_call(
        _matmul_residual_kernel,
        grid=(n // tm,),
        in_specs=[pl.BlockSpec((tm, y.shape[1]), lambda i: (i, 0)),
                  _resident(w.shape),
                  pl.BlockSpec((tm, D_MODEL), lambda i: (i, 0))],
        out_specs=pl.BlockSpec((tm, D_MODEL), lambda i: (i, 0)),
        out_shape=jax.ShapeDtypeStruct(x.shape, F32),
        compiler_params=_params(("parallel",), 32),
        name="matmul_residual",
    )(y, w, x)


def _rope_halves(x, cos, sin):
    half = cos.shape[-1]
    x1, x2 = x[:, :half], x[:, half:]
    return jnp.concatenate([x1 * cos - x2 * sin, x1 * sin + x2 * cos], axis=-1)


def _ret_chunk(q, k, v, s, dmat, qdec, kdec, sdec):
    vb = v.astype(BF16)
    scores = _dot_nt(q.astype(BF16), k.astype(BF16)) * dmat
    o = _dot(scores.astype(BF16), vb) + _dot((q * qdec).astype(BF16), s.astype(BF16))
    s_new = sdec * s + _dot((k * kdec).T.astype(BF16), vb)
    return o, s_new


def _group_norm_gate(o, gate, gain):
    mu = jnp.mean(o, axis=-1, keepdims=True)
    oc = o - mu
    var = jnp.mean(oc * oc, axis=-1, keepdims=True)
    on = oc * lax.rsqrt(var + NORM_EPS)
    return (gate * jax.nn.sigmoid(gate)) * on * gain


def _ret_prompt_kernel(x_ref, g_ref, win_ref, cos_ref, sin_ref, dmat_ref, qdec_ref,
                       kdec_ref, sdec_ref, gng_ref, wout_ref, o_ref, st_ref,
                       s_ref, y_ref):
    j = pl.program_id(1)

    @pl.when(j == 0)
    def _():
        s_ref[...] = jnp.zeros_like(s_ref)

    x = x_ref[0]
    tm = x.shape[0]
    h = _rms(x, g_ref[...]).astype(BF16)
    for hh in range(RET_HEADS):
        qs = slice(hh * RET_QK_DIM, (hh + 1) * RET_QK_DIM)
        ks = slice(D_MODEL + hh * RET_QK_DIM, D_MODEL + (hh + 1) * RET_QK_DIM)
        vs = slice(2 * D_MODEL + hh * RET_V_DIM, 2 * D_MODEL + (hh + 1) * RET_V_DIM)
        gs = slice(4 * D_MODEL + hh * RET_V_DIM, 4 * D_MODEL + (hh + 1) * RET_V_DIM)
        q = _rope_halves(_dot(h, win_ref[:, qs]), cos_ref[...], sin_ref[...]) * (RET_QK_DIM ** -0.5)
        k = _rope_halves(_dot(h, win_ref[:, ks]), cos_ref[...], sin_ref[...])
        v = _dot(h, win_ref[:, vs])
        gate = _dot(h, win_ref[:, gs])
        gain = gng_ref[:, hh * RET_V_DIM:(hh + 1) * RET_V_DIM]
        for c in range(tm // RET_CHUNK):
            rows = slice(c * RET_CHUNK, (c + 1) * RET_CHUNK)
            o, s_new = _ret_chunk(q[rows], k[rows], v[rows], s_ref[hh], dmat_ref[hh],
                                  qdec_ref[hh], kdec_ref[hh], sdec_ref[hh])
            s_ref[hh] = s_new
            y_ref[rows, hh * RET_V_DIM:(hh + 1) * RET_V_DIM] = (
                _group_norm_gate(o, gate[rows], gain).astype(BF16))
    o_ref[0] = x + _dot(y_ref[...], wout_ref[...])

    @pl.when(j == pl.num_programs(1) - 1)
    def _():
        st_ref[0] = s_ref[...]


def _ret_tables(length, log_g):
    idx = jnp.arange(RET_CHUNK, dtype=F32)
    valid = idx < length
    rel = idx[:, None] - idx[None, :]
    ok = (rel >= 0) & valid[:, None] & valid[None, :]
    dmat = jnp.where(ok[None], jnp.exp(jnp.maximum(rel, 0.0)[None] * log_g[:, None, None]), 0.0)
    qd = jnp.where(valid[None], jnp.exp((idx + 1.0)[None, :] * log_g[:, None]), 0.0)
    kd = jnp.where(valid[None], jnp.exp((length - 1.0 - idx)[None, :] * log_g[:, None]), 0.0)
    qdec = jnp.broadcast_to(qd[:, :, None], (RET_HEADS, RET_CHUNK, RET_QK_DIM))
    kdec = jnp.broadcast_to(kd[:, :, None], (RET_HEADS, RET_CHUNK, RET_QK_DIM))
    sdec = jnp.broadcast_to(jnp.exp(length * log_g)[:, None, None], (RET_HEADS, 1, RET_V_DIM))
    return dmat, qdec, kdec, sdec


def _ret_prompt(x, gain, w_in, cos, sin, tables, gn_gain, w_out):
    b, t, _ = x.shape
    tm = 256
    assert t % tm == 0
    dmat, qdec, kdec, sdec = tables
    return pl.pallas_call(
        _ret_prompt_kernel,
        grid=(b, t // tm),
        in_specs=[pl.BlockSpec((1, tm, D_MODEL), lambda i, j: (i, j, 0)),
                  _resident(gain.shape), _resident(w_in.shape),
                  pl.BlockSpec((tm, RET_QK_DIM // 2), lambda i, j: (j, 0)),
                  pl.BlockSpec((tm, RET_QK_DIM // 2), lambda i, j: (j, 0)),
                  _resident(dmat.shape), _resident(qdec.shape), _resident(kdec.shape),
                  _resident(sdec.shape), _resident(gn_gain.shape), _resident(w_out.shape)],
        out_specs=[pl.BlockSpec((1, tm, D_MODEL), lambda i, j: (i, j, 0)),
                   pl.BlockSpec((1, RET_HEADS, RET_QK_DIM, RET_V_DIM), lambda i, j: (i, 0, 0, 0))],
        out_shape=[jax.ShapeDtypeStruct(x.shape, F32),
                   jax.ShapeDtypeStruct((b, RET_HEADS, RET_QK_DIM, RET_V_DIM), F32)],
        scratch_shapes=[pltpu.VMEM((RET_HEADS, RET_QK_DIM, RET_V_DIM), F32),
                        pltpu.VMEM((tm, RET_HEADS * RET_V_DIM), BF16)],
        compiler_params=_params(("parallel", "arbitrary"), 56),
        name="ret_prompt",
    )(x, gain, w_in, cos, sin, dmat, qdec, kdec, sdec, gn_gain, w_out)


def _ret_sample_kernel(p_ref, s0_ref, cos_ref, sin_ref, dmat_ref, qdec_ref, kdec_ref,
                       sdec_ref, gng_ref, y_ref, st_ref, qp_ref, kp_ref, vp_ref):
    @pl.when(pl.program_id(0) == 0)
    def _():
        qp_ref[...] = jnp.zeros_like(qp_ref)
        kp_ref[...] = jnp.zeros_like(kp_ref)
        vp_ref[...] = jnp.zeros_like(vp_ref)

    t = p_ref.shape[1]
    for hh in range(RET_HEADS):
        qs = slice(hh * RET_QK_DIM, (hh + 1) * RET_QK_DIM)
        ks = slice(D_MODEL + hh * RET_QK_DIM, D_MODEL + (hh + 1) * RET_QK_DIM)
        vs = slice(2 * D_MODEL + hh * RET_V_DIM, 2 * D_MODEL + (hh + 1) * RET_V_DIM)
        gs = slice(4 * D_MODEL + hh * RET_V_DIM, 4 * D_MODEL + (hh + 1) * RET_V_DIM)
        qp_ref[0:t, :] = _rope_halves(p_ref[0, :, qs], cos_ref[...], sin_ref[...]) * (RET_QK_DIM ** -0.5)
        kp_ref[0:t, :] = _rope_halves(p_ref[0, :, ks], cos_ref[...], sin_ref[...])
        vp_ref[0:t, :] = p_ref[0, :, vs]
        o, s_new = _ret_chunk(qp_ref[...], kp_ref[...], vp_ref[...], s0_ref[0, hh], dmat_ref[hh],
                              qdec_ref[hh], kdec_ref[hh], sdec_ref[hh])
        st_ref[0, hh] = s_new
        gain = gng_ref[:, hh * RET_V_DIM:(hh + 1) * RET_V_DIM]
        y_ref[0, :, hh * RET_V_DIM:(hh + 1) * RET_V_DIM] = _group_norm_gate(o[0:t], p_ref[0, :, gs], gain)


def _ret_sample(proj, s0, cos, sin, tables, gn_gain):
    b, t, _ = proj.shape
    dmat, qdec, kdec, sdec = tables
    state_spec = pl.BlockSpec((1, RET_HEADS, RET_QK_DIM, RET_V_DIM), lambda i: (i, 0, 0, 0))
    return pl.pallas_call(
        _ret_sample_kernel,
        grid=(b,),
        in_specs=[pl.BlockSpec((1, t, proj.shape[2]), lambda i: (i, 0, 0)),
                  state_spec,
                  _resident(cos.shape), _resident(sin.shape),
                  _resident(dmat.shape), _resident(qdec.shape), _resident(kdec.shape),
                  _resident(sdec.shape), _resident(gn_gain.shape)],
        out_specs=[pl.BlockSpec((1, t, RET_HEADS * RET_V_DIM), lambda i: (i, 0, 0)),
                   state_spec],
        out_shape=[jax.ShapeDtypeStruct((b, t, RET_HEADS * RET_V_DIM), F32),
                   jax.ShapeDtypeStruct(s0.shape, F32)],
        scratch_shapes=[pltpu.VMEM((RET_CHUNK, RET_QK_DIM), F32),
                        pltpu.VMEM((RET_CHUNK, RET_QK_DIM), F32),
                        pltpu.VMEM((RET_CHUNK, RET_V_DIM), F32)],
        compiler_params=_params(("arbitrary",), 32),
        name="ret_sample",
    )(proj, s0, cos, sin, dmat, qdec, kdec, sdec, gn_gain)


def _gelu_tanh(x):
    return x * (0.5 * (1.0 + jnp.tanh(0.7978845608028654 * (x + 0.044715 * (x * x * x)))))


def _layer_norm(v, gain):
    mu = jnp.mean(v, axis=-1, keepdims=True)
    vc = v - mu
    var = jnp.mean(vc * vc, axis=-1, keepdims=True)
    return vc * lax.rsqrt(var + NORM_EPS) * gain


def _cm_prompt_kernel(x_ref, g_ref, win_ref, lng_ref, ws_ref, bias_ref, wout_ref,
                      o_ref, v_ref, z_ref):
    x = x_ref[...]
    tm = x.shape[0]
    h = _rms(x, g_ref[...]).astype(BF16)
    u = _gelu_tanh(_dot(h, win_ref[:, :D_MODEL]))
    v = _layer_norm(_gelu_tanh(_dot(h, win_ref[:, D_MODEL:])), lng_ref[...])
    v_ref[...] = v[tm - CM_CHUNK:]
    vb = v.astype(BF16)
    row = lax.broadcasted_iota(jnp.int32, (CM_CHUNK, CM_CHUNK), 0)
    col = lax.broadcasted_iota(jnp.int32, (CM_CHUNK, CM_CHUNK), 1)
    for gi in range(CM_GROUPS):
        lanes = slice(gi * CM_GROUP_DIM, (gi + 1) * CM_GROUP_DIM)
        w = jnp.where(row >= col, ws_ref[gi], 0.0).astype(BF16)
        for c in range(tm // CM_CHUNK):
            rows = slice(c * CM_CHUNK, (c + 1) * CM_CHUNK)
            mixed = _dot(w, vb[rows, lanes]) + bias_ref[:, lanes]
            z_ref[rows, lanes] = (u[rows, lanes] * mixed).astype(BF16)
    o_ref[...] = x + _dot(z_ref[...], wout_ref[...])


def _cm_prompt(x, gain, w_in, ln_gain, w_s, bias_full, w_out, seq):
    n = x.shape[0]
    tm = 512
    assert seq % tm == 0 and n % seq == 0
    per_seq = seq // tm
    return pl.pallas_call(
        _cm_prompt_kernel,
        grid=(n // tm,),
        in_specs=[pl.BlockSpec((tm, D_MODEL), lambda i: (i, 0)),
                  _resident(gain.shape), _resident(w_in.shape), _resident(ln_gain.shape),
                  _resident(w_s.shape), _resident(bias_full.shape), _resident(w_out.shape)],
        out_specs=[pl.BlockSpec((tm, D_MODEL), lambda i: (i, 0)),
                   pl.BlockSpec((CM_CHUNK, D_MODEL), lambda i: (i // per_seq, 0))],
        out_shape=[jax.ShapeDtypeStruct(x.shape, F32),
                   jax.ShapeDtypeStruct((n // seq * CM_CHUNK, D_MODEL), F32)],
        scratch_shapes=[pltpu.VMEM((tm, D_MODEL), BF16)],
        compiler_params=_params(("arbitrary",), 40),
        name="cm_prompt",
    )(x, gain, w_in, ln_gain, w_s, bias_full, w_out)


def _cm_sample_kernel(ws_ref, bs_ref, x_ref, g_ref, win_ref, lng_ref, wout_ref, o_ref, v_ref):
    t_len = x_ref.shape[0]
    us, vs = [], []
    for t in range(t_len):
        h = _rms(x_ref[t], g_ref[...]).astype(BF16)
        us.append(_gelu_tanh(_dot(h, win_ref[:, :D_MODEL])))
        v = _layer_norm(_gelu_tanh(_dot(h, win_ref[:, D_MODEL:])), lng_ref[...])
        v_ref[t] = v
        vs.append(v)
    for t in range(t_len):
        parts = []
        for gi in range(CM_GROUPS):
            lanes = slice(gi * CM_GROUP_DIM, (gi + 1) * CM_GROUP_DIM)
            mixed = jnp.full_like(vs[t][:, lanes], bs_ref[gi, t])
            for s in range(t + 1):
                mixed = mixed + ws_ref[gi, t * t_len + s] * vs[s][:, lanes]
            parts.append(us[t][:, lanes] * mixed)
        z = jnp.concatenate(parts, axis=-1).astype(BF16)
        o_ref[t] = x_ref[t] + _dot(z, wout_ref[...])


def _cm_sample(x_t, gain, w_in, ln_gain, w_s_small, b_s_small, w_out):
    smem = pl.BlockSpec(memory_space=pltpu.SMEM)
    vmem = pl.BlockSpec(memory_space=pltpu.VMEM)
    return pl.pallas_call(
        _cm_sample_kernel,
        in_specs=[smem, smem, vmem, vmem, vmem, vmem, vmem],
        out_specs=[vmem, vmem],
        out_shape=[jax.ShapeDtypeStruct(x_t.shape, F32), jax.ShapeDtypeStruct(x_t.shape, F32)],
        compiler_params=_params(None, 32),
        name="cm_sample",
    )(w_s_small, b_s_small, x_t, gain, w_in, ln_gain, w_out)


def _mla_q_kernel(x_ref, g_ref, wd_ref, gq_ref, gkv_ref, wuq_ref, wuk_ref, cos_ref, sin_ref,
                  q_ref, c_ref, kr_ref, kext_ref):
    h = _rms(x_ref[0], g_ref[...]).astype(BF16)
    d = _dot(h, wd_ref[...])
    cq = _rms(d[:, :MLA_Q_LORA], gq_ref[...]).astype(BF16)
    ckv = _rms(d[:, MLA_Q_LORA:MLA_Q_LORA + MLA_KV_LORA], gkv_ref[...])
    cos, sin = cos_ref[...], sin_ref[...]
    base = MLA_Q_LORA + MLA_KV_LORA
    krp = d[:, base:base + 128] * cos + d[:, base + 128:base + 256] * sin
    c_ref[0] = ckv
    kr_ref[0] = krp[:, :MLA_ROPE]
    kext_ref[0] = jnp.concatenate([ckv.astype(BF16), krp.astype(BF16)], axis=-1)
    nope_w = MLA_HEADS * MLA_NOPE
    for hh in range(MLA_HEADS):
        qn = _dot(cq, wuq_ref[:, hh * MLA_NOPE:(hh + 1) * MLA_NOPE])
        raw = _dot(cq, wuq_ref[:, nope_w + hh * 128:nope_w + (hh + 1) * 128])
        rot = _dot(cq, wuq_ref[:, 2 * nope_w + hh * 128:2 * nope_w + (hh + 1) * 128])
        ql = _dot(qn.astype(BF16), wuk_ref[hh]) * MLA_SCALE
        qr = (raw * cos + rot * sin) * MLA_SCALE
        q_ref[0, hh] = jnp.concatenate([ql.astype(BF16), qr.astype(BF16)], axis=-1)


def _mla_q(x, gain, wd_ext, g_q, g_kv, wuq_ext, wuk_t, cos2, sin2, tm):
    nb, t, _ = x.shape
    assert t % tm == 0
    return pl.pallas_call(
        _mla_q_kernel,
        grid=(nb, t // tm),
        in_specs=[pl.BlockSpec((1, tm, D_MODEL), lambda i, j: (i, j, 0)),
                  _resident(gain.shape), _resident(wd_ext.shape), _resident(g_q.shape),
                  _resident(g_kv.shape), _resident(wuq_ext.shape), _resident(wuk_t.shape),
                  pl.BlockSpec((tm, 128), lambda i, j: (j, 0)),
                  pl.BlockSpec((tm, 128), lambda i, j: (j, 0))],
        out_specs=[pl.BlockSpec((1, MLA_HEADS, tm, MLA_KEY_DIM), lambda i, j: (i, 0, j, 0)),
                   pl.BlockSpec((1, tm, MLA_KV_LORA), lambda i, j: (i, j, 0)),
                   pl.BlockSpec((1, tm, MLA_ROPE), lambda i, j: (i, j, 0)),
                   pl.BlockSpec((1, tm, MLA_KEY_DIM), lambda i, j: (i, j, 0))],
        out_shape=[jax.ShapeDtypeStruct((nb, MLA_HEADS, t, MLA_KEY_DIM), BF16),
                   jax.ShapeDtypeStruct((nb, t, MLA_KV_LORA), F32),
                   jax.ShapeDtypeStruct((nb, t, MLA_ROPE), F32),
                   jax.ShapeDtypeStruct((nb, t, MLA_KEY_DIM), BF16)],
        compiler_params=_params(("parallel", "parallel"), 40),
        name="mla_q",
    )(x, gain, wd_ext, g_q, g_kv, wuq_ext, wuk_t, cos2, sin2)


def _mla_attn_prompt_kernel(q_ref, k_ref, vt_ref, o_ref, m_ref, l_ref, acc_ref, *, tq):
    qi = pl.program_id(1)
    cols = MLA_HEADS * tq
    key_idx = lax.broadcasted_iota(jnp.int32, (tq, cols), 0)
    qry_idx = lax.broadcasted_iota(jnp.int32, (tq, cols), 1) & (tq - 1)
    q = q_ref[0].reshape(cols, MLA_KEY_DIM)
    m_ref[...] = jnp.full_like(m_ref, -jnp.inf)
    l_ref[...] = jnp.zeros_like(l_ref)
    acc_ref[...] = jnp.zeros_like(acc_ref)

    def kv_step(kb, diagonal):
        k = k_ref[0, pl.ds(pl.multiple_of(kb * tq, tq), tq), :]
        st = _dot_nt(k, q)
        if diagonal:
            st = jnp.where(key_idx <= qry_idx, st, -jnp.inf)
        m_old = m_ref[...]
        m_new = jnp.maximum(m_old, jnp.max(st, axis=0, keepdims=True))
        p = jnp.exp(st - m_new)
        alpha = jnp.exp(m_old - m_new)
        l_ref[...] = alpha * l_ref[...] + jnp.sum(p, axis=0, keepdims=True)
        acc_ref[...] = alpha * acc_ref[...] + _dot(vt_ref[0, kb], p.astype(BF16))
        m_ref[...] = m_new

    def kv_body(kb, c):
        kv_step(kb, False)
        return c

    lax.fori_loop(0, qi, kv_body, 0)
    kv_step(qi, True)
    o = acc_ref[...] / l_ref[...]
    o_ref[0] = o.T.reshape(MLA_HEADS, tq, MLA_KV_LORA).astype(BF16)


def _mla_attn_prompt(q, kext, vt):
    b, _, t, _ = q.shape
    tq = vt.shape[-1]
    assert t % tq == 0
    return pl.pallas_call(
        functools.partial(_mla_attn_prompt_kernel, tq=tq),
        grid=(b, t // tq),
        in_specs=[pl.BlockSpec((1, MLA_HEADS, tq, MLA_KEY_DIM), lambda i, j: (i, 0, j, 0)),
                  pl.BlockSpec((1, t, MLA_KEY_DIM), lambda i, j: (i, 0, 0)),
                  pl.BlockSpec((1, t // tq, MLA_KV_LORA, tq), lambda i, j: (i, 0, 0, 0))],
        out_specs=pl.BlockSpec((1, MLA_HEADS, tq, MLA_KV_LORA), lambda i, j: (i, 0, j, 0)),
        out_shape=jax.ShapeDtypeStruct((b, MLA_HEADS, t, MLA_KV_LORA), BF16),
        scratch_shapes=[pltpu.VMEM((1, MLA_HEADS * tq), F32), pltpu.VMEM((1, MLA_HEADS * tq), F32),
                        pltpu.VMEM((MLA_KV_LORA, MLA_HEADS * tq), F32)],
        compiler_params=_params(("parallel", "arbitrary"), 32),
        name="mla_attn_prompt",
    )(q, kext, vt)


def _mla_attn_sample_kernel(pt_ref, q_ref, *refs, t_new, n_pages):
    lat_refs, krt_refs = refs[:n_pages], refs[n_pages:2 * n_pages]
    knew_ref, o_ref, s_ref = refs[2 * n_pages:]
    q = q_ref[0]
    ql = q[:, :MLA_KV_LORA]
    qr = q[:, MLA_KV_LORA:MLA_KV_LORA + MLA_ROPE]
    for p in range(n_pages):
        c = lat_refs[p][0].astype(BF16)
        rt = krt_refs[p][0].astype(BF16)
        s_ref[:, p * PAGE_SIZE:(p + 1) * PAGE_SIZE] = _dot_nt(ql, c) + _dot(qr, rt)
    kn = knew_ref[0]
    sn = _dot_nt(q, kn)
    row_t = lax.broadcasted_iota(jnp.int32, sn.shape, 0) % t_new
    key = lax.broadcasted_iota(jnp.int32, sn.shape, 1)
    s_ref[:, n_pages * PAGE_SIZE:] = jnp.where((key < t_new) & (key <= row_t), sn, -jnp.inf)
    s = s_ref[...]
    e = jnp.exp(s - jnp.max(s, axis=-1, keepdims=True))
    denom = jnp.sum(e, axis=-1, keepdims=True)
    eb = e.astype(BF16)
    acc = _dot(eb[:, n_pages * PAGE_SIZE:], kn[:, :MLA_KV_LORA])
    for p in range(n_pages):
        acc = acc + _dot(eb[:, p * PAGE_SIZE:(p + 1) * PAGE_SIZE], lat_refs[p][0].astype(BF16))
    o_ref[0] = acc / denom


def _mla_attn_sample(page_table, q, cache_lat, cache_krt, knew, t_new):
    b, rows, _ = q.shape
    n_pages = page_table.shape[1]
    pt_flat = page_table.reshape(-1)

    def page_map(p):
        return lambda i, pt: (pt[i * n_pages + p], 0, 0)

    lat_specs = [pl.BlockSpec((1, PAGE_SIZE, MLA_KV_LORA), page_map(p)) for p in range(n_pages)]
    krt_specs = [pl.BlockSpec((1, MLA_ROPE, PAGE_SIZE), page_map(p)) for p in range(n_pages)]
    grid_spec = pltpu.PrefetchScalarGridSpec(
        num_scalar_prefetch=1,
        grid=(b,),
        in_specs=[pl.BlockSpec((1, rows, MLA_KEY_DIM), lambda i, pt: (i, 0, 0))]
        + lat_specs + krt_specs
        + [pl.BlockSpec((1, PAGE_SIZE, MLA_KEY_DIM), lambda i, pt: (i, 0, 0))],
        out_specs=pl.BlockSpec((1, rows, MLA_KV_LORA), lambda i, pt: (i, 0, 0)),
        scratch_shapes=[pltpu.VMEM((rows, (n_pages + 1) * PAGE_SIZE), F32)],
    )
    return pl.pallas_call(
        functools.partial(_mla_attn_sample_kernel, t_new=t_new, n_pages=n_pages),
        grid_spec=grid_spec,
        out_shape=jax.ShapeDtypeStruct((b, rows, MLA_KV_LORA), F32),
        compiler_params=_params(("parallel",), 40),
        name="mla_attn_sample",
    )(pt_flat, q, *([cache_lat] * n_pages), *([cache_krt] * n_pages), knew)


def _mla_out_kernel(o_ref, wuv_ref, wout_ref, x_ref, y_ref, cat_ref):
    for hh in range(MLA_HEADS):
        cat_ref[:, hh * MLA_V:(hh + 1) * MLA_V] = _dot(o_ref[0, hh], wuv_ref[hh]).astype(BF16)
    y_ref[0] = x_ref[0] + _dot(cat_ref[...], wout_ref[...])


def _mla_out(o_lat, wuv, w_out, x, tm):
    nb, t, _ = x.shape
    assert t % tm == 0
    return pl.pallas_call(
        _mla_out_kernel,
        grid=(nb, t // tm),
        in_specs=[pl.BlockSpec((1, MLA_HEADS, tm, MLA_KV_LORA), lambda i, j: (i, 0, j, 0)),
                  _resident(wuv.shape), _resident(w_out.shape),
                  pl.BlockSpec((1, tm, D_MODEL), lambda i, j: (i, j, 0))],
        out_specs=pl.BlockSpec((1, tm, D_MODEL), lambda i, j: (i, j, 0)),
        out_shape=jax.ShapeDtypeStruct(x.shape, F32),
        scratch_shapes=[pltpu.VMEM((tm, MLA_HEADS * MLA_V), BF16)],
        compiler_params=_params(("parallel", "parallel"), 32),
        name="mla_out",
    )(o_lat, wuv, w_out, x)


def _pool_prompt_kernel(x_ref, xp_ref, g_ref, w_ref, sc_ref, o_ref, hl_ref, ext_ref):
    j = pl.program_id(1)
    x = x_ref[0]
    tm = x.shape[0]
    h = _rms(x, g_ref[...])
    hp = _rms(xp_ref[0], g_ref[...])
    ext_ref[0:POOL_PREV, :] = jnp.where(j == 0, 0.0, hp)
    ext_ref[POOL_PREV:, :] = h
    pos = (j * tm + lax.broadcasted_iota(jnp.int32, (tm, 1), 0)).astype(F32)
    outs = []
    for gi, w in enumerate(POOL_WINDOWS):
        lanes = slice(gi * POOL_GROUP_DIM, (gi + 1) * POOL_GROUP_DIM)
        acc = h[:, lanes]
        for k in range(1, w):
            acc = acc + ext_ref[POOL_PREV - k:POOL_PREV - k + tm, lanes]
        pooled = acc / jnp.minimum(pos + 1.0, float(w)) - h[:, lanes]
        outs.append(_dot(pooled.astype(BF16), w_ref[gi]))
    o_ref[0] = x + jnp.concatenate(outs, axis=-1) * sc_ref[...]

    @pl.when(j == pl.num_programs(1) - 1)
    def _():
        hl_ref[0] = h[tm - POOL_PREV:]


def _pool_prompt(x, gain, w_pool, scale):
    b, t, _ = x.shape
    tm = 512
    assert t % tm == 0 and tm % POOL_PREV == 0
    per_tile = tm // POOL_PREV
    return pl.pallas_call(
        _pool_prompt_kernel,
        grid=(b, t // tm),
        in_specs=[pl.BlockSpec((1, tm, D_MODEL), lambda i, j: (i, j, 0)),
                  pl.BlockSpec((1, POOL_PREV, D_MODEL),
                               lambda i, j: (i, jnp.maximum(j * per_tile - 1, 0), 0)),
                  _resident(gain.shape), _resident(w_pool.shape), _resident(scale.shape)],
        out_specs=[pl.BlockSpec((1, tm, D_MODEL), lambda i, j: (i, j, 0)),
                   pl.BlockSpec((1, POOL_PREV, D_MODEL), lambda i, j: (i, 0, 0))],
        out_shape=[jax.ShapeDtypeStruct(x.shape, F32),
                   jax.ShapeDtypeStruct((b, POOL_PREV, D_MODEL), F32)],
        scratch_shapes=[pltpu.VMEM((POOL_PREV + tm, D_MODEL), F32)],
        compiler_params=_params(("parallel", "arbitrary"), 32),
        name="pool_prompt",
    )(x, x, gain, w_pool, scale)


def _pool_sample_kernel(x_ref, pre_ref, g_ref, w_ref, sc_ref, o_ref, h_ref, *, pos0):
    t_len = x_ref.shape[0]
    hs = [_rms(x_ref[t], g_ref[...]) for t in range(t_len)]
    for t in range(t_len):
        h_ref[t] = hs[t]

    def ext(e, lanes):
        return pre_ref[e, :, lanes] if e < POOL_BUF else hs[e - POOL_BUF][:, lanes]

    for t in range(t_len):
        outs = []
        for gi, w in enumerate(POOL_WINDOWS):
            lanes = slice(gi * POOL_GROUP_DIM, (gi + 1) * POOL_GROUP_DIM)
            acc = hs[t][:, lanes]
            for k in range(1, w):
                acc = acc + ext(POOL_BUF + t - k, lanes)
            pooled = acc / min(pos0 + t + 1.0, float(w)) - hs[t][:, lanes]
            outs.append(_dot(pooled.astype(BF16), w_ref[gi]))
        o_ref[t] = x_ref[t] + jnp.concatenate(outs, axis=-1) * sc_ref[...]


def _pool_sample(x_t, prefix_t, gain, w_pool, scale, pos0):
    vmem = pl.BlockSpec(memory_space=pltpu.VMEM)
    return pl.pallas_call(
        functools.partial(_pool_sample_kernel, pos0=pos0),
        in_specs=[vmem] * 5,
        out_specs=[vmem, vmem],
        out_shape=[jax.ShapeDtypeStruct(x_t.shape, F32), jax.ShapeDtypeStruct(x_t.shape, F32)],
        compiler_params=_params(None, 32),
        name="pool_sample",
    )(x_t, prefix_t, gain, w_pool, scale)


def _rope_tables(pos, half):
    inv = ROPE_BASE ** (-jnp.arange(half, dtype=F32) / half)
    ang = pos.astype(F32)[:, None] * inv[None, :]
    return jnp.cos(ang), jnp.sin(ang)


def _mla_rope_tables(pos):
    cos, sin = _rope_tables(pos, MLA_ROPE // 2)
    zeros = jnp.zeros((pos.shape[0], 128 - MLA_ROPE), F32)
    return (jnp.concatenate([cos, cos, zeros], axis=-1),
            jnp.concatenate([-sin, sin, zeros], axis=-1))


def _swap_halves(w):
    half = w.shape[-1] // 2
    return jnp.concatenate([w[..., half:], w[..., :half]], axis=-1)


def _pad_lanes(w, width):
    return jnp.pad(w, [(0, 0)] * (w.ndim - 1) + [(0, width - w.shape[-1])])


def _row(v):
    return v.reshape(1, -1).astype(F32)


def kernel(x_prompt, x_sample, state_ret, cache_mla_latent, cache_mla_krope, page_table, state_pool,
           norm_mix, norm_ffn, norm_final,
           ret_w_in, ret_gn_gain, ret_w_out,
           cm_w_in, cm_ln_gain, cm_w_spatial, cm_b_spatial, cm_w_out,
           mla_w_down, mla_q_norm, mla_kv_norm, mla_w_uq, mla_w_uk, mla_w_uv, mla_w_out,
           pool_w, pool_scale,
           ffn_w_gate_up, ffn_w_down):
    bp, tp, _ = x_prompt.shape
    bs, ts, _ = x_sample.shape
    xp = x_prompt
    xs = x_sample.reshape(bs * ts, D_MODEL)
    pos_p = jnp.arange(tp)
    pos_s = PAST_LEN + jnp.arange(ts)

    def ffn(x, i, final):
        n_chunks = FFN_HIDDEN // FFN_CHUNK
        wgu = ffn_w_gate_up[i].astype(BF16)
        wg = wgu[:, :FFN_HIDDEN].reshape(D_MODEL, n_chunks, FFN_CHUNK).transpose(1, 0, 2)
        wu = wgu[:, FFN_HIDDEN:].reshape(D_MODEL, n_chunks, FFN_CHUNK).transpose(1, 0, 2)
        wd = ffn_w_down[i].astype(BF16).reshape(n_chunks, FFN_CHUNK, D_MODEL)
        shape = x.shape
        y = _ffn(x.reshape(-1, D_MODEL), _row(norm_ffn[i]), wg, wu, wd, _row(norm_final), final)
        return y.reshape(shape)

    log_g = jnp.log1p(-jnp.exp2(-5.0 - jnp.arange(RET_HEADS, dtype=F32)))
    w_in = ret_w_in[0].astype(BF16)
    w_out = ret_w_out[0].astype(BF16)
    gn_gain = _row(ret_gn_gain[0])
    cos_p, sin_p = _rope_tables(pos_p, RET_QK_DIM // 2)
    cos_s, sin_s = _rope_tables(pos_s, RET_QK_DIM // 2)
    xp, ret_state_p = _ret_prompt(xp, _row(norm_mix[0]), w_in, cos_p, sin_p,
                                  _ret_tables(float(RET_CHUNK), log_g), gn_gain, w_out)
    proj_s = _norm_matmul(xs, _row(norm_mix[0]), w_in, 1024).reshape(bs, ts, -1)
    y_s, ret_state_s = _ret_sample(proj_s, state_ret[0], cos_s, sin_s,
                                   _ret_tables(float(ts), log_g), gn_gain)
    xs = _matmul_residual(y_s.reshape(bs * ts, -1), w_out, xs)
    xp, xs = ffn(xp, 0, False), ffn(xs, 0, False)

    cw_in = cm_w_in[0].astype(BF16)
    cw_out = cm_w_out[0].astype(BF16)
    bias_full = jnp.repeat(jnp.transpose(cm_b_spatial[0]), CM_GROUP_DIM, axis=1)
    xp2, cm_v_p = _cm_prompt(xp.reshape(bp * tp, D_MODEL), _row(norm_mix[1]), cw_in,
                             _row(cm_ln_gain[0]), cm_w_spatial[0], bias_full, cw_out, tp)
    xp = xp2.reshape(bp, tp, D_MODEL)
    xs_t, cm_v_s_t = _cm_sample(xs.reshape(bs, ts, D_MODEL).transpose(1, 0, 2), _row(norm_mix[1]), cw_in,
                                _row(cm_ln_gain[0]),
                                cm_w_spatial[0][:, :ts, :ts].reshape(CM_GROUPS, ts * ts),
                                cm_b_spatial[0][:, :ts], cw_out)
    xs = xs_t.transpose(1, 0, 2).reshape(bs * ts, D_MODEL)
    cm_v_s = cm_v_s_t.transpose(1, 0, 2)
    xp, xs = ffn(xp, 1, False), ffn(xs, 1, False)

    wd = mla_w_down[0]
    kr_w = wd[:, MLA_Q_LORA + MLA_KV_LORA:]
    wd_ext = jnp.concatenate([wd[:, :MLA_Q_LORA + MLA_KV_LORA], _pad_lanes(kr_w, 128),
                              _pad_lanes(_swap_halves(kr_w), 128)], axis=-1).astype(BF16)
    wuq = mla_w_uq[0].reshape(MLA_Q_LORA, MLA_HEADS, MLA_NOPE + MLA_ROPE)
    wuq_rope = wuq[:, :, MLA_NOPE:]
    wuq_ext = jnp.concatenate([
        wuq[:, :, :MLA_NOPE].reshape(MLA_Q_LORA, -1),
        _pad_lanes(wuq_rope, 128).reshape(MLA_Q_LORA, -1),
        _pad_lanes(_swap_halves(wuq_rope), 128).reshape(MLA_Q_LORA, -1)], axis=-1).astype(BF16)
    wuk_t = mla_w_uk[0].transpose(1, 2, 0).astype(BF16)
    wuv = mla_w_uv[0].transpose(1, 0, 2).astype(BF16)
    mw_out = mla_w_out[0].astype(BF16)
    cos2_p, sin2_p = _mla_rope_tables(pos_p)
    cos2_s, sin2_s = _mla_rope_tables(jnp.tile(pos_s, bs))
    gq, gkv = _row(mla_q_norm[0]), _row(mla_kv_norm[0])
    q_p, lat_p, kr_p, kext_p = _mla_q(xp, _row(norm_mix[2]), wd_ext, gq, gkv, wuq_ext, wuk_t,
                                      cos2_p, sin2_p, 256)
    tq = 256
    vt_p = kext_p[:, :, :MLA_KV_LORA].reshape(bp, tp // tq, tq, MLA_KV_LORA).swapaxes(2, 3)
    o_p = _mla_attn_prompt(q_p, kext_p, vt_p)
    xp = _mla_out(o_p, wuv, mw_out, xp, 256)
    ns = bs * ts
    q_s, lat_s, kr_s, kext_s = _mla_q(xs.reshape(1, ns, D_MODEL), _row(norm_mix[2]), wd_ext, gq, gkv,
                                      wuq_ext, wuk_t, cos2_s, sin2_s, ns)
    q_s = q_s.reshape(MLA_HEADS, bs, ts, MLA_KEY_DIM).transpose(1, 0, 2, 3).reshape(bs, MLA_HEADS * ts, MLA_KEY_DIM)
    knew = jnp.pad(kext_s.reshape(bs, ts, MLA_KEY_DIM), ((0, 0), (0, PAGE_SIZE - ts), (0, 0)))
    o_s = _mla_attn_sample(page_table, q_s, cache_mla_latent[0],
                           jnp.swapaxes(cache_mla_krope[0], 1, 2), knew, ts)
    o_s = o_s.reshape(bs, MLA_HEADS, ts, MLA_KV_LORA).transpose(1, 0, 2, 3).reshape(1, MLA_HEADS, ns, MLA_KV_LORA)
    xs = _mla_out(o_s.astype(BF16), wuv, mw_out, xs.reshape(1, ns, D_MODEL), ns).reshape(ns, D_MODEL)
    lat_s = lat_s.reshape(bs, ts, MLA_KV_LORA)
    kr_s = kr_s.reshape(bs, ts, MLA_ROPE)
    xp, xs = ffn(xp, 2, False), ffn(xs, 2, False)

    pw = pool_w[0].astype(BF16)
    xp, h_last = _pool_prompt(xp, _row(norm_mix[3]), pw, _row(pool_scale[0]))
    pool_state_p = h_last[:, POOL_PREV - POOL_BUF:]
    xs_t, h_s_t = _pool_sample(xs.reshape(bs, ts, D_MODEL).transpose(1, 0, 2),
                               state_pool[0].transpose(1, 0, 2), _row(norm_mix[3]), pw,
                               _row(pool_scale[0]), float(PAST_LEN))
    xs = xs_t.transpose(1, 0, 2).reshape(bs * ts, D_MODEL)
    pool_state_s = jnp.concatenate([state_pool[0], h_s_t.transpose(1, 0, 2)], axis=1)[:, -POOL_BUF:]
    xp, xs = ffn(xp, 3, True), ffn(xs, 3, True)

    return (xp, xs.reshape(bs, ts, D_MODEL),
            ret_state_p[None], ret_state_s[None],
            cm_v_p.reshape(bp, CM_CHUNK, D_MODEL)[None], cm_v_s[None],
            lat_p[None], kr_p[None], lat_s[None], kr_s[None],
            pool_state_p[None], pool_state_s[None])
```

```python
import functools

import jax
import jax.numpy as jnp
from jax import lax
from jax.experimental import pallas as pl
from jax.experimental.pallas import tpu as pltpu

F32 = jnp.float32
BF16 = jnp.bfloat16

D_MODEL = 1024
NORM_EPS = 1e-6
ROPE_BASE = 10000.0
PAST_LEN = 8192
PAGE_SIZE = 128

RET_HEADS = 4
RET_QK_DIM = 256
RET_V_DIM = 512
RET_CHUNK = 128

CM_CHUNK = 128
CM_GROUPS = 4
CM_GROUP_DIM = 256

MLA_HEADS = 8
MLA_Q_LORA = 384
MLA_KV_LORA = 256
MLA_NOPE = 128
MLA_ROPE = 64
MLA_V = 128
MLA_SCALE = (MLA_NOPE + MLA_ROPE) ** -0.5
MLA_KEY_DIM = MLA_KV_LORA + 128
MLA_KEY_CHUNK = 1024

POOL_WINDOWS = (2, 4, 8, 16)
POOL_GROUP_DIM = 256
POOL_BUF = 15
POOL_PREV = 16

FFN_HIDDEN = 2816
FFN_CHUNK = 256

V7X_VMEM_BYTES = 64 * 1024 * 1024
MIB = 1024 * 1024


def _params(semantics, vmem_mib):
    assert vmem_mib * MIB < V7X_VMEM_BYTES
    return pltpu.CompilerParams(dimension_semantics=semantics,
                                vmem_limit_bytes=vmem_mib * MIB)


def _resident(shape):
    nd = len(shape)
    return pl.BlockSpec(shape, lambda *_: (0,) * nd, pipeline_mode=pl.Buffered(1))


def _rms(x, g):
    return x * lax.rsqrt(jnp.mean(x * x, axis=-1, keepdims=True) + NORM_EPS) * g


def _dot(a, b):
    return jnp.dot(a, b, preferred_element_type=F32)


def _dot_nt(a, b):
    return lax.dot_general(a, b, (((1,), (1,)), ((), ())), preferred_element_type=F32)


def _ffn_kernel(x_ref, g_ref, wgu_ref, wd_ref, gf_ref, o_ref, *, final):
    x = x_ref[...]
    h = _rms(x, g_ref[...]).astype(BF16)
    acc = x
    for j in range(FFN_HIDDEN // FFN_CHUNK):
        cols = slice(j * FFN_CHUNK, (j + 1) * FFN_CHUNK)
        ucols = slice(FFN_HIDDEN + j * FFN_CHUNK, FFN_HIDDEN + (j + 1) * FFN_CHUNK)
        g = _dot(h, wgu_ref[:, cols])
        u = _dot(h, wgu_ref[:, ucols])
        a = (g * jax.nn.sigmoid(g)) * u
        acc = acc + _dot(a.astype(BF16), wd_ref[cols, :])
    o_ref[...] = _rms(acc, gf_ref[...]) if final else acc


def _ffn(x, gain, wgu_all, wd_all, layer, final_gain, final):
    n = x.shape[0]
    tm = min(n, 512)
    assert n % tm == 0

    def layer_slab(w):
        return pl.BlockSpec((None,) + w.shape[1:], lambda i: (layer, 0, 0),
                            pipeline_mode=pl.Buffered(1))

    return pl.pallas_call(
        functools.partial(_ffn_kernel, final=final),
        grid=(n // tm,),
        in_specs=[pl.BlockSpec((tm, D_MODEL), lambda i: (i, 0)),
                  _resident(gain.shape), layer_slab(wgu_all),
                  layer_slab(wd_all), _resident(final_gain.shape)],
        out_specs=pl.BlockSpec((tm, D_MODEL), lambda i: (i, 0)),
        out_shape=jax.ShapeDtypeStruct(x.shape, F32),
        compiler_params=_params(("parallel",), 48),
        name="ffn",
    )(x, gain, wgu_all, wd_all, final_gain)


def _norm_matmul_kernel(x_ref, g_ref, w_ref, o_ref):
    h = _rms(x_ref[...], g_ref[...]).astype(BF16)
    o_ref[...] = _dot(h, w_ref[...])


def _norm_matmul(x, gain, w, tn):
    n, dout = x.shape[0], w.shape[1]
    assert dout % tn == 0
    return pl.pallas_call(
        _norm_matmul_kernel,
        grid=(dout // tn,),
        in_specs=[pl.BlockSpec((n, D_MODEL), lambda j: (0, 0)),
                  pl.BlockSpec(gain.shape, lambda j: (0, 0)),
                  pl.BlockSpec((D_MODEL, tn), lambda j: (0, j))],
        out_specs=pl.BlockSpec((n, tn), lambda j: (0, j)),
        out_shape=jax.ShapeDtypeStruct((n, dout), F32),
        compiler_params=_params(("parallel",), 32),
        name="norm_matmul",
    )(x, gain, w)


def _matmul_residual_kernel(y_ref, w_ref, x_ref, o_ref):
    o_ref[...] = x_ref[...] + _dot(y_ref[...].astype(BF16), w_ref[...])


def _matmul_residual(y, w, x):
    n = x.shape[0]
    tm = min(n, 512)
    assert n % tm == 0
    return pl.pallas_call(
        _matmul_residual_kernel,
        grid=(n // tm,),
        in_specs=[pl.BlockSpec((tm, y.shape[1]), lambda i: (i, 0)),
                  _resident(w.shape),
                  pl.BlockSpec((tm, D_MODEL), lambda i: (i, 0))],
        out_specs=pl.BlockSpec((tm, D_MODEL), lambda i: (i, 0)),
        out_shape=jax.ShapeDtypeStruct(x.shape, F32),
        compiler_params=_params(("parallel",), 32),
        name="matmul_residual",
    )(y, w, x)


def _rope_halves(x, cos, sin):
    half = cos.shape[-1]
    x1, x2 = x[:, :half], x[:, half:]
    return jnp.concatenate([x1 * cos - x2 * sin, x1 * sin + x2 * cos], axis=-1)


def _ret_chunk(q, k, v, s, dmat, qdec, kdec, sdec):
    vb = v.astype(BF16)
    scores = _dot_nt(q.astype(BF16), k.astype(BF16)) * dmat
    o = _dot(scores.astype(BF16), vb) + _dot((q * qdec).astype(BF16), s.astype(BF16))
    s_new = sdec * s + _dot((k * kdec).T.astype(BF16), vb)
    return o, s_new


def _group_norm_gate(o, gate, gain):
    mu = jnp.mean(o, axis=-1, keepdims=True)
    oc = o - mu
    var = jnp.mean(oc * oc, axis=-1, keepdims=True)
    on = oc * lax.rsqrt(var + NORM_EPS)
    return (gate * jax.nn.sigmoid(gate)) * on * gain


def _ret_prompt_kernel(x_ref, g_ref, win_ref, cos_ref, sin_ref, dmat_ref, qdec_ref,
                       kdec_ref, sdec_ref, gng_ref, wout_ref, o_ref, st_ref,
                       s_ref, y_ref):
    j = pl.program_id(1)

    @pl.when(j == 0)
    def _():
        s_ref[...] = jnp.zeros_like(s_ref)

    x = x_ref[0]
    tm = x.shape[0]
    h = _rms(x, g_ref[...]).astype(BF16)
    for hh in range(RET_HEADS):
        qs = slice(hh * RET_QK_DIM, (hh + 1) * RET_QK_DIM)
        ks = slice(D_MODEL + hh * RET_QK_DIM, D_MODEL + (hh + 1) * RET_QK_DIM)
        vs = slice(2 * D_MODEL + hh * RET_V_DIM, 2 * D_MODEL + (hh + 1) * RET_V_DIM)
        gs = slice(4 * D_MODEL + hh * RET_V_DIM, 4 * D_MODEL + (hh + 1) * RET_V_DIM)
        q = _rope_halves(_dot(h, win_ref[:, qs]), cos_ref[...], sin_ref[...]) * (RET_QK_DIM ** -0.5)
        k = _rope_halves(_dot(h, win_ref[:, ks]), cos_ref[...], sin_ref[...])
        v = _dot(h, win_ref[:, vs])
        gate = _dot(h, win_ref[:, gs])
        gain = gng_ref[:, hh * RET_V_DIM:(hh + 1) * RET_V_DIM]
        for c in range(tm // RET_CHUNK):
            rows = slice(c * RET_CHUNK, (c + 1) * RET_CHUNK)
            o, s_new = _ret_chunk(q[rows], k[rows], v[rows], s_ref[hh], dmat_ref[hh],
                                  qdec_ref[hh], kdec_ref[hh], sdec_ref[hh])
            s_ref[hh] = s_new
            y_ref[rows, hh * RET_V_DIM:(hh + 1) * RET_V_DIM] = (
                _group_norm_gate(o, gate[rows], gain).astype(BF16))
    o_ref[0] = x + _dot(y_ref[...], wout_ref[...])

    @pl.when(j == pl.num_programs(1) - 1)
    def _():
        st_ref[0] = s_ref[...]


def _ret_tables(length, log_g):
    idx = jnp.arange(RET_CHUNK, dtype=F32)
    valid = idx < length
    rel = idx[:, None] - idx[None, :]
    ok = (rel >= 0) & valid[:, None] & valid[None, :]
    dmat = jnp.where(ok[None], jnp.exp(jnp.maximum(rel, 0.0)[None] * log_g[:, None, None]), 0.0)
    qd = jnp.where(valid[None], jnp.exp((idx + 1.0)[None, :] * log_g[:, None]), 0.0)
    kd = jnp.where(valid[None], jnp.exp((length - 1.0 - idx)[None, :] * log_g[:, None]), 0.0)
    qdec = jnp.broadcast_to(qd[:, :, None], (RET_HEADS, RET_CHUNK, RET_QK_DIM))
    kdec = jnp.broadcast_to(kd[:, :, None], (RET_HEADS, RET_CHUNK, RET_QK_DIM))
    sdec = jnp.broadcast_to(jnp.exp(length * log_g)[:, None, None], (RET_HEADS, 1, RET_V_DIM))
    return dmat, qdec, kdec, sdec


def _ret_prompt(x, gain, w_in, cos, sin, tables, gn_gain, w_out):
    b, t, _ = x.shape
    tm = 256
    assert t % tm == 0
    dmat, qdec, kdec, sdec = tables
    return pl.pallas_call(
        _ret_prompt_kernel,
        grid=(b, t // tm),
        in_specs=[pl.BlockSpec((1, tm, D_MODEL), lambda i, j: (i, j, 0)),
                  _resident(gain.shape), _resident(w_in.shape),
                  pl.BlockSpec((tm, RET_QK_DIM // 2), lambda i, j: (j, 0)),
                  pl.BlockSpec((tm, RET_QK_DIM // 2), lambda i, j: (j, 0)),
                  _resident(dmat.shape), _resident(qdec.shape), _resident(kdec.shape),
                  _resident(sdec.shape), _resident(gn_gain.shape), _resident(w_out.shape)],
        out_specs=[pl.BlockSpec((1, tm, D_MODEL), lambda i, j: (i, j, 0)),
                   pl.BlockSpec((1, RET_HEADS, RET_QK_DIM, RET_V_DIM), lambda i, j: (i, 0, 0, 0))],
        out_shape=[jax.ShapeDtypeStruct(x.shape, F32),
                   jax.ShapeDtypeStruct((b, RET_HEADS, RET_QK_DIM, RET_V_DIM), F32)],
        scratch_shapes=[pltpu.VMEM((RET_HEADS, RET_QK_DIM, RET_V_DIM), F32),
                        pltpu.VMEM((tm, RET_HEADS * RET_V_DIM), BF16)],
        compiler_params=_params(("parallel", "arbitrary"), 56),
        name="ret_prompt",
    )(x, gain, w_in, cos, sin, dmat, qdec, kdec, sdec, gn_gain, w_out)


def _ret_sample_kernel(p_ref, s0_ref, cos_ref, sin_ref, dmat_ref, qdec_ref, kdec_ref,
                       sdec_ref, gng_ref, y_ref, st_ref, qp_ref, kp_ref, vp_ref):
    @pl.when(pl.program_id(0) == 0)
    def _():
        qp_ref[...] = jnp.zeros_like(qp_ref)
        kp_ref[...] = jnp.zeros_like(kp_ref)
        vp_ref[...] = jnp.zeros_like(vp_ref)

    t = p_ref.shape[1]
    for hh in range(RET_HEADS):
        qs = slice(hh * RET_QK_DIM, (hh + 1) * RET_QK_DIM)
        ks = slice(D_MODEL + hh * RET_QK_DIM, D_MODEL + (hh + 1) * RET_QK_DIM)
        vs = slice(2 * D_MODEL + hh * RET_V_DIM, 2 * D_MODEL + (hh + 1) * RET_V_DIM)
        gs = slice(4 * D_MODEL + hh * RET_V_DIM, 4 * D_MODEL + (hh + 1) * RET_V_DIM)
        qp_ref[0:t, :] = _rope_halves(p_ref[0, :, qs], cos_ref[...], sin_ref[...]) * (RET_QK_DIM ** -0.5)
        kp_ref[0:t, :] = _rope_halves(p_ref[0, :, ks], cos_ref[...], sin_ref[...])
        vp_ref[0:t, :] = p_ref[0, :, vs]
        o, s_new = _ret_chunk(qp_ref[...], kp_ref[...], vp_ref[...], s0_ref[0, hh], dmat_ref[hh],
                              qdec_ref[hh], kdec_ref[hh], sdec_ref[hh])
        st_ref[0, hh] = s_new
        gain = gng_ref[:, hh * RET_V_DIM:(hh + 1) * RET_V_DIM]
        y_ref[0, :, hh * RET_V_DIM:(hh + 1) * RET_V_DIM] = _group_norm_gate(o[0:t], p_ref[0, :, gs], gain)


def _ret_sample(proj, s0, cos, sin, tables, gn_gain):
    b, t, _ = proj.shape
    dmat, qdec, kdec, sdec = tables
    state_spec = pl.BlockSpec((1, RET_HEADS, RET_QK_DIM, RET_V_DIM), lambda i: (i, 0, 0, 0))
    return pl.pallas_call(
        _ret_sample_kernel,
        grid=(b,),
        in_specs=[pl.BlockSpec((1, t, proj.shape[2]), lambda i: (i, 0, 0)),
                  state_spec,
                  _resident(cos.shape), _resident(sin.shape),
                  _resident(dmat.shape), _resident(qdec.shape), _resident(kdec.shape),
                  _resident(sdec.shape), _resident(gn_gain.shape)],
        out_specs=[pl.BlockSpec((1, t, RET_HEADS * RET_V_DIM), lambda i: (i, 0, 0)),
                   state_spec],
        out_shape=[jax.ShapeDtypeStruct((b, t, RET_HEADS * RET_V_DIM), F32),
                   jax.ShapeDtypeStruct(s0.shape, F32)],
        scratch_shapes=[pltpu.VMEM((RET_CHUNK, RET_QK_DIM), F32),
                        pltpu.VMEM((RET_CHUNK, RET_QK_DIM), F32),
                        pltpu.VMEM((RET_CHUNK, RET_V_DIM), F32)],
        compiler_params=_params(("arbitrary",), 32),
        name="ret_sample",
    )(proj, s0, cos, sin, dmat, qdec, kdec, sdec, gn_gain)


def _gelu_tanh(x):
    return x * (0.5 * (1.0 + jnp.tanh(0.7978845608028654 * (x + 0.044715 * (x * x * x)))))


def _layer_norm(v, gain):
    mu = jnp.mean(v, axis=-1, keepdims=True)
    vc = v - mu
    var = jnp.mean(vc * vc, axis=-1, keepdims=True)
    return vc * lax.rsqrt(var + NORM_EPS) * gain


def _cm_prompt_kernel(x_ref, g_ref, win_ref, lng_ref, ws_ref, bias_ref, wout_ref,
                      o_ref, v_ref, z_ref):
    x = x_ref[...]
    tm = x.shape[0]
    h = _rms(x, g_ref[...]).astype(BF16)
    u = _gelu_tanh(_dot(h, win_ref[:, :D_MODEL]))
    v = _layer_norm(_gelu_tanh(_dot(h, win_ref[:, D_MODEL:])), lng_ref[...])
    v_ref[...] = v[tm - CM_CHUNK:]
    vb = v.astype(BF16)
    row = lax.broadcasted_iota(jnp.int32, (CM_CHUNK, CM_CHUNK), 0)
    col = lax.broadcasted_iota(jnp.int32, (CM_CHUNK, CM_CHUNK), 1)
    for gi in range(CM_GROUPS):
        lanes = slice(gi * CM_GROUP_DIM, (gi + 1) * CM_GROUP_DIM)
        w = jnp.where(row >= col, ws_ref[gi], 0.0).astype(BF16)
        for c in range(tm // CM_CHUNK):
            rows = slice(c * CM_CHUNK, (c + 1) * CM_CHUNK)
            mixed = _dot(w, vb[rows, lanes]) + bias_ref[:, lanes]
            z_ref[rows, lanes] = (u[rows, lanes] * mixed).astype(BF16)
    o_ref[...] = x + _dot(z_ref[...], wout_ref[...])


def _cm_prompt(x, gain, w_in, ln_gain, w_s, bias_full, w_out, seq):
    n = x.shape[0]
    tm = 512
    assert seq % tm == 0 and n % seq == 0
    per_seq = seq // tm
    return pl.pallas_call(
        _cm_prompt_kernel,
        grid=(n // tm,),
        in_specs=[pl.BlockSpec((tm, D_MODEL), lambda i: (i, 0)),
                  _resident(gain.shape), _resident(w_in.shape), _resident(ln_gain.shape),
                  _resident(w_s.shape), _resident(bias_full.shape), _resident(w_out.shape)],
        out_specs=[pl.BlockSpec((tm, D_MODEL), lambda i: (i, 0)),
                   pl.BlockSpec((CM_CHUNK, D_MODEL), lambda i: (i // per_seq, 0))],
        out_shape=[jax.ShapeDtypeStruct(x.shape, F32),
                   jax.ShapeDtypeStruct((n // seq * CM_CHUNK, D_MODEL), F32)],
        scratch_shapes=[pltpu.VMEM((tm, D_MODEL), BF16)],
        compiler_params=_params(("arbitrary",), 40),
        name="cm_prompt",
    )(x, gain, w_in, ln_gain, w_s, bias_full, w_out)


def _cm_sample_kernel(ws_ref, bs_ref, x_ref, g_ref, win_ref, lng_ref, wout_ref, o_ref, v_ref):
    t_len = x_ref.shape[0]
    us, vs = [], []
    for t in range(t_len):
        h = _rms(x_ref[t], g_ref[...]).astype(BF16)
        us.append(_gelu_tanh(_dot(h, win_ref[:, :D_MODEL])))
        v = _layer_norm(_gelu_tanh(_dot(h, win_ref[:, D_MODEL:])), lng_ref[...])
        v_ref[t] = v
        vs.append(v)
    for t in range(t_len):
        parts = []
        for gi in range(CM_GROUPS):
            lanes = slice(gi * CM_GROUP_DIM, (gi + 1) * CM_GROUP_DIM)
            mixed = jnp.full_like(vs[t][:, lanes], bs_ref[gi, t])
            for s in range(t + 1):
                mixed = mixed + ws_ref[gi, t * t_len + s] * vs[s][:, lanes]
            parts.append(us[t][:, lanes] * mixed)
        z = jnp.concatenate(parts, axis=-1).astype(BF16)
        o_ref[t] = x_ref[t] + _dot(z, wout_ref[...])


def _cm_sample(x_t, gain, w_in, ln_gain, w_s_small, b_s_small, w_out):
    smem = pl.BlockSpec(memory_space=pltpu.SMEM)
    vmem = pl.BlockSpec(memory_space=pltpu.VMEM)
    return pl.pallas_call(
        _cm_sample_kernel,
        in_specs=[smem, smem, vmem, vmem, vmem, vmem, vmem],
        out_specs=[vmem, vmem],
        out_shape=[jax.ShapeDtypeStruct(x_t.shape, F32), jax.ShapeDtypeStruct(x_t.shape, F32)],
        compiler_params=_params(None, 32),
        name="cm_sample",
    )(w_s_small, b_s_small, x_t, gain, w_in, ln_gain, w_out)


def _mla_q_kernel(x_ref, g_ref, wd_ref, gq_ref, gkv_ref, wuq_ref, wuk_ref, cos_ref, sin_ref,
                  q_ref, c_ref, kr_ref, kext_ref):
    h = _rms(x_ref[0], g_ref[...]).astype(BF16)
    d = _dot(h, wd_ref[...])
    cq = _rms(d[:, :MLA_Q_LORA], gq_ref[...]).astype(BF16)
    ckv = _rms(d[:, MLA_Q_LORA:MLA_Q_LORA + MLA_KV_LORA], gkv_ref[...])
    cos, sin = cos_ref[...], sin_ref[...]
    base = MLA_Q_LORA + MLA_KV_LORA
    krp = d[:, base:base + 128] * cos + d[:, base + 128:base + 256] * sin
    c_ref[0] = ckv
    kr_ref[0] = krp[:, :MLA_ROPE]
    kext_ref[0] = jnp.concatenate([ckv.astype(BF16), krp.astype(BF16)], axis=-1)
    nope_w = MLA_HEADS * MLA_NOPE
    for hh in range(MLA_HEADS):
        qn = _dot(cq, wuq_ref[:, hh * MLA_NOPE:(hh + 1) * MLA_NOPE])
        raw = _dot(cq, wuq_ref[:, nope_w + hh * 128:nope_w + (hh + 1) * 128])
        rot = _dot(cq, wuq_ref[:, 2 * nope_w + hh * 128:2 * nope_w + (hh + 1) * 128])
        ql = _dot(qn.astype(BF16), wuk_ref[hh]) * MLA_SCALE
        qr = (raw * cos + rot * sin) * MLA_SCALE
        q_ref[0, hh] = jnp.concatenate([ql.astype(BF16), qr.astype(BF16)], axis=-1)


def _mla_q(x, gain, wd_ext, g_q, g_kv, wuq_ext, wuk_t, cos2, sin2, tm):
    nb, t, _ = x.shape
    assert t % tm == 0
    return pl.pallas_call(
        _mla_q_kernel,
        grid=(nb, t // tm),
        in_specs=[pl.BlockSpec((1, tm, D_MODEL), lambda i, j: (i, j, 0)),
                  _resident(gain.shape), _resident(wd_ext.shape), _resident(g_q.shape),
                  _resident(g_kv.shape), _resident(wuq_ext.shape), _resident(wuk_t.shape),
                  pl.BlockSpec((tm, 128), lambda i, j: (j, 0)),
                  pl.BlockSpec((tm, 128), lambda i, j: (j, 0))],
        out_specs=[pl.BlockSpec((1, MLA_HEADS, tm, MLA_KEY_DIM), lambda i, j: (i, 0, j, 0)),
                   pl.BlockSpec((1, tm, MLA_KV_LORA), lambda i, j: (i, j, 0)),
                   pl.BlockSpec((1, tm, MLA_ROPE), lambda i, j: (i, j, 0)),
                   pl.BlockSpec((1, tm, MLA_KEY_DIM), lambda i, j: (i, j, 0))],
        out_shape=[jax.ShapeDtypeStruct((nb, MLA_HEADS, t, MLA_KEY_DIM), BF16),
                   jax.ShapeDtypeStruct((nb, t, MLA_KV_LORA), F32),
                   jax.ShapeDtypeStruct((nb, t, MLA_ROPE), F32),
                   jax.ShapeDtypeStruct((nb, t, MLA_KEY_DIM), BF16)],
        compiler_params=_params(("parallel", "parallel"), 40),
        name="mla_q",
    )(x, gain, wd_ext, g_q, g_kv, wuq_ext, wuk_t, cos2, sin2)


def _mla_attn_prompt_kernel(q_ref, k_ref, vt_ref, o_ref, m_ref, l_ref, acc_ref, *, tq):
    qi = pl.program_id(1)
    cols = MLA_HEADS * tq
    key_idx = lax.broadcasted_iota(jnp.int32, (tq, cols), 0)
    qry_idx = lax.broadcasted_iota(jnp.int32, (tq, cols), 1) & (tq - 1)
    q = q_ref[0].reshape(cols, MLA_KEY_DIM)
    m_ref[...] = jnp.full_like(m_ref, -jnp.inf)
    l_ref[...] = jnp.zeros_like(l_ref)
    acc_ref[...] = jnp.zeros_like(acc_ref)

    def kv_step(kb, diagonal):
        k = k_ref[0, pl.ds(pl.multiple_of(kb * tq, tq), tq), :]
        st = _dot_nt(k, q)
        if diagonal:
            st = jnp.where(key_idx <= qry_idx, st, -jnp.inf)
        m_old = m_ref[...]
        m_new = jnp.maximum(m_old, jnp.max(st, axis=0, keepdims=True))
        p = jnp.exp(st - m_new)
        alpha = jnp.exp(m_old - m_new)
        l_ref[...] = alpha * l_ref[...] + jnp.sum(p, axis=0, keepdims=True)
        acc_ref[...] = alpha * acc_ref[...] + _dot(vt_ref[0, kb], p.astype(BF16))
        m_ref[...] = m_new

    def kv_body(kb, c):
        kv_step(kb, False)
        return c

    lax.fori_loop(0, qi, kv_body, 0)
    kv_step(qi, True)
    o = acc_ref[...] / l_ref[...]
    o_ref[0] = o.T.reshape(MLA_HEADS, tq, MLA_KV_LORA).astype(BF16)


def _mla_attn_prompt(q, kext, vt):
    b, _, t, _ = q.shape
    tq = vt.shape[-1]
    assert t % tq == 0
    return pl.pallas_call(
        functools.partial(_mla_attn_prompt_kernel, tq=tq),
        grid=(b, t // tq),
        in_specs=[pl.BlockSpec((1, MLA_HEADS, tq, MLA_KEY_DIM), lambda i, j: (i, 0, j, 0)),
                  pl.BlockSpec((1, t, MLA_KEY_DIM), lambda i, j: (i, 0, 0)),
                  pl.BlockSpec((1, t // tq, MLA_KV_LORA, tq), lambda i, j: (i, 0, 0, 0))],
        out_specs=pl.BlockSpec((1, MLA_HEADS, tq, MLA_KV_LORA), lambda i, j: (i, 0, j, 0)),
        out_shape=jax.ShapeDtypeStruct((b, MLA_HEADS, t, MLA_KV_LORA), BF16),
        scratch_shapes=[pltpu.VMEM((1, MLA_HEADS * tq), F32), pltpu.VMEM((1, MLA_HEADS * tq), F32),
                        pltpu.VMEM((MLA_KV_LORA, MLA_HEADS * tq), F32)],
        compiler_params=_params(("parallel", "arbitrary"), 32),
        name="mla_attn_prompt",
    )(q, kext, vt)


def _mla_attn_sample_kernel(pt_ref, q_ref, knew_ref, lat_hbm, krt_hbm, o_ref,
                            cbuf, rbuf, s_ref, sems, *, t_new, n_pages):
    b = pl.program_id(0)
    slot = b % 2
    n_keys = n_pages * PAGE_SIZE

    def page_copies(row, slot_):
        copies = []
        for p in range(n_pages):
            page = pt_ref[row * n_pages + p]
            keys = pl.ds(p * PAGE_SIZE, PAGE_SIZE)
            copies.append(pltpu.make_async_copy(lat_hbm.at[page], cbuf.at[slot_, keys, :],
                                                sems.at[0, slot_]))
            copies.append(pltpu.make_async_copy(krt_hbm.at[page], rbuf.at[slot_, :, keys],
                                                sems.at[1, slot_]))
        return copies

    @pl.when(b == 0)
    def _():
        for cp in page_copies(b, slot):
            cp.start()

    @pl.when(b + 1 < pl.num_programs(0))
    def _():
        for cp in page_copies(b + 1, 1 - slot):
            cp.start()

    for cp in page_copies(b, slot):
        cp.wait()

    q = q_ref[0]
    ql = q[:, :MLA_KV_LORA]
    qr = q[:, MLA_KV_LORA:MLA_KV_LORA + MLA_ROPE]
    for j in range(n_keys // MLA_KEY_CHUNK):
        keys = slice(j * MLA_KEY_CHUNK, (j + 1) * MLA_KEY_CHUNK)
        c = cbuf[slot, keys, :].astype(BF16)
        rt = rbuf[slot, :, keys].astype(BF16)
        s_ref[:, keys] = _dot_nt(ql, c) + _dot(qr, rt)
    kn = knew_ref[0]
    sn = _dot_nt(q, kn)
    row_t = lax.broadcasted_iota(jnp.int32, sn.shape, 0) % t_new
    key = lax.broadcasted_iota(jnp.int32, sn.shape, 1)
    s_ref[:, n_keys:] = jnp.where((key < t_new) & (key <= row_t), sn, -jnp.inf)
    s = s_ref[...]
    e = jnp.exp(s - jnp.max(s, axis=-1, keepdims=True))
    denom = jnp.sum(e, axis=-1, keepdims=True)
    eb = e.astype(BF16)
    acc = _dot(eb[:, n_keys:], kn[:, :MLA_KV_LORA])
    for j in range(n_keys // MLA_KEY_CHUNK):
        keys = slice(j * MLA_KEY_CHUNK, (j + 1) * MLA_KEY_CHUNK)
        acc = acc + _dot(eb[:, keys], cbuf[slot, keys, :].astype(BF16))
    o_ref[0] = acc / denom


def _mla_attn_sample(page_table, q, cache_lat, cache_krt, knew, t_new):
    b, rows, _ = q.shape
    n_pages = page_table.shape[1]
    n_keys = n_pages * PAGE_SIZE
    assert n_keys % MLA_KEY_CHUNK == 0
    grid_spec = pltpu.PrefetchScalarGridSpec(
        num_scalar_prefetch=1,
        grid=(b,),
        in_specs=[pl.BlockSpec((1, rows, MLA_KEY_DIM), lambda i, pt: (i, 0, 0)),
                  pl.BlockSpec((1, PAGE_SIZE, MLA_KEY_DIM), lambda i, pt: (i, 0, 0)),
                  pl.BlockSpec(memory_space=pl.ANY),
                  pl.BlockSpec(memory_space=pl.ANY)],
        out_specs=pl.BlockSpec((1, rows, MLA_KV_LORA), lambda i, pt: (i, 0, 0)),
        scratch_shapes=[pltpu.VMEM((2, n_keys, MLA_KV_LORA), F32),
                        pltpu.VMEM((2, MLA_ROPE, n_keys), F32),
                        pltpu.VMEM((rows, n_keys + PAGE_SIZE), F32),
                        pltpu.SemaphoreType.DMA((2, 2))],
    )
    return pl.pallas_call(
        functools.partial(_mla_attn_sample_kernel, t_new=t_new, n_pages=n_pages),
        grid_spec=grid_spec,
        out_shape=jax.ShapeDtypeStruct((b, rows, MLA_KV_LORA), F32),
        compiler_params=_params(("arbitrary",), 40),
        name="mla_attn_sample",
    )(page_table.reshape(-1), q, knew, cache_lat, cache_krt)


def _mla_out_kernel(o_ref, wuv_ref, wout_ref, x_ref, y_ref, cat_ref):
    for hh in range(MLA_HEADS):
        cat_ref[:, hh * MLA_V:(hh + 1) * MLA_V] = _dot(o_ref[0, hh], wuv_ref[hh]).astype(BF16)
    y_ref[0] = x_ref[0] + _dot(cat_ref[...], wout_ref[...])


def _mla_out(o_lat, wuv, w_out, x, tm):
    nb, t, _ = x.shape
    assert t % tm == 0
    return pl.pallas_call(
        _mla_out_kernel,
        grid=(nb, t // tm),
        in_specs=[pl.BlockSpec((1, MLA_HEADS, tm, MLA_KV_LORA), lambda i, j: (i, 0, j, 0)),
                  _resident(wuv.shape), _resident(w_out.shape),
                  pl.BlockSpec((1, tm, D_MODEL), lambda i, j: (i, j, 0))],
        out_specs=pl.BlockSpec((1, tm, D_MODEL), lambda i, j: (i, j, 0)),
        out_shape=jax.ShapeDtypeStruct(x.shape, F32),
        scratch_shapes=[pltpu.VMEM((tm, MLA_HEADS * MLA_V), BF16)],
        compiler_params=_params(("parallel", "parallel"), 32),
        name="mla_out",
    )(o_lat, wuv, w_out, x)


def _pool_prompt_kernel(x_ref, xp_ref, g_ref, w_ref, sc_ref, o_ref, hl_ref, ext_ref):
    j = pl.program_id(1)
    x = x_ref[0]
    tm = x.shape[0]
    h = _rms(x, g_ref[...])
    hp = _rms(xp_ref[0], g_ref[...])
    ext_ref[0:POOL_PREV, :] = jnp.where(j == 0, 0.0, hp)
    ext_ref[POOL_PREV:, :] = h
    pos = (j * tm + lax.broadcasted_iota(jnp.int32, (tm, 1), 0)).astype(F32)
    outs = []
    for gi, w in enumerate(POOL_WINDOWS):
        lanes = slice(gi * POOL_GROUP_DIM, (gi + 1) * POOL_GROUP_DIM)
        acc = h[:, lanes]
        for k in range(1, w):
            acc = acc + ext_ref[POOL_PREV - k:POOL_PREV - k + tm, lanes]
        pooled = acc / jnp.minimum(pos + 1.0, float(w)) - h[:, lanes]
        outs.append(_dot(pooled.astype(BF16), w_ref[gi]))
    o_ref[0] = x + jnp.concatenate(outs, axis=-1) * sc_ref[...]

    @pl.when(j == pl.num_programs(1) - 1)
    def _():
        hl_ref[0] = h[tm - POOL_PREV:]


def _pool_prompt(x, gain, w_pool, scale):
    b, t, _ = x.shape
    tm = 512
    assert t % tm == 0 and tm % POOL_PREV == 0
    per_tile = tm // POOL_PREV
    return pl.pallas_call(
        _pool_prompt_kernel,
        grid=(b, t // tm),
        in_specs=[pl.BlockSpec((1, tm, D_MODEL), lambda i, j: (i, j, 0)),
                  pl.BlockSpec((1, POOL_PREV, D_MODEL),
                               lambda i, j: (i, jnp.maximum(j * per_tile - 1, 0), 0)),
                  _resident(gain.shape), _resident(w_pool.shape), _resident(scale.shape)],
        out_specs=[pl.BlockSpec((1, tm, D_MODEL), lambda i, j: (i, j, 0)),
                   pl.BlockSpec((1, POOL_PREV, D_MODEL), lambda i, j: (i, 0, 0))],
        out_shape=[jax.ShapeDtypeStruct(x.shape, F32),
                   jax.ShapeDtypeStruct((b, POOL_PREV, D_MODEL), F32)],
        scratch_shapes=[pltpu.VMEM((POOL_PREV + tm, D_MODEL), F32)],
        compiler_params=_params(("parallel", "arbitrary"), 32),
        name="pool_prompt",
    )(x, x, gain, w_pool, scale)


def _pool_sample_kernel(x_ref, pre_ref, g_ref, w_ref, sc_ref, o_ref, h_ref, *, pos0):
    t_len = x_ref.shape[0]
    hs = [_rms(x_ref[t], g_ref[...]) for t in range(t_len)]
    for t in range(t_len):
        h_ref[t] = hs[t]

    def ext(e, lanes):
        return pre_ref[e, :, lanes] if e < POOL_BUF else hs[e - POOL_BUF][:, lanes]

    for t in range(t_len):
        outs = []
        for gi, w in enumerate(POOL_WINDOWS):
            lanes = slice(gi * POOL_GROUP_DIM, (gi + 1) * POOL_GROUP_DIM)
            acc = hs[t][:, lanes]
            for k in range(1, w):
                acc = acc + ext(POOL_BUF + t - k, lanes)
            pooled = acc / min(pos0 + t + 1.0, float(w)) - hs[t][:, lanes]
            outs.append(_dot(pooled.astype(BF16), w_ref[gi]))
        o_ref[t] = x_ref[t] + jnp.concatenate(outs, axis=-1) * sc_ref[...]


def _pool_sample(x_t, prefix_t, gain, w_pool, scale, pos0):
    vmem = pl.BlockSpec(memory_space=pltpu.VMEM)
    return pl.pallas_call(
        functools.partial(_pool_sample_kernel, pos0=pos0),
        in_specs=[vmem] * 5,
        out_specs=[vmem, vmem],
        out_shape=[jax.ShapeDtypeStruct(x_t.shape, F32), jax.ShapeDtypeStruct(x_t.shape, F32)],
        compiler_params=_params(None, 32),
        name="pool_sample",
    )(x_t, prefix_t, gain, w_pool, scale)


def _rope_tables(pos, half):
    inv = ROPE_BASE ** (-jnp.arange(half, dtype=F32) / half)
    ang = pos.astype(F32)[:, None] * inv[None, :]
    return jnp.cos(ang), jnp.sin(ang)


def _mla_rope_tables(pos):
    cos, sin = _rope_tables(pos, MLA_ROPE // 2)
    zeros = jnp.zeros((pos.shape[0], 128 - MLA_ROPE), F32)
    return (jnp.concatenate([cos, cos, zeros], axis=-1),
            jnp.concatenate([-sin, sin, zeros], axis=-1))


def _swap_halves(w):
    half = w.shape[-1] // 2
    return jnp.concatenate([w[..., half:], w[..., :half]], axis=-1)


def _pad_lanes(w, width):
    return jnp.pad(w, [(0, 0)] * (w.ndim - 1) + [(0, width - w.shape[-1])])


def _row(v):
    return v.reshape(1, -1).astype(F32)


def kernel(x_prompt, x_sample, state_ret, cache_mla_latent, cache_mla_krope, page_table, state_pool,
           norm_mix, norm_ffn, norm_final,
           ret_w_in, ret_gn_gain, ret_w_out,
           cm_w_in, cm_ln_gain, cm_w_spatial, cm_b_spatial, cm_w_out,
           mla_w_down, mla_q_norm, mla_kv_norm, mla_w_uq, mla_w_uk, mla_w_uv, mla_w_out,
           pool_w, pool_scale,
           ffn_w_gate_up, ffn_w_down):
    bp, tp, _ = x_prompt.shape
    bs, ts, _ = x_sample.shape
    xp = x_prompt
    xs = x_sample.reshape(bs * ts, D_MODEL)
    pos_p = jnp.arange(tp)
    pos_s = PAST_LEN + jnp.arange(ts)

    wgu_all = ffn_w_gate_up.astype(BF16)
    wd_all = ffn_w_down.astype(BF16)

    def ffn(x, i, final):
        shape = x.shape
        y = _ffn(x.reshape(-1, D_MODEL), _row(norm_ffn[i]), wgu_all, wd_all, i,
                 _row(norm_final), final)
        return y.reshape(shape)

    log_g = jnp.log1p(-jnp.exp2(-5.0 - jnp.arange(RET_HEADS, dtype=F32)))
    w_in = ret_w_in[0].astype(BF16)
    w_out = ret_w_out[0].astype(BF16)
    gn_gain = _row(ret_gn_gain[0])
    cos_p, sin_p = _rope_tables(pos_p, RET_QK_DIM // 2)
    cos_s, sin_s = _rope_tables(pos_s, RET_QK_DIM // 2)
    xp, ret_state_p = _ret_prompt(xp, _row(norm_mix[0]), w_in, cos_p, sin_p,
                                  _ret_tables(float(RET_CHUNK), log_g), gn_gain, w_out)
    proj_s = _norm_matmul(xs, _row(norm_mix[0]), w_in, 1024).reshape(bs, ts, -1)
    y_s, ret_state_s = _ret_sample(proj_s, state_ret[0], cos_s, sin_s,
                                   _ret_tables(float(ts), log_g), gn_gain)
    xs = _matmul_residual(y_s.reshape(bs * ts, -1), w_out, xs)
    xp, xs = ffn(xp, 0, False), ffn(xs, 0, False)

    cw_in = cm_w_in[0].astype(BF16)
    cw_out = cm_w_out[0].astype(BF16)
    bias_full = jnp.repeat(jnp.transpose(cm_b_spatial[0]), CM_GROUP_DIM, axis=1)
    xp2, cm_v_p = _cm_prompt(xp.reshape(bp * tp, D_MODEL), _row(norm_mix[1]), cw_in,
                             _row(cm_ln_gain[0]), cm_w_spatial[0], bias_full, cw_out, tp)
    xp = xp2.reshape(bp, tp, D_MODEL)
    xs_t, cm_v_s_t = _cm_sample(xs.reshape(bs, ts, D_MODEL).transpose(1, 0, 2), _row(norm_mix[1]), cw_in,
                                _row(cm_ln_gain[0]),
                                cm_w_spatial[0][:, :ts, :ts].reshape(CM_GROUPS, ts * ts),
                                cm_b_spatial[0][:, :ts], cw_out)
    xs = xs_t.transpose(1, 0, 2).reshape(bs * ts, D_MODEL)
    cm_v_s = cm_v_s_t.transpose(1, 0, 2)
    xp, xs = ffn(xp, 1, False), ffn(xs, 1, False)

    wd = mla_w_down[0]
    kr_w = wd[:, MLA_Q_LORA + MLA_KV_LORA:]
    wd_ext = jnp.concatenate([wd[:, :MLA_Q_LORA + MLA_KV_LORA], _pad_lanes(kr_w, 128),
                              _pad_lanes(_swap_halves(kr_w), 128)], axis=-1).astype(BF16)
    wuq = mla_w_uq[0].reshape(MLA_Q_LORA, MLA_HEADS, MLA_NOPE + MLA_ROPE)
    wuq_rope = wuq[:, :, MLA_NOPE:]
    wuq_ext = jnp.concatenate([
        wuq[:, :, :MLA_NOPE].reshape(MLA_Q_LORA, -1),
        _pad_lanes(wuq_rope, 128).reshape(MLA_Q_LORA, -1),
        _pad_lanes(_swap_halves(wuq_rope), 128).reshape(MLA_Q_LORA, -1)], axis=-1).astype(BF16)
    wuk_t = mla_w_uk[0].transpose(1, 2, 0).astype(BF16)
    wuv = mla_w_uv[0].transpose(1, 0, 2).astype(BF16)
    mw_out = mla_w_out[0].astype(BF16)
    cos2_p, sin2_p = _mla_rope_tables(pos_p)
    cos2_s, sin2_s = _mla_rope_tables(jnp.tile(pos_s, bs))
    gq, gkv = _row(mla_q_norm[0]), _row(mla_kv_norm[0])
    q_p, lat_p, kr_p, kext_p = _mla_q(xp, _row(norm_mix[2]), wd_ext, gq, gkv, wuq_ext, wuk_t,
                                      cos2_p, sin2_p, 256)
    tq = 256
    vt_p = kext_p[:, :, :MLA_KV_LORA].reshape(bp, tp // tq, tq, MLA_KV_LORA).swapaxes(2, 3)
    o_p = _mla_attn_prompt(q_p, kext_p, vt_p)
    xp = _mla_out(o_p, wuv, mw_out, xp, 256)
    ns = bs * ts
    q_s, lat_s, kr_s, kext_s = _mla_q(xs.reshape(1, ns, D_MODEL), _row(norm_mix[2]), wd_ext, gq, gkv,
                                      wuq_ext, wuk_t, cos2_s, sin2_s, ns)
    q_s = q_s.reshape(MLA_HEADS, bs, ts, MLA_KEY_DIM).transpose(1, 0, 2, 3).reshape(bs, MLA_HEADS * ts, MLA_KEY_DIM)
    knew = jnp.pad(kext_s.reshape(bs, ts, MLA_KEY_DIM), ((0, 0), (0, PAGE_SIZE - ts), (0, 0)))
    o_s = _mla_attn_sample(page_table, q_s, cache_mla_latent[0],
                           jnp.swapaxes(cache_mla_krope[0], 1, 2), knew, ts)
    o_s = o_s.reshape(bs, MLA_HEADS, ts, MLA_KV_LORA).transpose(1, 0, 2, 3).reshape(1, MLA_HEADS, ns, MLA_KV_LORA)
    xs = _mla_out(o_s.astype(BF16), wuv, mw_out, xs.reshape(1, ns, D_MODEL), ns).reshape(ns, D_MODEL)
    lat_s = lat_s.reshape(bs, ts, MLA_KV_LORA)
    kr_s = kr_s.reshape(bs, ts, MLA_ROPE)
    xp, xs = ffn(xp, 2, False), ffn(xs, 2, False)

    pw = pool_w[0].astype(BF16)
    xp, h_last = _pool_prompt(xp, _row(norm_mix[3]), pw, _row(pool_scale[0]))
    pool_state_p = h_last[:, POOL_PREV - POOL_BUF:]
    xs_t, h_s_t = _pool_sample(xs.reshape(bs, ts, D_MODEL).transpose(1, 0, 2),
                               state_pool[0].transpose(1, 0, 2), _row(norm_mix[3]), pw,
                               _row(pool_scale[0]), float(PAST_LEN))
    xs = xs_t.transpose(1, 0, 2).reshape(bs * ts, D_MODEL)
    pool_state_s = jnp.concatenate([state_pool[0], h_s_t.transpose(1, 0, 2)], axis=1)[:, -POOL_BUF:]
    xp, xs = ffn(xp, 3, True), ffn(xs, 3, True)

    return (xp, xs.reshape(bs, ts, D_MODEL),
            ret_state_p[None], ret_state_s[None],
            cm_v_p.reshape(bp, CM_CHUNK, D_MODEL)[None], cm_v_s[None],
            lat_p[None], kr_p[None], lat_s[None], kr_s[None],
            pool_state_p[None], pool_state_s[None])
```

```python
import functools

import jax
import jax.numpy as jnp
from jax import lax
from jax.experimental import pallas as pl
from jax.experimental.pallas import tpu as pltpu

F32 = jnp.float32
BF16 = jnp.bfloat16

D_MODEL = 1024
NORM_EPS = 1e-6
ROPE_BASE = 10000.0
PAST_LEN = 8192
PAGE_SIZE = 128

RET_HEADS = 4
RET_QK_DIM = 256
RET_V_DIM = 512
RET_CHUNK = 128
RET_PROMPT_CHUNK = 256
RET_SAMPLE_ROWS_PER_STEP = 2

CM_CHUNK = 128
CM_GROUPS = 4
CM_GROUP_DIM = 256

MLA_HEADS = 8
MLA_Q_LORA = 384
MLA_KV_LORA = 256
MLA_NOPE = 128
MLA_ROPE = 64
MLA_V = 128
MLA_SCALE = (MLA_NOPE + MLA_ROPE) ** -0.5
MLA_KEY_DIM = MLA_KV_LORA + 128
MLA_KEY_CHUNK = 1024
MLA_ATTN_GROUP_HEADS = 8

POOL_WINDOWS = (2, 4, 8, 16)
POOL_GROUP_DIM = 256
POOL_BUF = 15
POOL_PREV = 32
POOL_ALIGN = 8
POOL_TAIL = 16

FFN_HIDDEN = 2816
FFN_CHUNK = 256

V7X_VMEM_BYTES = 64 * 1024 * 1024
MIB = 1024 * 1024


def _params(semantics, vmem_mib):
    assert vmem_mib * MIB < V7X_VMEM_BYTES
    return pltpu.CompilerParams(dimension_semantics=semantics,
                                vmem_limit_bytes=vmem_mib * MIB)


def _resident(shape):
    nd = len(shape)
    return pl.BlockSpec(shape, lambda *_: (0,) * nd, pipeline_mode=pl.Buffered(1))


def _rms(x, g):
    return x * lax.rsqrt(jnp.mean(x * x, axis=-1, keepdims=True) + NORM_EPS) * g


def _dot(a, b):
    return jnp.dot(a, b, preferred_element_type=F32)


def _dot_nt(a, b):
    return lax.dot_general(a, b, (((1,), (1,)), ((), ())), preferred_element_type=F32)


def _ffn_kernel(xp_ref, xs_ref, g_ref, wgu_ref, wd_ref, gf_ref, op_ref, os_ref, *, final):
    def tile(x_ref, o_ref):
        x = x_ref[...]
        h = _rms(x, g_ref[...]).astype(BF16)
        acc = x
        for j in range(FFN_HIDDEN // FFN_CHUNK):
            cols = slice(j * FFN_CHUNK, (j + 1) * FFN_CHUNK)
            ucols = slice(FFN_HIDDEN + j * FFN_CHUNK, FFN_HIDDEN + (j + 1) * FFN_CHUNK)
            g = _dot(h, wgu_ref[:, cols].astype(BF16))
            u = _dot(h, wgu_ref[:, ucols].astype(BF16))
            a = (g * jax.nn.sigmoid(g)) * u
            acc = acc + _dot(a.astype(BF16), wd_ref[cols, :].astype(BF16))
        o_ref[...] = _rms(acc, gf_ref[...]) if final else acc

    is_prompt = pl.program_id(0) < pl.num_programs(0) - 1

    @pl.when(is_prompt)
    def _():
        tile(xp_ref, op_ref)

    @pl.when(jnp.logical_not(is_prompt))
    def _():
        tile(xs_ref, os_ref)


def _ffn(xp, xs, gain, wgu_all, wd_all, layer, final_gain, final):
    tm = xs.shape[0]
    assert xp.shape[0] % tm == 0
    n_prompt = xp.shape[0] // tm

    def layer_slab(w):
        return pl.BlockSpec((None,) + w.shape[1:], lambda i: (layer, 0, 0),
                            pipeline_mode=pl.Buffered(1))

    prompt_tile = pl.BlockSpec((tm, D_MODEL), lambda i: (jnp.minimum(i, n_prompt - 1), 0))
    sample_tile = pl.BlockSpec((tm, D_MODEL), lambda i: (0, 0))
    return pl.pallas_call(
        functools.partial(_ffn_kernel, final=final),
        grid=(n_prompt + 1,),
        in_specs=[prompt_tile, sample_tile,
                  _resident(gain.shape), layer_slab(wgu_all),
                  layer_slab(wd_all), _resident(final_gain.shape)],
        out_specs=[prompt_tile, sample_tile],
        out_shape=[jax.ShapeDtypeStruct(xp.shape, F32), jax.ShapeDtypeStruct(xs.shape, F32)],
        compiler_params=_params(("arbitrary",), 58),
        name="ffn",
    )(xp, xs, gain, wgu_all, wd_all, final_gain)


def _norm_matmul_kernel(x_ref, g_ref, w_ref, o_ref):
    h = _rms(x_ref[...], g_ref[...]).astype(BF16)
    o_ref[...] = _dot(h, w_ref[...])


def _norm_matmul(x, gain, w, tn):
    n, dout = x.shape[0], w.shape[1]
    assert dout % tn == 0
    return pl.pallas_call(
        _norm_matmul_kernel,
        grid=(dout // tn,),
        in_specs=[pl.BlockSpec((n, D_MODEL), lambda j: (0, 0)),
                  pl.BlockSpec(gain.shape, lambda j: (0, 0)),
                  pl.BlockSpec((D_MODEL, tn), lambda j: (0, j))],
        out_specs=pl.BlockSpec((n, tn), lambda j: (0, j)),
        out_shape=jax.ShapeDtypeStruct((n, dout), F32),
        compiler_params=_params(("parallel",), 32),
        name="norm_matmul",
    )(x, gain, w)


def _matmul_residual_kernel(y_ref, w_ref, x_ref, o_ref):
    o_ref[...] = x_ref[...] + _dot(y_ref[...].astype(BF16), w_ref[...])


def _matmul_residual(y, w, x):
    n = x.shape[0]
    tm = min(n, 512)
    assert n % tm == 0
    return pl.pallas_call(
        _matmul_residual_kernel,
        grid=(n // tm,),
        in_specs=[pl.BlockSpec((tm, y.shape[1]), lambda i: (i, 0)),
                  _resident(w.shape),
                  pl.BlockSpec((tm, D_MODEL), lambda i: (i, 0))],
        out_specs=pl.BlockSpec((tm, D_MODEL), lambda i: (i, 0)),
        out_shape=jax.ShapeDtypeStruct(x.shape, F32),
        compiler_params=_params(("parallel",), 32),
        name="matmul_residual",
    )(y, w, x)


def _rope_halves(x, cos, sin):
    half = cos.shape[-1]
    x1, x2 = x[:, :half], x[:, half:]
    return jnp.concatenate([x1 * cos - x2 * sin, x1 * sin + x2 * cos], axis=-1)


def _ret_chunk(q, k, v, s, dmat, qdec, kdec, sdec):
    vb = v.astype(BF16)
    scores = _dot_nt(q.astype(BF16), k.astype(BF16)) * dmat
    o = _dot(scores.astype(BF16), vb) + _dot((q * qdec).astype(BF16), s.astype(BF16))
    s_new = sdec * s + _dot((k * kdec).T.astype(BF16), vb)
    return o, s_new


def _group_norm_gate(o, gate, gain):
    mu = jnp.mean(o, axis=-1, keepdims=True)
    oc = o - mu
    var = jnp.mean(oc * oc, axis=-1, keepdims=True)
    on = oc * lax.rsqrt(var + NORM_EPS)
    return (gate * jax.nn.sigmoid(gate)) * on * gain


def _ret_prompt_kernel(x_ref, g_ref, win_ref, cos_ref, sin_ref, dmat_ref, qdec_ref,
                       kdec_ref, sdec_ref, gng_ref, wout_ref, o_ref, st_ref,
                       s_ref, y_ref):
    j = pl.program_id(1)

    @pl.when(j == 0)
    def _():
        s_ref[...] = jnp.zeros_like(s_ref)

    x = x_ref[0]
    tm = x.shape[0]
    h = _rms(x, g_ref[...]).astype(BF16)
    for hh in range(RET_HEADS):
        qs = slice(hh * RET_QK_DIM, (hh + 1) * RET_QK_DIM)
        ks = slice(D_MODEL + hh * RET_QK_DIM, D_MODEL + (hh + 1) * RET_QK_DIM)
        vs = slice(2 * D_MODEL + hh * RET_V_DIM, 2 * D_MODEL + (hh + 1) * RET_V_DIM)
        gs = slice(4 * D_MODEL + hh * RET_V_DIM, 4 * D_MODEL + (hh + 1) * RET_V_DIM)
        q = _rope_halves(_dot(h, win_ref[:, qs]), cos_ref[...], sin_ref[...]) * (RET_QK_DIM ** -0.5)
        k = _rope_halves(_dot(h, win_ref[:, ks]), cos_ref[...], sin_ref[...])
        v = _dot(h, win_ref[:, vs])
        gate = _dot(h, win_ref[:, gs])
        gain = gng_ref[:, hh * RET_V_DIM:(hh + 1) * RET_V_DIM]
        chunk = dmat_ref.shape[1]
        for c in range(tm // chunk):
            rows = slice(c * chunk, (c + 1) * chunk)
            o, s_new = _ret_chunk(q[rows], k[rows], v[rows], s_ref[hh], dmat_ref[hh],
                                  qdec_ref[hh], kdec_ref[hh], sdec_ref[hh])
            s_ref[hh] = s_new
            y_ref[rows, hh * RET_V_DIM:(hh + 1) * RET_V_DIM] = (
                _group_norm_gate(o, gate[rows], gain).astype(BF16))
    o_ref[0] = x + _dot(y_ref[...], wout_ref[...])

    @pl.when(j == pl.num_programs(1) - 1)
    def _():
        st_ref[0] = s_ref[...]


def _ret_tables(length, chunk, log_g):
    idx = jnp.arange(chunk, dtype=F32)
    valid = idx < length
    rel = idx[:, None] - idx[None, :]
    ok = (rel >= 0) & valid[:, None] & valid[None, :]
    dmat = jnp.where(ok[None], jnp.exp(jnp.maximum(rel, 0.0)[None] * log_g[:, None, None]), 0.0)
    qd = jnp.where(valid[None], jnp.exp((idx + 1.0)[None, :] * log_g[:, None]), 0.0)
    kd = jnp.where(valid[None], jnp.exp((length - 1.0 - idx)[None, :] * log_g[:, None]), 0.0)
    qdec = jnp.broadcast_to(qd[:, :, None], (RET_HEADS, chunk, RET_QK_DIM))
    kdec = jnp.broadcast_to(kd[:, :, None], (RET_HEADS, chunk, RET_QK_DIM))
    sdec = jnp.broadcast_to(jnp.exp(length * log_g)[:, None, None], (RET_HEADS, 1, RET_V_DIM))
    return dmat, qdec, kdec, sdec


def _ret_prompt(x, gain, w_in, cos, sin, tables, gn_gain, w_out):
    b, t, _ = x.shape
    tm = 256
    assert t % tm == 0
    dmat, qdec, kdec, sdec = tables
    return pl.pallas_call(
        _ret_prompt_kernel,
        grid=(b, t // tm),
        in_specs=[pl.BlockSpec((1, tm, D_MODEL), lambda i, j: (i, j, 0)),
                  _resident(gain.shape), _resident(w_in.shape),
                  pl.BlockSpec((tm, RET_QK_DIM // 2), lambda i, j: (j, 0)),
                  pl.BlockSpec((tm, RET_QK_DIM // 2), lambda i, j: (j, 0)),
                  _resident(dmat.shape), _resident(qdec.shape), _resident(kdec.shape),
                  _resident(sdec.shape), _resident(gn_gain.shape), _resident(w_out.shape)],
        out_specs=[pl.BlockSpec((1, tm, D_MODEL), lambda i, j: (i, j, 0)),
                   pl.BlockSpec((1, RET_HEADS, RET_QK_DIM, RET_V_DIM), lambda i, j: (i, 0, 0, 0))],
        out_shape=[jax.ShapeDtypeStruct(x.shape, F32),
                   jax.ShapeDtypeStruct((b, RET_HEADS, RET_QK_DIM, RET_V_DIM), F32)],
        scratch_shapes=[pltpu.VMEM((RET_HEADS, RET_QK_DIM, RET_V_DIM), F32),
                        pltpu.VMEM((tm, RET_HEADS * RET_V_DIM), BF16)],
        compiler_params=_params(("parallel", "arbitrary"), 56),
        name="ret_prompt",
    )(x, gain, w_in, cos, sin, dmat, qdec, kdec, sdec, gn_gain, w_out)


def _ret_sample_kernel(p_ref, s0_ref, cos_ref, sin_ref, dmat_ref, qdec_ref, kdec_ref,
                       sdec_ref, gng_ref, y_ref, st_ref, qp_ref, kp_ref, vp_ref):
    @pl.when(pl.program_id(0) == 0)
    def _():
        qp_ref[...] = jnp.zeros_like(qp_ref)
        kp_ref[...] = jnp.zeros_like(kp_ref)
        vp_ref[...] = jnp.zeros_like(vp_ref)

    t = cos_ref.shape[0]
    for bb in range(s0_ref.shape[0]):
        rows = slice(bb * t, (bb + 1) * t)
        for hh in range(RET_HEADS):
            qs = slice(hh * RET_QK_DIM, (hh + 1) * RET_QK_DIM)
            ks = slice(D_MODEL + hh * RET_QK_DIM, D_MODEL + (hh + 1) * RET_QK_DIM)
            vs = slice(2 * D_MODEL + hh * RET_V_DIM, 2 * D_MODEL + (hh + 1) * RET_V_DIM)
            gs = slice(4 * D_MODEL + hh * RET_V_DIM, 4 * D_MODEL + (hh + 1) * RET_V_DIM)
            qp_ref[0:t, :] = (_rope_halves(p_ref[rows, qs], cos_ref[...], sin_ref[...])
                              * (RET_QK_DIM ** -0.5))
            kp_ref[0:t, :] = _rope_halves(p_ref[rows, ks], cos_ref[...], sin_ref[...])
            vp_ref[0:t, :] = p_ref[rows, vs]
            o, s_new = _ret_chunk(qp_ref[...], kp_ref[...], vp_ref[...], s0_ref[bb, hh], dmat_ref[hh],
                                  qdec_ref[hh], kdec_ref[hh], sdec_ref[hh])
            st_ref[bb, hh] = s_new
            gain = gng_ref[:, hh * RET_V_DIM:(hh + 1) * RET_V_DIM]
            y_ref[rows, hh * RET_V_DIM:(hh + 1) * RET_V_DIM] = _group_norm_gate(o[0:t], p_ref[rows, gs], gain)


def _ret_sample(proj, s0, cos, sin, tables, gn_gain):
    b = s0.shape[0]
    t = cos.shape[0]
    nb = RET_SAMPLE_ROWS_PER_STEP
    assert b % nb == 0 and proj.shape[0] == b * t
    dmat, qdec, kdec, sdec = tables
    state_spec = pl.BlockSpec((nb, RET_HEADS, RET_QK_DIM, RET_V_DIM), lambda i: (i, 0, 0, 0))
    return pl.pallas_call(
        _ret_sample_kernel,
        grid=(b // nb,),
        in_specs=[pl.BlockSpec((nb * t, proj.shape[1]), lambda i: (i, 0)),
                  state_spec,
                  _resident(cos.shape), _resident(sin.shape),
                  _resident(dmat.shape), _resident(qdec.shape), _resident(kdec.shape),
                  _resident(sdec.shape), _resident(gn_gain.shape)],
        out_specs=[pl.BlockSpec((nb * t, RET_HEADS * RET_V_DIM), lambda i: (i, 0)),
                   state_spec],
        out_shape=[jax.ShapeDtypeStruct((b * t, RET_HEADS * RET_V_DIM), F32),
                   jax.ShapeDtypeStruct(s0.shape, F32)],
        scratch_shapes=[pltpu.VMEM((RET_CHUNK, RET_QK_DIM), F32),
                        pltpu.VMEM((RET_CHUNK, RET_QK_DIM), F32),
                        pltpu.VMEM((RET_CHUNK, RET_V_DIM), F32)],
        compiler_params=_params(("arbitrary",), 32),
        name="ret_sample",
    )(proj, s0, cos, sin, dmat, qdec, kdec, sdec, gn_gain)


def _gelu_tanh(x):
    return x * (0.5 * (1.0 + jnp.tanh(0.7978845608028654 * (x + 0.044715 * (x * x * x)))))


def _layer_norm(v, gain):
    mu = jnp.mean(v, axis=-1, keepdims=True)
    vc = v - mu
    var = jnp.mean(vc * vc, axis=-1, keepdims=True)
    return vc * lax.rsqrt(var + NORM_EPS) * gain


def _cm_prompt_kernel(x_ref, g_ref, win_ref, lng_ref, ws_ref, bias_ref, wout_ref,
                      o_ref, v_ref, z_ref):
    x = x_ref[...]
    tm = x.shape[0]
    h = _rms(x, g_ref[...]).astype(BF16)
    u = _gelu_tanh(_dot(h, win_ref[:, :D_MODEL]))
    v = _layer_norm(_gelu_tanh(_dot(h, win_ref[:, D_MODEL:])), lng_ref[...])
    v_ref[...] = v[tm - CM_CHUNK:]
    vb = v.astype(BF16)
    row = lax.broadcasted_iota(jnp.int32, (CM_CHUNK, CM_CHUNK), 0)
    col = lax.broadcasted_iota(jnp.int32, (CM_CHUNK, CM_CHUNK), 1)
    for gi in range(CM_GROUPS):
        lanes = slice(gi * CM_GROUP_DIM, (gi + 1) * CM_GROUP_DIM)
        w = jnp.where(row >= col, ws_ref[gi], 0.0).astype(BF16)
        for c in range(tm // CM_CHUNK):
            rows = slice(c * CM_CHUNK, (c + 1) * CM_CHUNK)
            mixed = _dot(w, vb[rows, lanes]) + bias_ref[:, lanes]
            z_ref[rows, lanes] = (u[rows, lanes] * mixed).astype(BF16)
    o_ref[...] = x + _dot(z_ref[...], wout_ref[...])


def _cm_prompt(x, gain, w_in, ln_gain, w_s, bias_full, w_out, seq):
    n = x.shape[0]
    tm = 512
    assert seq % tm == 0 and n % seq == 0
    per_seq = seq // tm
    return pl.pallas_call(
        _cm_prompt_kernel,
        grid=(n // tm,),
        in_specs=[pl.BlockSpec((tm, D_MODEL), lambda i: (i, 0)),
                  _resident(gain.shape), _resident(w_in.shape), _resident(ln_gain.shape),
                  _resident(w_s.shape), _resident(bias_full.shape), _resident(w_out.shape)],
        out_specs=[pl.BlockSpec((tm, D_MODEL), lambda i: (i, 0)),
                   pl.BlockSpec((CM_CHUNK, D_MODEL), lambda i: (i // per_seq, 0))],
        out_shape=[jax.ShapeDtypeStruct(x.shape, F32),
                   jax.ShapeDtypeStruct((n // seq * CM_CHUNK, D_MODEL), F32)],
        scratch_shapes=[pltpu.VMEM((tm, D_MODEL), BF16)],
        compiler_params=_params(("arbitrary",), 40),
        name="cm_prompt",
    )(x, gain, w_in, ln_gain, w_s, bias_full, w_out)


def _cm_sample_kernel(ws_ref, bs_ref, x_ref, g_ref, win_ref, lng_ref, wout_ref, o_ref, v_ref):
    t_len = x_ref.shape[0]
    us, vs = [], []
    for t in range(t_len):
        h = _rms(x_ref[t], g_ref[...]).astype(BF16)
        us.append(_gelu_tanh(_dot(h, win_ref[:, :D_MODEL])))
        v = _layer_norm(_gelu_tanh(_dot(h, win_ref[:, D_MODEL:])), lng_ref[...])
        v_ref[t] = v
        vs.append(v)
    for t in range(t_len):
        parts = []
        for gi in range(CM_GROUPS):
            lanes = slice(gi * CM_GROUP_DIM, (gi + 1) * CM_GROUP_DIM)
            mixed = jnp.full_like(vs[t][:, lanes], bs_ref[gi, t])
            for s in range(t + 1):
                mixed = mixed + ws_ref[gi, t * t_len + s] * vs[s][:, lanes]
            parts.append(us[t][:, lanes] * mixed)
        z = jnp.concatenate(parts, axis=-1).astype(BF16)
        o_ref[t] = x_ref[t] + _dot(z, wout_ref[...])


def _cm_sample(x_t, gain, w_in, ln_gain, w_s_small, b_s_small, w_out):
    smem = pl.BlockSpec(memory_space=pltpu.SMEM)
    vmem = pl.BlockSpec(memory_space=pltpu.VMEM)
    return pl.pallas_call(
        _cm_sample_kernel,
        in_specs=[smem, smem, vmem, vmem, vmem, vmem, vmem],
        out_specs=[vmem, vmem],
        out_shape=[jax.ShapeDtypeStruct(x_t.shape, F32), jax.ShapeDtypeStruct(x_t.shape, F32)],
        compiler_params=_params(None, 32),
        name="cm_sample",
    )(w_s_small, b_s_small, x_t, gain, w_in, ln_gain, w_out)


def _mla_q_kernel(x_ref, g_ref, wd_ref, gq_ref, gkv_ref, wuq_ref, wuk_ref, cos_ref, sin_ref,
                  q_ref, c_ref, kr_ref, kext_ref):
    h = _rms(x_ref[0], g_ref[...]).astype(BF16)
    d = _dot(h, wd_ref[...])
    cq = _rms(d[:, :MLA_Q_LORA], gq_ref[...]).astype(BF16)
    ckv = _rms(d[:, MLA_Q_LORA:MLA_Q_LORA + MLA_KV_LORA], gkv_ref[...])
    cos, sin = cos_ref[...], sin_ref[...]
    base = MLA_Q_LORA + MLA_KV_LORA
    krp = d[:, base:base + 128] * cos + d[:, base + 128:base + 256] * sin
    c_ref[0] = ckv
    kr_ref[0] = krp[:, :MLA_ROPE]
    kext_ref[0] = jnp.concatenate([ckv.astype(BF16), krp.astype(BF16)], axis=-1)
    nope_w = MLA_HEADS * MLA_NOPE
    for hh in range(MLA_HEADS):
        qn = _dot(cq, wuq_ref[:, hh * MLA_NOPE:(hh + 1) * MLA_NOPE])
        raw = _dot(cq, wuq_ref[:, nope_w + hh * 128:nope_w + (hh + 1) * 128])
        rot = _dot(cq, wuq_ref[:, 2 * nope_w + hh * 128:2 * nope_w + (hh + 1) * 128])
        ql = _dot(qn.astype(BF16), wuk_ref[hh]) * MLA_SCALE
        qr = (raw * cos + rot * sin) * MLA_SCALE
        q_ref[0, hh] = jnp.concatenate([ql.astype(BF16), qr.astype(BF16)], axis=-1)


def _mla_q(x, gain, wd_ext, g_q, g_kv, wuq_ext, wuk_t, cos2, sin2, tm):
    nb, t, _ = x.shape
    assert t % tm == 0
    return pl.pallas_call(
        _mla_q_kernel,
        grid=(nb, t // tm),
        in_specs=[pl.BlockSpec((1, tm, D_MODEL), lambda i, j: (i, j, 0)),
                  _resident(gain.shape), _resident(wd_ext.shape), _resident(g_q.shape),
                  _resident(g_kv.shape), _resident(wuq_ext.shape), _resident(wuk_t.shape),
                  pl.BlockSpec((tm, 128), lambda i, j: (j, 0)),
                  pl.BlockSpec((tm, 128), lambda i, j: (j, 0))],
        out_specs=[pl.BlockSpec((1, MLA_HEADS, tm, MLA_KEY_DIM), lambda i, j: (i, 0, j, 0)),
                   pl.BlockSpec((1, tm, MLA_KV_LORA), lambda i, j: (i, j, 0)),
                   pl.BlockSpec((1, tm, MLA_ROPE), lambda i, j: (i, j, 0)),
                   pl.BlockSpec((1, tm, MLA_KEY_DIM), lambda i, j: (i, j, 0))],
        out_shape=[jax.ShapeDtypeStruct((nb, MLA_HEADS, t, MLA_KEY_DIM), BF16),
                   jax.ShapeDtypeStruct((nb, t, MLA_KV_LORA), F32),
                   jax.ShapeDtypeStruct((nb, t, MLA_ROPE), F32),
                   jax.ShapeDtypeStruct((nb, t, MLA_KEY_DIM), BF16)],
        compiler_params=_params(("parallel", "parallel"), 40),
        name="mla_q",
    )(x, gain, wd_ext, g_q, g_kv, wuq_ext, wuk_t, cos2, sin2)


def _mla_attn_prompt_kernel(q_ref, k_ref, vt_ref, o_ref, m_ref, l_ref, acc_ref, *, tq):
    qi = pl.program_id(1)
    gw = MLA_ATTN_GROUP_HEADS * tq
    key_idx = lax.broadcasted_iota(jnp.int32, (tq, gw), 0)
    qry_idx = lax.broadcasted_iota(jnp.int32, (tq, gw), 1) & (tq - 1)
    m_ref[...] = jnp.full_like(m_ref, -jnp.inf)
    l_ref[...] = jnp.zeros_like(l_ref)
    acc_ref[...] = jnp.zeros_like(acc_ref)

    def kv_step(kb, diagonal):
        k = k_ref[0, pl.ds(pl.multiple_of(kb * tq, tq), tq), :]
        vt = vt_ref[0, kb]
        for g in range(MLA_HEADS // MLA_ATTN_GROUP_HEADS):
            heads = slice(g * MLA_ATTN_GROUP_HEADS, (g + 1) * MLA_ATTN_GROUP_HEADS)
            lanes = slice(g * gw, (g + 1) * gw)
            q = q_ref[0, heads].reshape(gw, MLA_KEY_DIM)
            st = _dot_nt(k, q)
            if diagonal:
                st = jnp.where(key_idx <= qry_idx, st, -jnp.inf)
            m_old = m_ref[:, lanes]
            m_new = jnp.maximum(m_old, jnp.max(st, axis=0, keepdims=True))
            p = jnp.exp(st - m_new)
            alpha = jnp.exp(m_old - m_new)
            l_ref[:, lanes] = alpha * l_ref[:, lanes] + jnp.sum(p, axis=0, keepdims=True)
            acc_ref[:, lanes] = alpha * acc_ref[:, lanes] + _dot(vt, p.astype(BF16))
            m_ref[:, lanes] = m_new

    def kv_pair(j, c):
        kv_step(2 * j, False)
        kv_step(2 * j + 1, False)
        return c

    lax.fori_loop(0, qi // 2, kv_pair, 0)

    @pl.when(qi % 2 == 1)
    def _():
        kv_step(qi - 1, False)

    kv_step(qi, True)
    o = acc_ref[...] / l_ref[...]
    o_ref[0] = o.T.reshape(MLA_HEADS, tq, MLA_KV_LORA).astype(BF16)


def _mla_attn_prompt(q, kext, vt):
    b, _, t, _ = q.shape
    tq = vt.shape[-1]
    assert t % tq == 0
    return pl.pallas_call(
        functools.partial(_mla_attn_prompt_kernel, tq=tq),
        grid=(b, t // tq),
        in_specs=[pl.BlockSpec((1, MLA_HEADS, tq, MLA_KEY_DIM), lambda i, j: (i, 0, j, 0)),
                  pl.BlockSpec((1, t, MLA_KEY_DIM), lambda i, j: (i, 0, 0)),
                  pl.BlockSpec((1, t // tq, MLA_KV_LORA, tq), lambda i, j: (i, 0, 0, 0))],
        out_specs=pl.BlockSpec((1, MLA_HEADS, tq, MLA_KV_LORA), lambda i, j: (i, 0, j, 0)),
        out_shape=jax.ShapeDtypeStruct((b, MLA_HEADS, t, MLA_KV_LORA), BF16),
        scratch_shapes=[pltpu.VMEM((1, MLA_HEADS * tq), F32), pltpu.VMEM((1, MLA_HEADS * tq), F32),
                        pltpu.VMEM((MLA_KV_LORA, MLA_HEADS * tq), F32)],
        compiler_params=_params(("parallel", "arbitrary"), 32),
        name="mla_attn_prompt",
    )(q, kext, vt)


def _mla_attn_sample_kernel(pt_ref, q_ref, knew_ref, lat_hbm, krt_hbm, o_ref,
                            cbuf, rbuf, s_ref, sems, *, t_new, n_pages):
    b = pl.program_id(0)
    slot = b % 2
    n_keys = n_pages * PAGE_SIZE

    def page_copies(row, slot_):
        copies = []
        for p in range(n_pages):
            page = pt_ref[row * n_pages + p]
            keys = pl.ds(p * PAGE_SIZE, PAGE_SIZE)
            copies.append(pltpu.make_async_copy(lat_hbm.at[page], cbuf.at[slot_, keys, :],
                                                sems.at[0, slot_]))
            copies.append(pltpu.make_async_copy(krt_hbm.at[page], rbuf.at[slot_, :, keys],
                                                sems.at[1, slot_]))
        return copies

    @pl.when(b == 0)
    def _():
        for cp in page_copies(b, slot):
            cp.start()

    @pl.when(b + 1 < pl.num_programs(0))
    def _():
        for cp in page_copies(b + 1, 1 - slot):
            cp.start()

    for cp in page_copies(b, slot):
        cp.wait()

    q = q_ref[0]
    ql = q[:, :MLA_KV_LORA]
    qr = q[:, MLA_KV_LORA:MLA_KV_LORA + MLA_ROPE]
    for j in range(n_keys // MLA_KEY_CHUNK):
        keys = slice(j * MLA_KEY_CHUNK, (j + 1) * MLA_KEY_CHUNK)
        c = cbuf[slot, keys, :].astype(BF16)
        rt = rbuf[slot, :, keys].astype(BF16)
        s_ref[:, keys] = _dot_nt(ql, c) + _dot(qr, rt)
    kn = knew_ref[0]
    sn = _dot_nt(q, kn)
    row_t = lax.broadcasted_iota(jnp.int32, sn.shape, 0) % t_new
    key = lax.broadcasted_iota(jnp.int32, sn.shape, 1)
    s_ref[:, n_keys:] = jnp.where((key < t_new) & (key <= row_t), sn, -jnp.inf)
    s = s_ref[...]
    e = jnp.exp(s - jnp.max(s, axis=-1, keepdims=True))
    denom = jnp.sum(e, axis=-1, keepdims=True)
    eb = e.astype(BF16)
    acc = _dot(eb[:, n_keys:], kn[:, :MLA_KV_LORA])
    for j in range(n_keys // MLA_KEY_CHUNK):
        keys = slice(j * MLA_KEY_CHUNK, (j + 1) * MLA_KEY_CHUNK)
        acc = acc + _dot(eb[:, keys], cbuf[slot, keys, :].astype(BF16))
    o_ref[0] = acc / denom


def _mla_attn_sample(page_table, q, cache_lat, cache_krt, knew, t_new):
    b, rows, _ = q.shape
    n_pages = page_table.shape[1]
    n_keys = n_pages * PAGE_SIZE
    assert n_keys % MLA_KEY_CHUNK == 0
    grid_spec = pltpu.PrefetchScalarGridSpec(
        num_scalar_prefetch=1,
        grid=(b,),
        in_specs=[pl.BlockSpec((1, rows, MLA_KEY_DIM), lambda i, pt: (i, 0, 0)),
                  pl.BlockSpec((1, PAGE_SIZE, MLA_KEY_DIM), lambda i, pt: (i, 0, 0)),
                  pl.BlockSpec(memory_space=pl.ANY),
                  pl.BlockSpec(memory_space=pl.ANY)],
        out_specs=pl.BlockSpec((1, rows, MLA_KV_LORA), lambda i, pt: (i, 0, 0)),
        scratch_shapes=[pltpu.VMEM((2, n_keys, MLA_KV_LORA), F32),
                        pltpu.VMEM((2, MLA_ROPE, n_keys), F32),
                        pltpu.VMEM((rows, n_keys + PAGE_SIZE), F32),
                        pltpu.SemaphoreType.DMA((2, 2))],
    )
    return pl.pallas_call(
        functools.partial(_mla_attn_sample_kernel, t_new=t_new, n_pages=n_pages),
        grid_spec=grid_spec,
        out_shape=jax.ShapeDtypeStruct((b, rows, MLA_KV_LORA), F32),
        compiler_params=_params(("arbitrary",), 40),
        name="mla_attn_sample",
    )(page_table.reshape(-1), q, knew, cache_lat, cache_krt)


def _mla_out_kernel(o_ref, wuv_ref, wout_ref, x_ref, y_ref, cat_ref):
    for hh in range(MLA_HEADS):
        cat_ref[:, hh * MLA_V:(hh + 1) * MLA_V] = _dot(o_ref[0, hh], wuv_ref[hh]).astype(BF16)
    y_ref[0] = x_ref[0] + _dot(cat_ref[...], wout_ref[...])


def _mla_out(o_lat, wuv, w_out, x, tm):
    nb, t, _ = x.shape
    assert t % tm == 0
    return pl.pallas_call(
        _mla_out_kernel,
        grid=(nb, t // tm),
        in_specs=[pl.BlockSpec((1, MLA_HEADS, tm, MLA_KV_LORA), lambda i, j: (i, 0, j, 0)),
                  _resident(wuv.shape), _resident(w_out.shape),
                  pl.BlockSpec((1, tm, D_MODEL), lambda i, j: (i, j, 0))],
        out_specs=pl.BlockSpec((1, tm, D_MODEL), lambda i, j: (i, j, 0)),
        out_shape=jax.ShapeDtypeStruct(x.shape, F32),
        scratch_shapes=[pltpu.VMEM((tm, MLA_HEADS * MLA_V), BF16)],
        compiler_params=_params(("parallel", "parallel"), 32),
        name="mla_out",
    )(o_lat, wuv, w_out, x)


def _pool_prompt_kernel(x_ref, xp_ref, g_ref, w_ref, sc_ref, o_ref, hl_ref, ext_ref, tmp_ref):
    j = pl.program_id(1)
    x = x_ref[0]
    tm = x.shape[0]
    h = _rms(x, g_ref[...])
    hp = _rms(xp_ref[0], g_ref[...])
    ext_ref[0:POOL_PREV, :] = jnp.where(j == 0, 0.0, hp)
    ext_ref[POOL_PREV:, :] = h
    tmp_ref[0:POOL_ALIGN, :] = jnp.zeros((POOL_ALIGN, D_MODEL), F32)
    n = POOL_PREV + tm - POOL_ALIGN
    pos = (j * tm + lax.broadcasted_iota(jnp.int32, (tm, 1), 0)).astype(F32)
    outs = []
    for gi, w in enumerate(POOL_WINDOWS):
        lanes = slice(gi * POOL_GROUP_DIM, (gi + 1) * POOL_GROUP_DIM)
        src, dst = ext_ref, tmp_ref
        shift = 1
        while shift < w:
            dst[POOL_ALIGN:, lanes] = (src[POOL_ALIGN:, lanes]
                                       + src[POOL_ALIGN - shift:POOL_ALIGN - shift + n, lanes])
            src, dst = dst, src
            shift *= 2
        pooled = src[POOL_PREV:, lanes] / jnp.minimum(pos + 1.0, float(w)) - h[:, lanes]
        outs.append(_dot(pooled.astype(BF16), w_ref[gi]))
    o_ref[0] = x + jnp.concatenate(outs, axis=-1) * sc_ref[...]

    @pl.when(j == pl.num_programs(1) - 1)
    def _():
        hl_ref[0] = h[tm - POOL_TAIL:]


def _pool_prompt(x, gain, w_pool, scale):
    b, t, _ = x.shape
    tm = 512
    assert t % tm == 0 and tm % POOL_PREV == 0
    per_tile = tm // POOL_PREV
    return pl.pallas_call(
        _pool_prompt_kernel,
        grid=(b, t // tm),
        in_specs=[pl.BlockSpec((1, tm, D_MODEL), lambda i, j: (i, j, 0)),
                  pl.BlockSpec((1, POOL_PREV, D_MODEL),
                               lambda i, j: (i, jnp.maximum(j * per_tile - 1, 0), 0)),
                  _resident(gain.shape), _resident(w_pool.shape), _resident(scale.shape)],
        out_specs=[pl.BlockSpec((1, tm, D_MODEL), lambda i, j: (i, j, 0)),
                   pl.BlockSpec((1, POOL_TAIL, D_MODEL), lambda i, j: (i, 0, 0))],
        out_shape=[jax.ShapeDtypeStruct(x.shape, F32),
                   jax.ShapeDtypeStruct((b, POOL_TAIL, D_MODEL), F32)],
        scratch_shapes=[pltpu.VMEM((POOL_PREV + tm, D_MODEL), F32),
                        pltpu.VMEM((POOL_PREV + tm, D_MODEL), F32)],
        compiler_params=_params(("parallel", "arbitrary"), 32),
        name="pool_prompt",
    )(x, x, gain, w_pool, scale)


def _pool_sample_kernel(x_ref, pre_ref, g_ref, w_ref, sc_ref, o_ref, h_ref, *, pos0):
    t_len = x_ref.shape[0]
    hs = [_rms(x_ref[t], g_ref[...]) for t in range(t_len)]
    for t in range(t_len):
        h_ref[t] = hs[t]

    def ext(e, lanes):
        return pre_ref[e, :, lanes] if e < POOL_BUF else hs[e - POOL_BUF][:, lanes]

    for t in range(t_len):
        outs = []
        for gi, w in enumerate(POOL_WINDOWS):
            lanes = slice(gi * POOL_GROUP_DIM, (gi + 1) * POOL_GROUP_DIM)
            acc = hs[t][:, lanes]
            for k in range(1, w):
                acc = acc + ext(POOL_BUF + t - k, lanes)
            pooled = acc / min(pos0 + t + 1.0, float(w)) - hs[t][:, lanes]
            outs.append(_dot(pooled.astype(BF16), w_ref[gi]))
        o_ref[t] = x_ref[t] + jnp.concatenate(outs, axis=-1) * sc_ref[...]


def _pool_sample(x_t, prefix_t, gain, w_pool, scale, pos0):
    vmem = pl.BlockSpec(memory_space=pltpu.VMEM)
    return pl.pallas_call(
        functools.partial(_pool_sample_kernel, pos0=pos0),
        in_specs=[vmem] * 5,
        out_specs=[vmem, vmem],
        out_shape=[jax.ShapeDtypeStruct(x_t.shape, F32), jax.ShapeDtypeStruct(x_t.shape, F32)],
        compiler_params=_params(None, 32),
        name="pool_sample",
    )(x_t, prefix_t, gain, w_pool, scale)


def _rope_tables(pos, half):
    inv = ROPE_BASE ** (-jnp.arange(half, dtype=F32) / half)
    ang = pos.astype(F32)[:, None] * inv[None, :]
    return jnp.cos(ang), jnp.sin(ang)


def _mla_rope_tables(pos):
    cos, sin = _rope_tables(pos, MLA_ROPE // 2)
    zeros = jnp.zeros((pos.shape[0], 128 - MLA_ROPE), F32)
    return (jnp.concatenate([cos, cos, zeros], axis=-1),
            jnp.concatenate([-sin, sin, zeros], axis=-1))


def _swap_halves(w):
    half = w.shape[-1] // 2
    return jnp.concatenate([w[..., half:], w[..., :half]], axis=-1)


def _pad_lanes(w, width):
    return jnp.pad(w, [(0, 0)] * (w.ndim - 1) + [(0, width - w.shape[-1])])


def _row(v):
    return v.reshape(1, -1).astype(F32)


def kernel(x_prompt, x_sample, state_ret, cache_mla_latent, cache_mla_krope, page_table, state_pool,
           norm_mix, norm_ffn, norm_final,
           ret_w_in, ret_gn_gain, ret_w_out,
           cm_w_in, cm_ln_gain, cm_w_spatial, cm_b_spatial, cm_w_out,
           mla_w_down, mla_q_norm, mla_kv_norm, mla_w_uq, mla_w_uk, mla_w_uv, mla_w_out,
           pool_w, pool_scale,
           ffn_w_gate_up, ffn_w_down):
    bp, tp, _ = x_prompt.shape
    bs, ts, _ = x_sample.shape
    xp = x_prompt
    xs = x_sample.reshape(bs * ts, D_MODEL)
    pos_p = jnp.arange(tp)
    pos_s = PAST_LEN + jnp.arange(ts)

    def ffn(xp_, xs_, i, final):
        yp, ys = _ffn(xp_.reshape(-1, D_MODEL), xs_, _row(norm_ffn[i]), ffn_w_gate_up, ffn_w_down, i,
                      _row(norm_final), final)
        return yp.reshape(xp_.shape), ys

    log_g = jnp.log1p(-jnp.exp2(-5.0 - jnp.arange(RET_HEADS, dtype=F32)))
    w_in = ret_w_in[0].astype(BF16)
    w_out = ret_w_out[0].astype(BF16)
    gn_gain = _row(ret_gn_gain[0])
    cos_p, sin_p = _rope_tables(pos_p, RET_QK_DIM // 2)
    cos_s, sin_s = _rope_tables(pos_s, RET_QK_DIM // 2)
    xp, ret_state_p = _ret_prompt(xp, _row(norm_mix[0]), w_in, cos_p, sin_p,
                                  _ret_tables(float(RET_PROMPT_CHUNK), RET_PROMPT_CHUNK, log_g),
                                  gn_gain, w_out)
    proj_s = _norm_matmul(xs, _row(norm_mix[0]), w_in, 1024)
    y_s, ret_state_s = _ret_sample(proj_s, state_ret[0], cos_s, sin_s,
                                   _ret_tables(float(ts), RET_CHUNK, log_g), gn_gain)
    xs = _matmul_residual(y_s, w_out, xs)
    xp, xs = ffn(xp, xs, 0, False)

    cw_in = cm_w_in[0].astype(BF16)
    cw_out = cm_w_out[0].astype(BF16)
    bias_full = jnp.repeat(jnp.transpose(cm_b_spatial[0]), CM_GROUP_DIM, axis=1)
    xp2, cm_v_p = _cm_prompt(xp.reshape(bp * tp, D_MODEL), _row(norm_mix[1]), cw_in,
                             _row(cm_ln_gain[0]), cm_w_spatial[0], bias_full, cw_out, tp)
    xp = xp2.reshape(bp, tp, D_MODEL)
    xs_t, cm_v_s_t = _cm_sample(xs.reshape(bs, ts, D_MODEL).transpose(1, 0, 2), _row(norm_mix[1]), cw_in,
                                _row(cm_ln_gain[0]),
                                cm_w_spatial[0][:, :ts, :ts].reshape(CM_GROUPS, ts * ts),
                                cm_b_spatial[0][:, :ts], cw_out)
    xs = xs_t.transpose(1, 0, 2).reshape(bs * ts, D_MODEL)
    cm_v_s = cm_v_s_t.transpose(1, 0, 2)
    xp, xs = ffn(xp, xs, 1, False)

    wd = mla_w_down[0]
    kr_w = wd[:, MLA_Q_LORA + MLA_KV_LORA:]
    wd_ext = jnp.concatenate([wd[:, :MLA_Q_LORA + MLA_KV_LORA], _pad_lanes(kr_w, 128),
                              _pad_lanes(_swap_halves(kr_w), 128)], axis=-1).astype(BF16)
    wuq = mla_w_uq[0].reshape(MLA_Q_LORA, MLA_HEADS, MLA_NOPE + MLA_ROPE)
    wuq_rope = wuq[:, :, MLA_NOPE:]
    wuq_ext = jnp.concatenate([
        wuq[:, :, :MLA_NOPE].reshape(MLA_Q_LORA, -1),
        _pad_lanes(wuq_rope, 128).reshape(MLA_Q_LORA, -1),
        _pad_lanes(_swap_halves(wuq_rope), 128).reshape(MLA_Q_LORA, -1)], axis=-1).astype(BF16)
    wuk_t = mla_w_uk[0].transpose(1, 2, 0).astype(BF16)
    wuv = mla_w_uv[0].transpose(1, 0, 2).astype(BF16)
    mw_out = mla_w_out[0].astype(BF16)
    cos2_p, sin2_p = _mla_rope_tables(pos_p)
    cos2_s, sin2_s = _mla_rope_tables(jnp.tile(pos_s, bs))
    gq, gkv = _row(mla_q_norm[0]), _row(mla_kv_norm[0])
    q_p, lat_p, kr_p, kext_p = _mla_q(xp, _row(norm_mix[2]), wd_ext, gq, gkv, wuq_ext, wuk_t,
                                      cos2_p, sin2_p, 256)
    tq = 256
    vt_p = kext_p[:, :, :MLA_KV_LORA].reshape(bp, tp // tq, tq, MLA_KV_LORA).swapaxes(2, 3)
    o_p = _mla_attn_prompt(q_p, kext_p, vt_p)
    xp = _mla_out(o_p, wuv, mw_out, xp, 256)
    ns = bs * ts
    q_s, lat_s, kr_s, kext_s = _mla_q(xs.reshape(1, ns, D_MODEL), _row(norm_mix[2]), wd_ext, gq, gkv,
                                      wuq_ext, wuk_t, cos2_s, sin2_s, ns)
    q_s = q_s.reshape(MLA_HEADS, bs, ts, MLA_KEY_DIM).transpose(1, 0, 2, 3).reshape(bs, MLA_HEADS * ts, MLA_KEY_DIM)
    knew = jnp.pad(kext_s.reshape(bs, ts, MLA_KEY_DIM), ((0, 0), (0, PAGE_SIZE - ts), (0, 0)))
    o_s = _mla_attn_sample(page_table, q_s, cache_mla_latent[0],
                           jnp.swapaxes(cache_mla_krope[0], 1, 2), knew, ts)
    o_s = o_s.reshape(bs, MLA_HEADS, ts, MLA_KV_LORA).transpose(1, 0, 2, 3).reshape(1, MLA_HEADS, ns, MLA_KV_LORA)
    xs = _mla_out(o_s.astype(BF16), wuv, mw_out, xs.reshape(1, ns, D_MODEL), ns).reshape(ns, D_MODEL)
    lat_s = lat_s.reshape(bs, ts, MLA_KV_LORA)
    kr_s = kr_s.reshape(bs, ts, MLA_ROPE)
    xp, xs = ffn(xp, xs, 2, False)

    pw = pool_w[0].astype(BF16)
    xp, h_last = _pool_prompt(xp, _row(norm_mix[3]), pw, _row(pool_scale[0]))
    pool_state_p = h_last[:, POOL_TAIL - POOL_BUF:]
    xs_t, h_s_t = _pool_sample(xs.reshape(bs, ts, D_MODEL).transpose(1, 0, 2),
                               state_pool[0].transpose(1, 0, 2), _row(norm_mix[3]), pw,
                               _row(pool_scale[0]), float(PAST_LEN))
    xs = xs_t.transpose(1, 0, 2).reshape(bs * ts, D_MODEL)
    pool_state_s = jnp.concatenate([state_pool[0], h_s_t.transpose(1, 0, 2)], axis=1)[:, -POOL_BUF:]
    xp, xs = ffn(xp, xs, 3, True)

    return (xp, xs.reshape(bs, ts, D_MODEL),
            ret_state_p[None], ret_state_s[None],
            cm_v_p.reshape(bp, CM_CHUNK, D_MODEL)[None], cm_v_s[None],
            lat_p[None], kr_p[None], lat_s[None], kr_s[None],
            pool_state_p[None], pool_state_s[None])
```

```python
import functools

import jax
import jax.numpy as jnp
from jax import lax
from jax.experimental import pallas as pl
from jax.experimental.pallas import tpu as pltpu

F32 = jnp.float32
BF16 = jnp.bfloat16

D_MODEL = 1024
NORM_EPS = 1e-6
ROPE_BASE = 10000.0
PAST_LEN = 8192
PAGE_SIZE = 128

RET_HEADS = 4
RET_QK_DIM = 256
RET_V_DIM = 512
RET_CHUNK = 128
RET_PROMPT_CHUNK = 256
RET_SAMPLE_ROWS_PER_STEP = 4

CM_CHUNK = 128
CM_GROUPS = 4
CM_GROUP_DIM = 256

MLA_HEADS = 8
MLA_Q_LORA = 384
MLA_KV_LORA = 256
MLA_NOPE = 128
MLA_ROPE = 64
MLA_V = 128
MLA_SCALE = (MLA_NOPE + MLA_ROPE) ** -0.5
MLA_KEY_DIM = MLA_KV_LORA + 128
MLA_KEY_CHUNK = 1024
MLA_SAMPLE_ROWS_PER_STEP = 2
MLA_ATTN_TQ = 256

POOL_WINDOWS = (2, 4, 8, 16)
POOL_GROUP_DIM = 256
POOL_BUF = 15
POOL_PREV = 32
POOL_ALIGN = 8
POOL_TAIL = 16

FFN_HIDDEN = 2816
FFN_CHUNK = 256

V7X_VMEM_BYTES = 64 * 1024 * 1024
MIB = 1024 * 1024


def _params(semantics, vmem_mib):
    assert vmem_mib * MIB < V7X_VMEM_BYTES
    return pltpu.CompilerParams(dimension_semantics=semantics,
                                vmem_limit_bytes=vmem_mib * MIB)


def _resident(shape):
    nd = len(shape)
    return pl.BlockSpec(shape, lambda *_: (0,) * nd, pipeline_mode=pl.Buffered(1))


def _rms(x, g):
    return x * lax.rsqrt(jnp.mean(x * x, axis=-1, keepdims=True) + NORM_EPS) * g


def _dot(a, b):
    return jnp.dot(a, b, preferred_element_type=F32)


def _dot_nt(a, b):
    return lax.dot_general(a, b, (((1,), (1,)), ((), ())), preferred_element_type=F32)


def _ffn_kernel(xp_ref, xs_ref, g_ref, wgu_ref, wd_ref, gf_ref, op_ref, os_ref, *, final):
    def tile(x_ref, o_ref):
        x = x_ref[...]
        h = _rms(x, g_ref[...]).astype(BF16)
        acc = x
        for j in range(FFN_HIDDEN // FFN_CHUNK):
            cols = slice(j * FFN_CHUNK, (j + 1) * FFN_CHUNK)
            ucols = slice(FFN_HIDDEN + j * FFN_CHUNK, FFN_HIDDEN + (j + 1) * FFN_CHUNK)
            g = _dot(h, wgu_ref[:, cols].astype(BF16))
            u = _dot(h, wgu_ref[:, ucols].astype(BF16))
            a = (g * jax.nn.sigmoid(g)) * u
            acc = acc + _dot(a.astype(BF16), wd_ref[cols, :].astype(BF16))
        o_ref[...] = _rms(acc, gf_ref[...]) if final else acc

    is_prompt = pl.program_id(0) < pl.num_programs(0) - 1

    @pl.when(is_prompt)
    def _():
        tile(xp_ref, op_ref)

    @pl.when(jnp.logical_not(is_prompt))
    def _():
        tile(xs_ref, os_ref)


def _ffn(xp, xs, gain, wgu_all, wd_all, layer, final_gain, final):
    tm = xs.shape[0]
    assert xp.shape[0] % tm == 0
    n_prompt = xp.shape[0] // tm

    def layer_slab(w):
        return pl.BlockSpec((None,) + w.shape[1:], lambda i: (layer, 0, 0),
                            pipeline_mode=pl.Buffered(1))

    prompt_tile = pl.BlockSpec((tm, D_MODEL), lambda i: (jnp.minimum(i, n_prompt - 1), 0))
    sample_tile = pl.BlockSpec((tm, D_MODEL), lambda i: (0, 0))
    return pl.pallas_call(
        functools.partial(_ffn_kernel, final=final),
        grid=(n_prompt + 1,),
        in_specs=[prompt_tile, sample_tile,
                  _resident(gain.shape), layer_slab(wgu_all),
                  layer_slab(wd_all), _resident(final_gain.shape)],
        out_specs=[prompt_tile, sample_tile],
        out_shape=[jax.ShapeDtypeStruct(xp.shape, F32), jax.ShapeDtypeStruct(xs.shape, F32)],
        compiler_params=_params(("arbitrary",), 58),
        name="ffn",
    )(xp, xs, gain, wgu_all, wd_all, final_gain)


def _norm_matmul_kernel(x_ref, g_ref, w_ref, o_ref):
    h = _rms(x_ref[...], g_ref[...]).astype(BF16)
    o_ref[...] = _dot(h, w_ref[...])


def _norm_matmul(x, gain, w, tn):
    n, dout = x.shape[0], w.shape[1]
    assert dout % tn == 0
    return pl.pallas_call(
        _norm_matmul_kernel,
        grid=(dout // tn,),
        in_specs=[pl.BlockSpec((n, D_MODEL), lambda j: (0, 0)),
                  pl.BlockSpec(gain.shape, lambda j: (0, 0)),
                  pl.BlockSpec((D_MODEL, tn), lambda j: (0, j))],
        out_specs=pl.BlockSpec((n, tn), lambda j: (0, j)),
        out_shape=jax.ShapeDtypeStruct((n, dout), F32),
        compiler_params=_params(("parallel",), 32),
        name="norm_matmul",
    )(x, gain, w)


def _matmul_residual_kernel(y_ref, w_ref, x_ref, o_ref):
    o_ref[...] = x_ref[...] + _dot(y_ref[...].astype(BF16), w_ref[...])


def _matmul_residual(y, w, x):
    n = x.shape[0]
    tm = min(n, 512)
    assert n % tm == 0
    return pl.pallas_call(
        _matmul_residual_kernel,
        grid=(n // tm,),
        in_specs=[pl.BlockSpec((tm, y.shape[1]), lambda i: (i, 0)),
                  _resident(w.shape),
                  pl.BlockSpec((tm, D_MODEL), lambda i: (i, 0))],
        out_specs=pl.BlockSpec((tm, D_MODEL), lambda i: (i, 0)),
        out_shape=jax.ShapeDtypeStruct(x.shape, F32),
        compiler_params=_params(("parallel",), 32),
        name="matmul_residual",
    )(y, w, x)


def _rope_halves(x, cos, sin):
    half = cos.shape[-1]
    x1, x2 = x[:, :half], x[:, half:]
    return jnp.concatenate([x1 * cos - x2 * sin, x1 * sin + x2 * cos], axis=-1)


def _ret_chunk(q, k, v, s, dmat, qdec, kdec, sdec):
    vb = v.astype(BF16)
    scores = _dot_nt(q.astype(BF16), k.astype(BF16)) * dmat
    o = _dot(scores.astype(BF16), vb) + _dot((q * qdec).astype(BF16), s.astype(BF16))
    s_new = sdec * s + _dot((k * kdec).T.astype(BF16), vb)
    return o, s_new


def _group_norm_gate(o, gate, gain):
    mu = jnp.mean(o, axis=-1, keepdims=True)
    oc = o - mu
    var = jnp.mean(oc * oc, axis=-1, keepdims=True)
    on = oc * lax.rsqrt(var + NORM_EPS)
    return (gate * jax.nn.sigmoid(gate)) * on * gain


def _ret_prompt_kernel(x_ref, g_ref, win_ref, cos_ref, sin_ref, dmat_ref, qdec_ref,
                       kdec_ref, sdec_ref, gng_ref, wout_ref, o_ref, st_ref,
                       s_ref, y_ref):
    j = pl.program_id(1)

    @pl.when(j == 0)
    def _():
        s_ref[...] = jnp.zeros_like(s_ref)

    x = x_ref[0]
    tm = x.shape[0]
    h = _rms(x, g_ref[...]).astype(BF16)
    for hh in range(RET_HEADS):
        qs = slice(hh * RET_QK_DIM, (hh + 1) * RET_QK_DIM)
        ks = slice(D_MODEL + hh * RET_QK_DIM, D_MODEL + (hh + 1) * RET_QK_DIM)
        vs = slice(2 * D_MODEL + hh * RET_V_DIM, 2 * D_MODEL + (hh + 1) * RET_V_DIM)
        gs = slice(4 * D_MODEL + hh * RET_V_DIM, 4 * D_MODEL + (hh + 1) * RET_V_DIM)
        q = _rope_halves(_dot(h, win_ref[:, qs]), cos_ref[...], sin_ref[...]) * (RET_QK_DIM ** -0.5)
        k = _rope_halves(_dot(h, win_ref[:, ks]), cos_ref[...], sin_ref[...])
        v = _dot(h, win_ref[:, vs])
        gate = _dot(h, win_ref[:, gs])
        gain = gng_ref[:, hh * RET_V_DIM:(hh + 1) * RET_V_DIM]
        chunk = dmat_ref.shape[1]
        for c in range(tm // chunk):
            rows = slice(c * chunk, (c + 1) * chunk)
            o, s_new = _ret_chunk(q[rows], k[rows], v[rows], s_ref[hh], dmat_ref[hh],
                                  qdec_ref[hh], kdec_ref[hh], sdec_ref[hh])
            s_ref[hh] = s_new
            y_ref[rows, hh * RET_V_DIM:(hh + 1) * RET_V_DIM] = (
                _group_norm_gate(o, gate[rows], gain).astype(BF16))
    o_ref[0] = x + _dot(y_ref[...], wout_ref[...])

    @pl.when(j == pl.num_programs(1) - 1)
    def _():
        st_ref[0] = s_ref[...]


def _ret_tables(length, chunk, log_g):
    idx = jnp.arange(chunk, dtype=F32)
    valid = idx < length
    rel = idx[:, None] - idx[None, :]
    ok = (rel >= 0) & valid[:, None] & valid[None, :]
    dmat = jnp.where(ok[None], jnp.exp(jnp.maximum(rel, 0.0)[None] * log_g[:, None, None]), 0.0)
    qd = jnp.where(valid[None], jnp.exp((idx + 1.0)[None, :] * log_g[:, None]), 0.0)
    kd = jnp.where(valid[None], jnp.exp((length - 1.0 - idx)[None, :] * log_g[:, None]), 0.0)
    qdec = jnp.broadcast_to(qd[:, :, None], (RET_HEADS, chunk, RET_QK_DIM))
    kdec = jnp.broadcast_to(kd[:, :, None], (RET_HEADS, chunk, RET_QK_DIM))
    sdec = jnp.broadcast_to(jnp.exp(length * log_g)[:, None, None], (RET_HEADS, 1, RET_V_DIM))
    return dmat, qdec, kdec, sdec


def _ret_prompt(x, gain, w_in, cos, sin, tables, gn_gain, w_out):
    b, t, _ = x.shape
    tm = 512
    assert t % tm == 0
    dmat, qdec, kdec, sdec = tables
    return pl.pallas_call(
        _ret_prompt_kernel,
        grid=(b, t // tm),
        in_specs=[pl.BlockSpec((1, tm, D_MODEL), lambda i, j: (i, j, 0)),
                  _resident(gain.shape), _resident(w_in.shape),
                  pl.BlockSpec((tm, RET_QK_DIM // 2), lambda i, j: (j, 0)),
                  pl.BlockSpec((tm, RET_QK_DIM // 2), lambda i, j: (j, 0)),
                  _resident(dmat.shape), _resident(qdec.shape), _resident(kdec.shape),
                  _resident(sdec.shape), _resident(gn_gain.shape), _resident(w_out.shape)],
        out_specs=[pl.BlockSpec((1, tm, D_MODEL), lambda i, j: (i, j, 0)),
                   pl.BlockSpec((1, RET_HEADS, RET_QK_DIM, RET_V_DIM), lambda i, j: (i, 0, 0, 0))],
        out_shape=[jax.ShapeDtypeStruct(x.shape, F32),
                   jax.ShapeDtypeStruct((b, RET_HEADS, RET_QK_DIM, RET_V_DIM), F32)],
        scratch_shapes=[pltpu.VMEM((RET_HEADS, RET_QK_DIM, RET_V_DIM), F32),
                        pltpu.VMEM((tm, RET_HEADS * RET_V_DIM), BF16)],
        compiler_params=_params(("parallel", "arbitrary"), 56),
        name="ret_prompt",
    )(x, gain, w_in, cos, sin, dmat, qdec, kdec, sdec, gn_gain, w_out)


def _ret_sample_kernel(p_ref, s0_ref, cos_ref, sin_ref, dmat_ref, qdec_ref, kdec_ref,
                       sdec_ref, gng_ref, y_ref, st_ref, qp_ref, kp_ref, vp_ref):
    @pl.when(pl.program_id(0) == 0)
    def _():
        qp_ref[...] = jnp.zeros_like(qp_ref)
        kp_ref[...] = jnp.zeros_like(kp_ref)
        vp_ref[...] = jnp.zeros_like(vp_ref)

    t = cos_ref.shape[0]
    for bb in range(s0_ref.shape[0]):
        rows = slice(bb * t, (bb + 1) * t)
        for hh in range(RET_HEADS):
            qs = slice(hh * RET_QK_DIM, (hh + 1) * RET_QK_DIM)
            ks = slice(D_MODEL + hh * RET_QK_DIM, D_MODEL + (hh + 1) * RET_QK_DIM)
            vs = slice(2 * D_MODEL + hh * RET_V_DIM, 2 * D_MODEL + (hh + 1) * RET_V_DIM)
            gs = slice(4 * D_MODEL + hh * RET_V_DIM, 4 * D_MODEL + (hh + 1) * RET_V_DIM)
            qp_ref[0:t, :] = (_rope_halves(p_ref[rows, qs], cos_ref[...], sin_ref[...])
                              * (RET_QK_DIM ** -0.5))
            kp_ref[0:t, :] = _rope_halves(p_ref[rows, ks], cos_ref[...], sin_ref[...])
            vp_ref[0:t, :] = p_ref[rows, vs]
            o, s_new = _ret_chunk(qp_ref[...], kp_ref[...], vp_ref[...], s0_ref[bb, hh], dmat_ref[hh],
                                  qdec_ref[hh], kdec_ref[hh], sdec_ref[hh])
            st_ref[bb, hh] = s_new
            gain = gng_ref[:, hh * RET_V_DIM:(hh + 1) * RET_V_DIM]
            y_ref[rows, hh * RET_V_DIM:(hh + 1) * RET_V_DIM] = _group_norm_gate(o[0:t], p_ref[rows, gs], gain)


def _ret_sample(proj, s0, cos, sin, tables, gn_gain):
    b = s0.shape[0]
    t = cos.shape[0]
    nb = RET_SAMPLE_ROWS_PER_STEP
    assert b % nb == 0 and proj.shape[0] == b * t
    dmat, qdec, kdec, sdec = tables
    state_spec = pl.BlockSpec((nb, RET_HEADS, RET_QK_DIM, RET_V_DIM), lambda i: (i, 0, 0, 0))
    return pl.pallas_call(
        _ret_sample_kernel,
        grid=(b // nb,),
        in_specs=[pl.BlockSpec((nb * t, proj.shape[1]), lambda i: (i, 0)),
                  state_spec,
                  _resident(cos.shape), _resident(sin.shape),
                  _resident(dmat.shape), _resident(qdec.shape), _resident(kdec.shape),
                  _resident(sdec.shape), _resident(gn_gain.shape)],
        out_specs=[pl.BlockSpec((nb * t, RET_HEADS * RET_V_DIM), lambda i: (i, 0)),
                   state_spec],
        out_shape=[jax.ShapeDtypeStruct((b * t, RET_HEADS * RET_V_DIM), F32),
                   jax.ShapeDtypeStruct(s0.shape, F32)],
        scratch_shapes=[pltpu.VMEM((RET_CHUNK, RET_QK_DIM), F32),
                        pltpu.VMEM((RET_CHUNK, RET_QK_DIM), F32),
                        pltpu.VMEM((RET_CHUNK, RET_V_DIM), F32)],
        compiler_params=_params(("arbitrary",), 48),
        name="ret_sample",
    )(proj, s0, cos, sin, dmat, qdec, kdec, sdec, gn_gain)


def _gelu_tanh(x):
    return x * (0.5 * (1.0 + jnp.tanh(0.7978845608028654 * (x + 0.044715 * (x * x * x)))))


def _layer_norm(v, gain):
    mu = jnp.mean(v, axis=-1, keepdims=True)
    vc = v - mu
    var = jnp.mean(vc * vc, axis=-1, keepdims=True)
    return vc * lax.rsqrt(var + NORM_EPS) * gain


def _cm_prompt_kernel(x_ref, g_ref, win_ref, lng_ref, ws_ref, bias_ref, wout_ref,
                      o_ref, v_ref, z_ref):
    x = x_ref[...]
    tm = x.shape[0]
    h = _rms(x, g_ref[...]).astype(BF16)
    u = _gelu_tanh(_dot(h, win_ref[:, :D_MODEL]))
    v = _layer_norm(_gelu_tanh(_dot(h, win_ref[:, D_MODEL:])), lng_ref[...])
    v_ref[...] = v[tm - CM_CHUNK:]
    vb = v.astype(BF16)
    row = lax.broadcasted_iota(jnp.int32, (CM_CHUNK, CM_CHUNK), 0)
    col = lax.broadcasted_iota(jnp.int32, (CM_CHUNK, CM_CHUNK), 1)
    for gi in range(CM_GROUPS):
        lanes = slice(gi * CM_GROUP_DIM, (gi + 1) * CM_GROUP_DIM)
        w = jnp.where(row >= col, ws_ref[gi], 0.0).astype(BF16)
        for c in range(tm // CM_CHUNK):
            rows = slice(c * CM_CHUNK, (c + 1) * CM_CHUNK)
            mixed = _dot(w, vb[rows, lanes]) + bias_ref[:, lanes]
            z_ref[rows, lanes] = (u[rows, lanes] * mixed).astype(BF16)
    o_ref[...] = x + _dot(z_ref[...], wout_ref[...])


def _cm_prompt(x, gain, w_in, ln_gain, w_s, bias_full, w_out, seq):
    n = x.shape[0]
    tm = 512
    assert seq % tm == 0 and n % seq == 0
    per_seq = seq // tm
    return pl.pallas_call(
        _cm_prompt_kernel,
        grid=(n // tm,),
        in_specs=[pl.BlockSpec((tm, D_MODEL), lambda i: (i, 0)),
                  _resident(gain.shape), _resident(w_in.shape), _resident(ln_gain.shape),
                  _resident(w_s.shape), _resident(bias_full.shape), _resident(w_out.shape)],
        out_specs=[pl.BlockSpec((tm, D_MODEL), lambda i: (i, 0)),
                   pl.BlockSpec((CM_CHUNK, D_MODEL), lambda i: (i // per_seq, 0))],
        out_shape=[jax.ShapeDtypeStruct(x.shape, F32),
                   jax.ShapeDtypeStruct((n // seq * CM_CHUNK, D_MODEL), F32)],
        scratch_shapes=[pltpu.VMEM((tm, D_MODEL), BF16)],
        compiler_params=_params(("arbitrary",), 40),
        name="cm_prompt",
    )(x, gain, w_in, ln_gain, w_s, bias_full, w_out)


def _cm_sample_kernel(ws_ref, bs_ref, x_ref, g_ref, win_ref, lng_ref, wout_ref, o_ref, v_ref):
    t_len = x_ref.shape[0]
    us, vs = [], []
    for t in range(t_len):
        h = _rms(x_ref[t], g_ref[...]).astype(BF16)
        us.append(_gelu_tanh(_dot(h, win_ref[:, :D_MODEL])))
        v = _layer_norm(_gelu_tanh(_dot(h, win_ref[:, D_MODEL:])), lng_ref[...])
        v_ref[t] = v
        vs.append(v)
    for t in range(t_len):
        parts = []
        for gi in range(CM_GROUPS):
            lanes = slice(gi * CM_GROUP_DIM, (gi + 1) * CM_GROUP_DIM)
            mixed = jnp.full_like(vs[t][:, lanes], bs_ref[gi, t])
            for s in range(t + 1):
                mixed = mixed + ws_ref[gi, t * t_len + s] * vs[s][:, lanes]
            parts.append(us[t][:, lanes] * mixed)
        z = jnp.concatenate(parts, axis=-1).astype(BF16)
        o_ref[t] = x_ref[t] + _dot(z, wout_ref[...])


def _cm_sample(x_t, gain, w_in, ln_gain, w_s_small, b_s_small, w_out):
    smem = pl.BlockSpec(memory_space=pltpu.SMEM)
    vmem = pl.BlockSpec(memory_space=pltpu.VMEM)
    return pl.pallas_call(
        _cm_sample_kernel,
        in_specs=[smem, smem, vmem, vmem, vmem, vmem, vmem],
        out_specs=[vmem, vmem],
        out_shape=[jax.ShapeDtypeStruct(x_t.shape, F32), jax.ShapeDtypeStruct(x_t.shape, F32)],
        compiler_params=_params(None, 32),
        name="cm_sample",
    )(w_s_small, b_s_small, x_t, gain, w_in, ln_gain, w_out)


def _mla_q_kernel(x_ref, g_ref, wd_ref, gq_ref, gkv_ref, wuq_ref, wuk_ref, cos_ref, sin_ref,
                  q_ref, c_ref, kr_ref, kext_ref):
    h = _rms(x_ref[0], g_ref[...]).astype(BF16)
    d = _dot(h, wd_ref[...])
    cq = _rms(d[:, :MLA_Q_LORA], gq_ref[...]).astype(BF16)
    ckv = _rms(d[:, MLA_Q_LORA:MLA_Q_LORA + MLA_KV_LORA], gkv_ref[...])
    cos, sin = cos_ref[...], sin_ref[...]
    base = MLA_Q_LORA + MLA_KV_LORA
    krp = d[:, base:base + 128] * cos + d[:, base + 128:base + 256] * sin
    c_ref[0] = ckv
    kr_ref[0] = krp[:, :MLA_ROPE]
    kext_ref[0] = jnp.concatenate([ckv.astype(BF16), krp.astype(BF16)], axis=-1)
    nope_w = MLA_HEADS * MLA_NOPE
    for hh in range(MLA_HEADS):
        qn = _dot(cq, wuq_ref[:, hh * MLA_NOPE:(hh + 1) * MLA_NOPE])
        raw = _dot(cq, wuq_ref[:, nope_w + hh * 128:nope_w + (hh + 1) * 128])
        rot = _dot(cq, wuq_ref[:, 2 * nope_w + hh * 128:2 * nope_w + (hh + 1) * 128])
        ql = _dot(qn.astype(BF16), wuk_ref[hh]) * MLA_SCALE
        qr = (raw * cos + rot * sin) * MLA_SCALE
        q_ref[0, hh] = jnp.concatenate([ql.astype(BF16), qr.astype(BF16)], axis=-1)


def _mla_q(x, gain, wd_ext, g_q, g_kv, wuq_ext, wuk_t, cos2, sin2, tm):
    nb, t, _ = x.shape
    assert t % tm == 0
    return pl.pallas_call(
        _mla_q_kernel,
        grid=(nb, t // tm),
        in_specs=[pl.BlockSpec((1, tm, D_MODEL), lambda i, j: (i, j, 0)),
                  _resident(gain.shape), _resident(wd_ext.shape), _resident(g_q.shape),
                  _resident(g_kv.shape), _resident(wuq_ext.shape), _resident(wuk_t.shape),
                  pl.BlockSpec((tm, 128), lambda i, j: (j, 0)),
                  pl.BlockSpec((tm, 128), lambda i, j: (j, 0))],
        out_specs=[pl.BlockSpec((1, MLA_HEADS, tm, MLA_KEY_DIM), lambda i, j: (i, 0, j, 0)),
                   pl.BlockSpec((1, tm, MLA_KV_LORA), lambda i, j: (i, j, 0)),
                   pl.BlockSpec((1, tm, MLA_ROPE), lambda i, j: (i, j, 0)),
                   pl.BlockSpec((1, tm, MLA_KEY_DIM), lambda i, j: (i, j, 0))],
        out_shape=[jax.ShapeDtypeStruct((nb, MLA_HEADS, t, MLA_KEY_DIM), BF16),
                   jax.ShapeDtypeStruct((nb, t, MLA_KV_LORA), F32),
                   jax.ShapeDtypeStruct((nb, t, MLA_ROPE), F32),
                   jax.ShapeDtypeStruct((nb, t, MLA_KEY_DIM), BF16)],
        compiler_params=_params(("parallel", "parallel"), 40),
        name="mla_q",
    )(x, gain, wd_ext, g_q, g_kv, wuq_ext, wuk_t, cos2, sin2)


def _mla_attn_prompt_kernel(q_ref, k_ref, vt_ref, o_ref, m_ref, l_ref, acc_ref, *, tq):
    qi = pl.program_id(1)
    tk = 2 * tq
    cols = MLA_HEADS * tq
    q = q_ref[0].reshape(cols, MLA_KEY_DIM)
    m_ref[...] = jnp.full_like(m_ref, -jnp.inf)
    l_ref[...] = jnp.zeros_like(l_ref)
    acc_ref[...] = jnp.zeros_like(acc_ref)

    def kv_step(kb, n_keys, first_masked_key):
        k = k_ref[0, pl.ds(pl.multiple_of(kb * tk, tk), n_keys), :]
        st = _dot_nt(k, q)
        if first_masked_key is not None:
            key_idx = lax.broadcasted_iota(jnp.int32, (n_keys, cols), 0) - first_masked_key
            qry_idx = lax.broadcasted_iota(jnp.int32, (n_keys, cols), 1) & (tq - 1)
            st = jnp.where(key_idx <= qry_idx, st, -jnp.inf)
        m_old = m_ref[...]
        m_new = jnp.maximum(m_old, jnp.max(st, axis=0, keepdims=True))
        p = jnp.exp(st - m_new)
        alpha = jnp.exp(m_old - m_new)
        l_ref[...] = alpha * l_ref[...] + jnp.sum(p, axis=0, keepdims=True)
        acc_ref[...] = alpha * acc_ref[...] + _dot(vt_ref[0, kb, :, :n_keys], p.astype(BF16))
        m_ref[...] = m_new

    def kv_body(kb, c):
        kv_step(kb, tk, None)
        return c

    n_full = qi // 2
    lax.fori_loop(0, n_full, kv_body, 0)

    @pl.when(qi % 2 == 1)
    def _():
        kv_step(n_full, tk, tq)

    @pl.when(qi % 2 == 0)
    def _():
        kv_step(n_full, tq, 0)

    o = acc_ref[...] / l_ref[...]
    o_ref[0] = o.T.reshape(MLA_HEADS, tq, MLA_KV_LORA).astype(BF16)


def _mla_attn_prompt(q, kext, vt):
    b, _, t, _ = q.shape
    tq = vt.shape[-1] // 2
    assert t % (2 * tq) == 0 and tq & (tq - 1) == 0
    return pl.pallas_call(
        functools.partial(_mla_attn_prompt_kernel, tq=tq),
        grid=(b, t // tq),
        in_specs=[pl.BlockSpec((1, MLA_HEADS, tq, MLA_KEY_DIM), lambda i, j: (i, 0, j, 0)),
                  pl.BlockSpec((1, t, MLA_KEY_DIM), lambda i, j: (i, 0, 0)),
                  pl.BlockSpec((1, t // (2 * tq), MLA_KV_LORA, 2 * tq), lambda i, j: (i, 0, 0, 0))],
        out_specs=pl.BlockSpec((1, MLA_HEADS, tq, MLA_KV_LORA), lambda i, j: (i, 0, j, 0)),
        out_shape=jax.ShapeDtypeStruct((b, MLA_HEADS, t, MLA_KV_LORA), BF16),
        scratch_shapes=[pltpu.VMEM((1, MLA_HEADS * tq), F32), pltpu.VMEM((1, MLA_HEADS * tq), F32),
                        pltpu.VMEM((MLA_KV_LORA, MLA_HEADS * tq), F32)],
        compiler_params=_params(("parallel", "arbitrary"), 32),
        name="mla_attn_prompt",
    )(q, kext, vt)


def _mla_attn_sample_kernel(pt_ref, q_ref, knew_ref, lat_hbm, krt_hbm, o_ref,
                            cbuf, rbuf, s_ref, sems, *, t_new, n_pages):
    g = pl.program_id(0)
    slot = g % 2
    n_rows = q_ref.shape[0]
    n_keys = n_pages * PAGE_SIZE
    n_chunks = n_keys // MLA_KEY_CHUNK

    def page_copies(group, slot_):
        copies = []
        for r in range(n_rows):
            for p in range(n_pages):
                page = pt_ref[(group * n_rows + r) * n_pages + p]
                keys = pl.ds(p * PAGE_SIZE, PAGE_SIZE)
                copies.append(pltpu.make_async_copy(lat_hbm.at[page], cbuf.at[slot_, r, keys, :],
                                                    sems.at[0, slot_]))
                copies.append(pltpu.make_async_copy(krt_hbm.at[page], rbuf.at[slot_, r, :, keys],
                                                    sems.at[1, slot_]))
        return copies

    @pl.when(g == 0)
    def _():
        for cp in page_copies(g, slot):
            cp.start()

    @pl.when(g + 1 < pl.num_programs(0))
    def _():
        for cp in page_copies(g + 1, 1 - slot):
            cp.start()

    for cp in page_copies(g, slot):
        cp.wait()

    def past_scores(r, j):
        keys = slice(j * MLA_KEY_CHUNK, (j + 1) * MLA_KEY_CHUNK)
        q = q_ref[r]
        c = cbuf[slot, r, keys, :].astype(BF16)
        rt = rbuf[slot, r, :, keys].astype(BF16)
        s_ref[r, :, keys] = (_dot_nt(q[:, :MLA_KV_LORA], c)
                             + _dot(q[:, MLA_KV_LORA:MLA_KV_LORA + MLA_ROPE], rt))

    for j in range(n_chunks):
        past_scores(0, j)
    for r in range(n_rows):
        kn = knew_ref[r]
        sn = _dot_nt(q_ref[r], kn)
        row_t = lax.broadcasted_iota(jnp.int32, sn.shape, 0) % t_new
        key = lax.broadcasted_iota(jnp.int32, sn.shape, 1)
        s_ref[r, :, n_keys:] = jnp.where((key < t_new) & (key <= row_t), sn, -jnp.inf)
        s = s_ref[r]
        e = jnp.exp(s - jnp.max(s, axis=-1, keepdims=True))
        denom = jnp.sum(e, axis=-1, keepdims=True)
        eb = e.astype(BF16)
        acc = _dot(eb[:, n_keys:], kn[:, :MLA_KV_LORA])
        for j in range(n_chunks):
            keys = slice(j * MLA_KEY_CHUNK, (j + 1) * MLA_KEY_CHUNK)
            acc = acc + _dot(eb[:, keys], cbuf[slot, r, keys, :].astype(BF16))
            if r + 1 < n_rows:
                past_scores(r + 1, j)
        o_ref[r] = acc / denom


def _mla_attn_sample(page_table, q, cache_lat, cache_krt, knew, t_new):
    b, rows, _ = q.shape
    n_pages = page_table.shape[1]
    n_keys = n_pages * PAGE_SIZE
    nr = MLA_SAMPLE_ROWS_PER_STEP
    assert n_keys % MLA_KEY_CHUNK == 0 and b % nr == 0
    grid_spec = pltpu.PrefetchScalarGridSpec(
        num_scalar_prefetch=1,
        grid=(b // nr,),
        in_specs=[pl.BlockSpec((nr, rows, MLA_KEY_DIM), lambda i, pt: (i, 0, 0)),
                  pl.BlockSpec((nr, PAGE_SIZE, MLA_KEY_DIM), lambda i, pt: (i, 0, 0)),
                  pl.BlockSpec(memory_space=pl.ANY),
                  pl.BlockSpec(memory_space=pl.ANY)],
        out_specs=pl.BlockSpec((nr, rows, MLA_KV_LORA), lambda i, pt: (i, 0, 0)),
        scratch_shapes=[pltpu.VMEM((2, nr, n_keys, MLA_KV_LORA), F32),
                        pltpu.VMEM((2, nr, MLA_ROPE, n_keys), F32),
                        pltpu.VMEM((nr, rows, n_keys + PAGE_SIZE), F32),
                        pltpu.SemaphoreType.DMA((2, 2))],
    )
    return pl.pallas_call(
        functools.partial(_mla_attn_sample_kernel, t_new=t_new, n_pages=n_pages),
        grid_spec=grid_spec,
        out_shape=jax.ShapeDtypeStruct((b, rows, MLA_KV_LORA), F32),
        compiler_params=_params(("arbitrary",), 56),
        name="mla_attn_sample",
    )(page_table.reshape(-1), q, knew, cache_lat, cache_krt)


def _mla_out_kernel(o_ref, wuv_ref, wout_ref, x_ref, y_ref, cat_ref):
    for hh in range(MLA_HEADS):
        cat_ref[:, hh * MLA_V:(hh + 1) * MLA_V] = _dot(o_ref[0, hh], wuv_ref[hh]).astype(BF16)
    y_ref[0] = x_ref[0] + _dot(cat_ref[...], wout_ref[...])


def _mla_out(o_lat, wuv, w_out, x, tm):
    nb, t, _ = x.shape
    assert t % tm == 0
    return pl.pallas_call(
        _mla_out_kernel,
        grid=(nb, t // tm),
        in_specs=[pl.BlockSpec((1, MLA_HEADS, tm, MLA_KV_LORA), lambda i, j: (i, 0, j, 0)),
                  _resident(wuv.shape), _resident(w_out.shape),
                  pl.BlockSpec((1, tm, D_MODEL), lambda i, j: (i, j, 0))],
        out_specs=pl.BlockSpec((1, tm, D_MODEL), lambda i, j: (i, j, 0)),
        out_shape=jax.ShapeDtypeStruct(x.shape, F32),
        scratch_shapes=[pltpu.VMEM((tm, MLA_HEADS * MLA_V), BF16)],
        compiler_params=_params(("parallel", "parallel"), 32),
        name="mla_out",
    )(o_lat, wuv, w_out, x)


def _pool_prompt_kernel(x_ref, xp_ref, g_ref, w_ref, sc_ref, o_ref, hl_ref, ext_ref, tmp_ref):
    j = pl.program_id(1)
    x = x_ref[0]
    tm = x.shape[0]
    h = _rms(x, g_ref[...])
    hp = _rms(xp_ref[0], g_ref[...])
    ext_ref[0:POOL_PREV, :] = jnp.where(j == 0, 0.0, hp)
    ext_ref[POOL_PREV:, :] = h
    tmp_ref[0:POOL_ALIGN, :] = jnp.zeros((POOL_ALIGN, D_MODEL), F32)
    n = POOL_PREV + tm - POOL_ALIGN
    pos = (j * tm + lax.broadcasted_iota(jnp.int32, (tm, 1), 0)).astype(F32)
    outs = []
    for gi, w in enumerate(POOL_WINDOWS):
        lanes = slice(gi * POOL_GROUP_DIM, (gi + 1) * POOL_GROUP_DIM)
        src, dst = ext_ref, tmp_ref
        shift = 1
        while shift < w:
            dst[POOL_ALIGN:, lanes] = (src[POOL_ALIGN:, lanes]
                                       + src[POOL_ALIGN - shift:POOL_ALIGN - shift + n, lanes])
            src, dst = dst, src
            shift *= 2
        pooled = src[POOL_PREV:, lanes] / jnp.minimum(pos + 1.0, float(w)) - h[:, lanes]
        outs.append(_dot(pooled.astype(BF16), w_ref[gi]))
    o_ref[0] = x + jnp.concatenate(outs, axis=-1) * sc_ref[...]

    @pl.when(j == pl.num_programs(1) - 1)
    def _():
        hl_ref[0] = h[tm - POOL_TAIL:]


def _pool_prompt(x, gain, w_pool, scale):
    b, t, _ = x.shape
    tm = 512
    assert t % tm == 0 and tm % POOL_PREV == 0
    per_tile = tm // POOL_PREV
    return pl.pallas_call(
        _pool_prompt_kernel,
        grid=(b, t // tm),
        in_specs=[pl.BlockSpec((1, tm, D_MODEL), lambda i, j: (i, j, 0)),
                  pl.BlockSpec((1, POOL_PREV, D_MODEL),
                               lambda i, j: (i, jnp.maximum(j * per_tile - 1, 0), 0)),
                  _resident(gain.shape), _resident(w_pool.shape), _resident(scale.shape)],
        out_specs=[pl.BlockSpec((1, tm, D_MODEL), lambda i, j: (i, j, 0)),
                   pl.BlockSpec((1, POOL_TAIL, D_MODEL), lambda i, j: (i, 0, 0))],
        out_shape=[jax.ShapeDtypeStruct(x.shape, F32),
                   jax.ShapeDtypeStruct((b, POOL_TAIL, D_MODEL), F32)],
        scratch_shapes=[pltpu.VMEM((POOL_PREV + tm, D_MODEL), F32),
                        pltpu.VMEM((POOL_PREV + tm, D_MODEL), F32)],
        compiler_params=_params(("parallel", "arbitrary"), 32),
        name="pool_prompt",
    )(x, x, gain, w_pool, scale)


def _pool_sample_kernel(x_ref, pre_ref, g_ref, w_ref, sc_ref, o_ref, h_ref, *, pos0):
    t_len = x_ref.shape[0]
    hs = [_rms(x_ref[t], g_ref[...]) for t in range(t_len)]
    for t in range(t_len):
        h_ref[t] = hs[t]

    def ext(e, lanes):
        return pre_ref[e, :, lanes] if e < POOL_BUF else hs[e - POOL_BUF][:, lanes]

    for t in range(t_len):
        outs = []
        for gi, w in enumerate(POOL_WINDOWS):
            lanes = slice(gi * POOL_GROUP_DIM, (gi + 1) * POOL_GROUP_DIM)
            acc = hs[t][:, lanes]
            for k in range(1, w):
                acc = acc + ext(POOL_BUF + t - k, lanes)
            pooled = acc / min(pos0 + t + 1.0, float(w)) - hs[t][:, lanes]
            outs.append(_dot(pooled.astype(BF16), w_ref[gi]))
        o_ref[t] = x_ref[t] + jnp.concatenate(outs, axis=-1) * sc_ref[...]


def _pool_sample(x_t, prefix_t, gain, w_pool, scale, pos0):
    vmem = pl.BlockSpec(memory_space=pltpu.VMEM)
    return pl.pallas_call(
        functools.partial(_pool_sample_kernel, pos0=pos0),
        in_specs=[vmem] * 5,
        out_specs=[vmem, vmem],
        out_shape=[jax.ShapeDtypeStruct(x_t.shape, F32), jax.ShapeDtypeStruct(x_t.shape, F32)],
        compiler_params=_params(None, 32),
        name="pool_sample",
    )(x_t, prefix_t, gain, w_pool, scale)


def _rope_tables(pos, half):
    inv = ROPE_BASE ** (-jnp.arange(half, dtype=F32) / half)
    ang = pos.astype(F32)[:, None] * inv[None, :]
    return jnp.cos(ang), jnp.sin(ang)


def _mla_rope_tables(pos):
    cos, sin = _rope_tables(pos, MLA_ROPE // 2)
    zeros = jnp.zeros((pos.shape[0], 128 - MLA_ROPE), F32)
    return (jnp.concatenate([cos, cos, zeros], axis=-1),
            jnp.concatenate([-sin, sin, zeros], axis=-1))


def _swap_halves(w):
    half = w.shape[-1] // 2
    return jnp.concatenate([w[..., half:], w[..., :half]], axis=-1)


def _pad_lanes(w, width):
    return jnp.pad(w, [(0, 0)] * (w.ndim - 1) + [(0, width - w.shape[-1])])


def _row(v):
    return v.reshape(1, -1).astype(F32)


def kernel(x_prompt, x_sample, state_ret, cache_mla_latent, cache_mla_krope, page_table, state_pool,
           norm_mix, norm_ffn, norm_final,
           ret_w_in, ret_gn_gain, ret_w_out,
           cm_w_in, cm_ln_gain, cm_w_spatial, cm_b_spatial, cm_w_out,
           mla_w_down, mla_q_norm, mla_kv_norm, mla_w_uq, mla_w_uk, mla_w_uv, mla_w_out,
           pool_w, pool_scale,
           ffn_w_gate_up, ffn_w_down):
    bp, tp, _ = x_prompt.shape
    bs, ts, _ = x_sample.shape
    xp = x_prompt
    xs = x_sample.reshape(bs * ts, D_MODEL)
    pos_p = jnp.arange(tp)
    pos_s = PAST_LEN + jnp.arange(ts)

    def ffn(xp_, xs_, i, final):
        yp, ys = _ffn(xp_.reshape(-1, D_MODEL), xs_, _row(norm_ffn[i]), ffn_w_gate_up, ffn_w_down, i,
                      _row(norm_final), final)
        return yp.reshape(xp_.shape), ys

    log_g = jnp.log1p(-jnp.exp2(-5.0 - jnp.arange(RET_HEADS, dtype=F32)))
    w_in = ret_w_in[0].astype(BF16)
    w_out = ret_w_out[0].astype(BF16)
    gn_gain = _row(ret_gn_gain[0])
    cos_p, sin_p = _rope_tables(pos_p, RET_QK_DIM // 2)
    cos_s, sin_s = _rope_tables(pos_s, RET_QK_DIM // 2)
    xp, ret_state_p = _ret_prompt(xp, _row(norm_mix[0]), w_in, cos_p, sin_p,
                                  _ret_tables(float(RET_PROMPT_CHUNK), RET_PROMPT_CHUNK, log_g),
                                  gn_gain, w_out)
    proj_s = _norm_matmul(xs, _row(norm_mix[0]), w_in, 1024)
    y_s, ret_state_s = _ret_sample(proj_s, state_ret[0], cos_s, sin_s,
                                   _ret_tables(float(ts), RET_CHUNK, log_g), gn_gain)
    xs = _matmul_residual(y_s, w_out, xs)
    xp, xs = ffn(xp, xs, 0, False)

    cw_in = cm_w_in[0].astype(BF16)
    cw_out = cm_w_out[0].astype(BF16)
    bias_full = jnp.repeat(jnp.transpose(cm_b_spatial[0]), CM_GROUP_DIM, axis=1)
    xp2, cm_v_p = _cm_prompt(xp.reshape(bp * tp, D_MODEL), _row(norm_mix[1]), cw_in,
                             _row(cm_ln_gain[0]), cm_w_spatial[0], bias_full, cw_out, tp)
    xp = xp2.reshape(bp, tp, D_MODEL)
    xs_t, cm_v_s_t = _cm_sample(xs.reshape(bs, ts, D_MODEL).transpose(1, 0, 2), _row(norm_mix[1]), cw_in,
                                _row(cm_ln_gain[0]),
                                cm_w_spatial[0][:, :ts, :ts].reshape(CM_GROUPS, ts * ts),
                                cm_b_spatial[0][:, :ts], cw_out)
    xs = xs_t.transpose(1, 0, 2).reshape(bs * ts, D_MODEL)
    cm_v_s = cm_v_s_t.transpose(1, 0, 2)
    xp, xs = ffn(xp, xs, 1, False)

    wd = mla_w_down[0]
    kr_w = wd[:, MLA_Q_LORA + MLA_KV_LORA:]
    wd_ext = jnp.concatenate([wd[:, :MLA_Q_LORA + MLA_KV_LORA], _pad_lanes(kr_w, 128),
                              _pad_lanes(_swap_halves(kr_w), 128)], axis=-1).astype(BF16)
    wuq = mla_w_uq[0].reshape(MLA_Q_LORA, MLA_HEADS, MLA_NOPE + MLA_ROPE)
    wuq_rope = wuq[:, :, MLA_NOPE:]
    wuq_ext = jnp.concatenate([
        wuq[:, :, :MLA_NOPE].reshape(MLA_Q_LORA, -1),
        _pad_lanes(wuq_rope, 128).reshape(MLA_Q_LORA, -1),
        _pad_lanes(_swap_halves(wuq_rope), 128).reshape(MLA_Q_LORA, -1)], axis=-1).astype(BF16)
    wuk_t = mla_w_uk[0].transpose(1, 2, 0).astype(BF16)
    wuv = mla_w_uv[0].transpose(1, 0, 2).astype(BF16)
    mw_out = mla_w_out[0].astype(BF16)
    cos2_p, sin2_p = _mla_rope_tables(pos_p)
    cos2_s, sin2_s = _mla_rope_tables(jnp.tile(pos_s, bs))
    gq, gkv = _row(mla_q_norm[0]), _row(mla_kv_norm[0])
    q_p, lat_p, kr_p, kext_p = _mla_q(xp, _row(norm_mix[2]), wd_ext, gq, gkv, wuq_ext, wuk_t,
                                      cos2_p, sin2_p, 512)
    tk = 2 * MLA_ATTN_TQ
    vt_p = kext_p[:, :, :MLA_KV_LORA].reshape(bp, tp // tk, tk, MLA_KV_LORA).swapaxes(2, 3)
    o_p = _mla_attn_prompt(q_p, kext_p, vt_p)
    xp = _mla_out(o_p, wuv, mw_out, xp, 512)
    ns = bs * ts
    q_s, lat_s, kr_s, kext_s = _mla_q(xs.reshape(1, ns, D_MODEL), _row(norm_mix[2]), wd_ext, gq, gkv,
                                      wuq_ext, wuk_t, cos2_s, sin2_s, ns)
    q_s = q_s.reshape(MLA_HEADS, bs, ts, MLA_KEY_DIM).transpose(1, 0, 2, 3).reshape(bs, MLA_HEADS * ts, MLA_KEY_DIM)
    knew = jnp.pad(kext_s.reshape(bs, ts, MLA_KEY_DIM), ((0, 0), (0, PAGE_SIZE - ts), (0, 0)))
    o_s = _mla_attn_sample(page_table, q_s, cache_mla_latent[0],
                           jnp.swapaxes(cache_mla_krope[0], 1, 2), knew, ts)
    o_s = o_s.reshape(bs, MLA_HEADS, ts, MLA_KV_LORA).transpose(1, 0, 2, 3).reshape(1, MLA_HEADS, ns, MLA_KV_LORA)
    xs = _mla_out(o_s.astype(BF16), wuv, mw_out, xs.reshape(1, ns, D_MODEL), ns).reshape(ns, D_MODEL)
    lat_s = lat_s.reshape(bs, ts, MLA_KV_LORA)
    kr_s = kr_s.reshape(bs, ts, MLA_ROPE)
    xp, xs = ffn(xp, xs, 2, False)

    pw = pool_w[0].astype(BF16)
    xp, h_last = _pool_prompt(xp, _row(norm_mix[3]), pw, _row(pool_scale[0]))
    pool_state_p = h_last[:, POOL_TAIL - POOL_BUF:]
    xs_t, h_s_t = _pool_sample(xs.reshape(bs, ts, D_MODEL).transpose(1, 0, 2),
                               state_pool[0].transpose(1, 0, 2), _row(norm_mix[3]), pw,
                               _row(pool_scale[0]), float(PAST_LEN))
    xs = xs_t.transpose(1, 0, 2).reshape(bs * ts, D_MODEL)
    pool_state_s = jnp.concatenate([state_pool[0], h_s_t.transpose(1, 0, 2)], axis=1)[:, -POOL_BUF:]
    xp, xs = ffn(xp, xs, 3, True)

    return (xp, xs.reshape(bs, ts, D_MODEL),
            ret_state_p[None], ret_state_s[None],
            cm_v_p.reshape(bp, CM_CHUNK, D_MODEL)[None], cm_v_s[None],
            lat_p[None], kr_p[None], lat_s[None], kr_s[None],
            pool_state_p[None], pool_state_s[None])
```

```python
import functools

import jax
import jax.numpy as jnp
from jax import lax
from jax.experimental import pallas as pl
from jax.experimental.pallas import tpu as pltpu

F32 = jnp.float32
BF16 = jnp.bfloat16

D_MODEL = 1024
NORM_EPS = 1e-6
ROPE_BASE = 10000.0
PAST_LEN = 8192
PAGE_SIZE = 128

RET_HEADS = 4
RET_QK_DIM = 256
RET_V_DIM = 512
RET_CHUNK = 128
RET_PROMPT_CHUNK = 256
RET_SAMPLE_ROWS_PER_STEP = 4

CM_CHUNK = 128
CM_GROUPS = 4
CM_GROUP_DIM = 256

MLA_HEADS = 8
MLA_Q_LORA = 384
MLA_KV_LORA = 256
MLA_NOPE = 128
MLA_ROPE = 64
MLA_V = 128
MLA_SCALE = (MLA_NOPE + MLA_ROPE) ** -0.5
MLA_KEY_DIM = MLA_KV_LORA + 128
MLA_KEY_CHUNK = 1024
MLA_SAMPLE_ROWS_PER_STEP = 2
MLA_ATTN_TQ = 256

POOL_WINDOWS = (2, 4, 8, 16)
POOL_GROUP_DIM = 256
POOL_BUF = 15
POOL_PREV = 32
POOL_ALIGN = 8
POOL_TAIL = 16

FFN_HIDDEN = 2816
FFN_CHUNK = 256

V7X_VMEM_BYTES = 64 * 1024 * 1024
MIB = 1024 * 1024


def _params(semantics, vmem_mib):
    assert vmem_mib * MIB < V7X_VMEM_BYTES
    return pltpu.CompilerParams(dimension_semantics=semantics,
                                vmem_limit_bytes=vmem_mib * MIB)


def _resident(shape):
    nd = len(shape)
    return pl.BlockSpec(shape, lambda *_: (0,) * nd, pipeline_mode=pl.Buffered(1))


def _rms(x, g):
    return x * lax.rsqrt(jnp.mean(x * x, axis=-1, keepdims=True) + NORM_EPS) * g


def _dot(a, b):
    return jnp.dot(a, b, preferred_element_type=F32)


def _wdot(a, w):
    return jnp.dot(a, w.astype(BF16), preferred_element_type=F32)


def _dot_nt(a, b):
    return lax.dot_general(a, b, (((1,), (1,)), ((), ())), preferred_element_type=F32)


def _ffn_kernel(xp_ref, xs_ref, g_ref, wgu_ref, wd_ref, gf_ref, op_ref, os_ref, *, final):
    def tile(x_ref, o_ref):
        x = x_ref[...]
        h = _rms(x, g_ref[...]).astype(BF16)
        acc = x
        for j in range(FFN_HIDDEN // FFN_CHUNK):
            cols = slice(j * FFN_CHUNK, (j + 1) * FFN_CHUNK)
            ucols = slice(FFN_HIDDEN + j * FFN_CHUNK, FFN_HIDDEN + (j + 1) * FFN_CHUNK)
            g = _dot(h, wgu_ref[:, cols].astype(BF16))
            u = _dot(h, wgu_ref[:, ucols].astype(BF16))
            a = (g * jax.nn.sigmoid(g)) * u
            acc = acc + _dot(a.astype(BF16), wd_ref[cols, :].astype(BF16))
        o_ref[...] = _rms(acc, gf_ref[...]) if final else acc

    is_prompt = pl.program_id(0) < pl.num_programs(0) - 1

    @pl.when(is_prompt)
    def _():
        tile(xp_ref, op_ref)

    @pl.when(jnp.logical_not(is_prompt))
    def _():
        tile(xs_ref, os_ref)


def _ffn(xp, xs, gain, wgu_all, wd_all, layer, final_gain, final):
    tm = xs.shape[0]
    assert xp.shape[0] % tm == 0
    n_prompt = xp.shape[0] // tm

    def layer_slab(w):
        return pl.BlockSpec((None,) + w.shape[1:], lambda i: (layer, 0, 0),
                            pipeline_mode=pl.Buffered(1))

    prompt_tile = pl.BlockSpec((tm, D_MODEL), lambda i: (jnp.minimum(i, n_prompt - 1), 0))
    sample_tile = pl.BlockSpec((tm, D_MODEL), lambda i: (0, 0))
    return pl.pallas_call(
        functools.partial(_ffn_kernel, final=final),
        grid=(n_prompt + 1,),
        in_specs=[prompt_tile, sample_tile,
                  _resident(gain.shape), layer_slab(wgu_all),
                  layer_slab(wd_all), _resident(final_gain.shape)],
        out_specs=[prompt_tile, sample_tile],
        out_shape=[jax.ShapeDtypeStruct(xp.shape, F32), jax.ShapeDtypeStruct(xs.shape, F32)],
        compiler_params=_params(("arbitrary",), 58),
        name="ffn",
    )(xp, xs, gain, wgu_all, wd_all, final_gain)


def _norm_matmul_kernel(x_ref, g_ref, w_ref, o_ref):
    h = _rms(x_ref[...], g_ref[...]).astype(BF16)
    o_ref[...] = _wdot(h, w_ref[...])


def _norm_matmul(x, gain, w, tn):
    n, dout = x.shape[0], w.shape[1]
    assert dout % tn == 0
    return pl.pallas_call(
        _norm_matmul_kernel,
        grid=(dout // tn,),
        in_specs=[pl.BlockSpec((n, D_MODEL), lambda j: (0, 0)),
                  pl.BlockSpec(gain.shape, lambda j: (0, 0)),
                  pl.BlockSpec((D_MODEL, tn), lambda j: (0, j))],
        out_specs=pl.BlockSpec((n, tn), lambda j: (0, j)),
        out_shape=jax.ShapeDtypeStruct((n, dout), F32),
        compiler_params=_params(("parallel",), 32),
        name="norm_matmul",
    )(x, gain, w)


def _matmul_residual_kernel(y_ref, w_ref, x_ref, o_ref):
    o_ref[...] = x_ref[...] + _wdot(y_ref[...].astype(BF16), w_ref[...])


def _matmul_residual(y, w, x):
    n = x.shape[0]
    tm = min(n, 512)
    assert n % tm == 0
    return pl.pallas_call(
        _matmul_residual_kernel,
        grid=(n // tm,),
        in_specs=[pl.BlockSpec((tm, y.shape[1]), lambda i: (i, 0)),
                  _resident(w.shape),
                  pl.BlockSpec((tm, D_MODEL), lambda i: (i, 0))],
        out_specs=pl.BlockSpec((tm, D_MODEL), lambda i: (i, 0)),
        out_shape=jax.ShapeDtypeStruct(x.shape, F32),
        compiler_params=_params(("parallel",), 32),
        name="matmul_residual",
    )(y, w, x)


def _rope_halves(x, cos, sin):
    half = cos.shape[-1]
    x1, x2 = x[:, :half], x[:, half:]
    return jnp.concatenate([x1 * cos - x2 * sin, x1 * sin + x2 * cos], axis=-1)


def _ret_chunk(q, k, v, s, dmat, qdec, kdec, sdec):
    vb = v.astype(BF16)
    scores = _dot_nt(q.astype(BF16), k.astype(BF16)) * dmat
    o = _dot(scores.astype(BF16), vb) + _dot((q * qdec).astype(BF16), s.astype(BF16))
    s_new = sdec * s + _dot((k * kdec).T.astype(BF16), vb)
    return o, s_new


def _group_norm_gate(o, gate, gain):
    mu = jnp.mean(o, axis=-1, keepdims=True)
    oc = o - mu
    var = jnp.mean(oc * oc, axis=-1, keepdims=True)
    on = oc * lax.rsqrt(var + NORM_EPS)
    return (gate * jax.nn.sigmoid(gate)) * on * gain


def _ret_prompt_kernel(x_ref, g_ref, win_ref, cos_ref, sin_ref, dmat_ref, qdec_ref,
                       kdec_ref, sdec_ref, gng_ref, wout_ref, o_ref, st_ref,
                       s_ref, y_ref):
    j = pl.program_id(1)

    @pl.when(j == 0)
    def _():
        s_ref[...] = jnp.zeros_like(s_ref)

    x = x_ref[0]
    tm = x.shape[0]
    h = _rms(x, g_ref[...]).astype(BF16)
    for hh in range(RET_HEADS):
        qs = slice(hh * RET_QK_DIM, (hh + 1) * RET_QK_DIM)
        ks = slice(D_MODEL + hh * RET_QK_DIM, D_MODEL + (hh + 1) * RET_QK_DIM)
        vs = slice(2 * D_MODEL + hh * RET_V_DIM, 2 * D_MODEL + (hh + 1) * RET_V_DIM)
        gs = slice(4 * D_MODEL + hh * RET_V_DIM, 4 * D_MODEL + (hh + 1) * RET_V_DIM)
        q = _rope_halves(_wdot(h, win_ref[:, qs]), cos_ref[...], sin_ref[...]) * (RET_QK_DIM ** -0.5)
        k = _rope_halves(_wdot(h, win_ref[:, ks]), cos_ref[...], sin_ref[...])
        v = _wdot(h, win_ref[:, vs])
        gate = _wdot(h, win_ref[:, gs])
        gain = gng_ref[:, hh * RET_V_DIM:(hh + 1) * RET_V_DIM]
        chunk = dmat_ref.shape[1]
        for c in range(tm // chunk):
            rows = slice(c * chunk, (c + 1) * chunk)
            o, s_new = _ret_chunk(q[rows], k[rows], v[rows], s_ref[hh], dmat_ref[hh],
                                  qdec_ref[hh], kdec_ref[hh], sdec_ref[hh])
            s_ref[hh] = s_new
            y_ref[rows, hh * RET_V_DIM:(hh + 1) * RET_V_DIM] = (
                _group_norm_gate(o, gate[rows], gain).astype(BF16))
    o_ref[0] = x + _wdot(y_ref[...], wout_ref[...])

    @pl.when(j == pl.num_programs(1) - 1)
    def _():
        st_ref[0] = s_ref[...]


def _ret_tables(length, chunk, log_g):
    idx = jnp.arange(chunk, dtype=F32)
    valid = idx < length
    rel = idx[:, None] - idx[None, :]
    ok = (rel >= 0) & valid[:, None] & valid[None, :]
    dmat = jnp.where(ok[None], jnp.exp(jnp.maximum(rel, 0.0)[None] * log_g[:, None, None]), 0.0)
    qd = jnp.where(valid[None], jnp.exp((idx + 1.0)[None, :] * log_g[:, None]), 0.0)
    kd = jnp.where(valid[None], jnp.exp((length - 1.0 - idx)[None, :] * log_g[:, None]), 0.0)
    qdec = jnp.broadcast_to(qd[:, :, None], (RET_HEADS, chunk, RET_QK_DIM))
    kdec = jnp.broadcast_to(kd[:, :, None], (RET_HEADS, chunk, RET_QK_DIM))
    sdec = jnp.broadcast_to(jnp.exp(length * log_g)[:, None, None], (RET_HEADS, 1, RET_V_DIM))
    return dmat, qdec, kdec, sdec


def _ret_prompt(x, gain, w_in, cos, sin, tables, gn_gain, w_out):
    b, t, _ = x.shape
    tm = 256
    assert t % tm == 0
    dmat, qdec, kdec, sdec = tables
    return pl.pallas_call(
        _ret_prompt_kernel,
        grid=(b, t // tm),
        in_specs=[pl.BlockSpec((1, tm, D_MODEL), lambda i, j: (i, j, 0)),
                  _resident(gain.shape), _resident(w_in.shape),
                  pl.BlockSpec((tm, RET_QK_DIM // 2), lambda i, j: (j, 0)),
                  pl.BlockSpec((tm, RET_QK_DIM // 2), lambda i, j: (j, 0)),
                  _resident(dmat.shape), _resident(qdec.shape), _resident(kdec.shape),
                  _resident(sdec.shape), _resident(gn_gain.shape), _resident(w_out.shape)],
        out_specs=[pl.BlockSpec((1, tm, D_MODEL), lambda i, j: (i, j, 0)),
                   pl.BlockSpec((1, RET_HEADS, RET_QK_DIM, RET_V_DIM), lambda i, j: (i, 0, 0, 0))],
        out_shape=[jax.ShapeDtypeStruct(x.shape, F32),
                   jax.ShapeDtypeStruct((b, RET_HEADS, RET_QK_DIM, RET_V_DIM), F32)],
        scratch_shapes=[pltpu.VMEM((RET_HEADS, RET_QK_DIM, RET_V_DIM), F32),
                        pltpu.VMEM((tm, RET_HEADS * RET_V_DIM), BF16)],
        compiler_params=_params(("parallel", "arbitrary"), 56),
        name="ret_prompt",
    )(x, gain, w_in, cos, sin, dmat, qdec, kdec, sdec, gn_gain, w_out)


def _ret_sample_kernel(p_ref, s0_ref, cos_ref, sin_ref, dmat_ref, qdec_ref, kdec_ref,
                       sdec_ref, gng_ref, y_ref, st_ref, qp_ref, kp_ref, vp_ref):
    @pl.when(pl.program_id(0) == 0)
    def _():
        qp_ref[...] = jnp.zeros_like(qp_ref)
        kp_ref[...] = jnp.zeros_like(kp_ref)
        vp_ref[...] = jnp.zeros_like(vp_ref)

    t = cos_ref.shape[0]
    for bb in range(s0_ref.shape[0]):
        rows = slice(bb * t, (bb + 1) * t)
        for hh in range(RET_HEADS):
            qs = slice(hh * RET_QK_DIM, (hh + 1) * RET_QK_DIM)
            ks = slice(D_MODEL + hh * RET_QK_DIM, D_MODEL + (hh + 1) * RET_QK_DIM)
            vs = slice(2 * D_MODEL + hh * RET_V_DIM, 2 * D_MODEL + (hh + 1) * RET_V_DIM)
            gs = slice(4 * D_MODEL + hh * RET_V_DIM, 4 * D_MODEL + (hh + 1) * RET_V_DIM)
            qp_ref[0:t, :] = (_rope_halves(p_ref[rows, qs], cos_ref[...], sin_ref[...])
                              * (RET_QK_DIM ** -0.5))
            kp_ref[0:t, :] = _rope_halves(p_ref[rows, ks], cos_ref[...], sin_ref[...])
            vp_ref[0:t, :] = p_ref[rows, vs]
            o, s_new = _ret_chunk(qp_ref[...], kp_ref[...], vp_ref[...], s0_ref[bb, hh], dmat_ref[hh],
                                  qdec_ref[hh], kdec_ref[hh], sdec_ref[hh])
            st_ref[bb, hh] = s_new
            gain = gng_ref[:, hh * RET_V_DIM:(hh + 1) * RET_V_DIM]
            y_ref[rows, hh * RET_V_DIM:(hh + 1) * RET_V_DIM] = _group_norm_gate(o[0:t], p_ref[rows, gs], gain)


def _ret_sample(proj, s0, cos, sin, tables, gn_gain):
    b = s0.shape[0]
    t = cos.shape[0]
    nb = RET_SAMPLE_ROWS_PER_STEP
    assert b % nb == 0 and proj.shape[0] == b * t
    dmat, qdec, kdec, sdec = tables
    state_spec = pl.BlockSpec((nb, RET_HEADS, RET_QK_DIM, RET_V_DIM), lambda i: (i, 0, 0, 0))
    return pl.pallas_call(
        _ret_sample_kernel,
        grid=(b // nb,),
        in_specs=[pl.BlockSpec((nb * t, proj.shape[1]), lambda i: (i, 0)),
                  state_spec,
                  _resident(cos.shape), _resident(sin.shape),
                  _resident(dmat.shape), _resident(qdec.shape), _resident(kdec.shape),
                  _resident(sdec.shape), _resident(gn_gain.shape)],
        out_specs=[pl.BlockSpec((nb * t, RET_HEADS * RET_V_DIM), lambda i: (i, 0)),
                   state_spec],
        out_shape=[jax.ShapeDtypeStruct((b * t, RET_HEADS * RET_V_DIM), F32),
                   jax.ShapeDtypeStruct(s0.shape, F32)],
        scratch_shapes=[pltpu.VMEM((RET_CHUNK, RET_QK_DIM), F32),
                        pltpu.VMEM((RET_CHUNK, RET_QK_DIM), F32),
                        pltpu.VMEM((RET_CHUNK, RET_V_DIM), F32)],
        compiler_params=_params(("arbitrary",), 48),
        name="ret_sample",
    )(proj, s0, cos, sin, dmat, qdec, kdec, sdec, gn_gain)


def _gelu_tanh(x):
    return x * (0.5 * (1.0 + jnp.tanh(0.7978845608028654 * (x + 0.044715 * (x * x * x)))))


def _layer_norm(v, gain):
    mu = jnp.mean(v, axis=-1, keepdims=True)
    vc = v - mu
    var = jnp.mean(vc * vc, axis=-1, keepdims=True)
    return vc * lax.rsqrt(var + NORM_EPS) * gain


def _cm_prompt_kernel(x_ref, g_ref, win_ref, lng_ref, ws_ref, bias_ref, wout_ref,
                      o_ref, v_ref, z_ref):
    x = x_ref[...]
    tm = x.shape[0]
    h = _rms(x, g_ref[...]).astype(BF16)
    u = _gelu_tanh(_wdot(h, win_ref[:, :D_MODEL]))
    v = _layer_norm(_gelu_tanh(_wdot(h, win_ref[:, D_MODEL:])), lng_ref[...])
    v_ref[...] = v[tm - CM_CHUNK:]
    vb = v.astype(BF16)
    row = lax.broadcasted_iota(jnp.int32, (CM_CHUNK, CM_CHUNK), 0)
    col = lax.broadcasted_iota(jnp.int32, (CM_CHUNK, CM_CHUNK), 1)
    for gi in range(CM_GROUPS):
        lanes = slice(gi * CM_GROUP_DIM, (gi + 1) * CM_GROUP_DIM)
        w = jnp.where(row >= col, ws_ref[gi], 0.0).astype(BF16)
        for c in range(tm // CM_CHUNK):
            rows = slice(c * CM_CHUNK, (c + 1) * CM_CHUNK)
            mixed = _dot(w, vb[rows, lanes]) + bias_ref[:, lanes]
            z_ref[rows, lanes] = (u[rows, lanes] * mixed).astype(BF16)
    o_ref[...] = x + _wdot(z_ref[...], wout_ref[...])


def _cm_prompt(x, gain, w_in, ln_gain, w_s, bias_full, w_out, seq):
    n = x.shape[0]
    tm = 512
    assert seq % tm == 0 and n % seq == 0
    per_seq = seq // tm
    return pl.pallas_call(
        _cm_prompt_kernel,
        grid=(n // tm,),
        in_specs=[pl.BlockSpec((tm, D_MODEL), lambda i: (i, 0)),
                  _resident(gain.shape), _resident(w_in.shape), _resident(ln_gain.shape),
                  _resident(w_s.shape), _resident(bias_full.shape), _resident(w_out.shape)],
        out_specs=[pl.BlockSpec((tm, D_MODEL), lambda i: (i, 0)),
                   pl.BlockSpec((CM_CHUNK, D_MODEL), lambda i: (i // per_seq, 0))],
        out_shape=[jax.ShapeDtypeStruct(x.shape, F32),
                   jax.ShapeDtypeStruct((n // seq * CM_CHUNK, D_MODEL), F32)],
        scratch_shapes=[pltpu.VMEM((tm, D_MODEL), BF16)],
        compiler_params=_params(("arbitrary",), 40),
        name="cm_prompt",
    )(x, gain, w_in, ln_gain, w_s, bias_full, w_out)


def _cm_sample_kernel(ws_ref, bs_ref, x_ref, g_ref, win_ref, lng_ref, wout_ref, o_ref, v_ref):
    t_len = x_ref.shape[0]
    us, vs = [], []
    for t in range(t_len):
        h = _rms(x_ref[t], g_ref[...]).astype(BF16)
        us.append(_gelu_tanh(_wdot(h, win_ref[:, :D_MODEL])))
        v = _layer_norm(_gelu_tanh(_wdot(h, win_ref[:, D_MODEL:])), lng_ref[...])
        v_ref[t] = v
        vs.append(v)
    for t in range(t_len):
        parts = []
        for gi in range(CM_GROUPS):
            lanes = slice(gi * CM_GROUP_DIM, (gi + 1) * CM_GROUP_DIM)
            mixed = jnp.full_like(vs[t][:, lanes], bs_ref[gi, t])
            for s in range(t + 1):
                mixed = mixed + ws_ref[gi, t * t_len + s] * vs[s][:, lanes]
            parts.append(us[t][:, lanes] * mixed)
        z = jnp.concatenate(parts, axis=-1).astype(BF16)
        o_ref[t] = x_ref[t] + _wdot(z, wout_ref[...])


def _cm_sample(x_t, gain, w_in, ln_gain, w_s_small, b_s_small, w_out):
    smem = pl.BlockSpec(memory_space=pltpu.SMEM)
    vmem = pl.BlockSpec(memory_space=pltpu.VMEM)
    return pl.pallas_call(
        _cm_sample_kernel,
        in_specs=[smem, smem, vmem, vmem, vmem, vmem, vmem],
        out_specs=[vmem, vmem],
        out_shape=[jax.ShapeDtypeStruct(x_t.shape, F32), jax.ShapeDtypeStruct(x_t.shape, F32)],
        compiler_params=_params(None, 32),
        name="cm_sample",
    )(w_s_small, b_s_small, x_t, gain, w_in, ln_gain, w_out)


def _mla_q_kernel(x_ref, g_ref, wd_ref, gq_ref, gkv_ref, wuq_ref, wuk_ref, cos_ref, sin_ref,
                  q_ref, c_ref, kr_ref, kext_ref):
    h = _rms(x_ref[0], g_ref[...]).astype(BF16)
    d = _dot(h, wd_ref[...])
    cq = _rms(d[:, :MLA_Q_LORA], gq_ref[...]).astype(BF16)
    ckv = _rms(d[:, MLA_Q_LORA:MLA_Q_LORA + MLA_KV_LORA], gkv_ref[...])
    cos, sin = cos_ref[...], sin_ref[...]
    base = MLA_Q_LORA + MLA_KV_LORA
    krp = d[:, base:base + 128] * cos + d[:, base + 128:base + 256] * sin
    c_ref[0] = ckv
    kr_ref[0] = krp[:, :MLA_ROPE]
    kext_ref[0] = jnp.concatenate([ckv.astype(BF16), krp.astype(BF16)], axis=-1)
    nope_w = MLA_HEADS * MLA_NOPE
    for hh in range(MLA_HEADS):
        qn = _dot(cq, wuq_ref[:, hh * MLA_NOPE:(hh + 1) * MLA_NOPE])
        raw = _dot(cq, wuq_ref[:, nope_w + hh * 128:nope_w + (hh + 1) * 128])
        rot = _dot(cq, wuq_ref[:, 2 * nope_w + hh * 128:2 * nope_w + (hh + 1) * 128])
        ql = _dot(qn.astype(BF16), wuk_ref[hh]) * MLA_SCALE
        qr = (raw * cos + rot * sin) * MLA_SCALE
        q_ref[0, hh] = jnp.concatenate([ql.astype(BF16), qr.astype(BF16)], axis=-1)


def _mla_q(x, gain, wd_ext, g_q, g_kv, wuq_ext, wuk_t, cos2, sin2, tm):
    nb, t, _ = x.shape
    assert t % tm == 0
    return pl.pallas_call(
        _mla_q_kernel,
        grid=(nb, t // tm),
        in_specs=[pl.BlockSpec((1, tm, D_MODEL), lambda i, j: (i, j, 0)),
                  _resident(gain.shape), _resident(wd_ext.shape), _resident(g_q.shape),
                  _resident(g_kv.shape), _resident(wuq_ext.shape), _resident(wuk_t.shape),
                  pl.BlockSpec((tm, 128), lambda i, j: (j, 0)),
                  pl.BlockSpec((tm, 128), lambda i, j: (j, 0))],
        out_specs=[pl.BlockSpec((1, MLA_HEADS, tm, MLA_KEY_DIM), lambda i, j: (i, 0, j, 0)),
                   pl.BlockSpec((1, tm, MLA_KV_LORA), lambda i, j: (i, j, 0)),
                   pl.BlockSpec((1, tm, MLA_ROPE), lambda i, j: (i, j, 0)),
                   pl.BlockSpec((1, tm, MLA_KEY_DIM), lambda i, j: (i, j, 0))],
        out_shape=[jax.ShapeDtypeStruct((nb, MLA_HEADS, t, MLA_KEY_DIM), BF16),
                   jax.ShapeDtypeStruct((nb, t, MLA_KV_LORA), F32),
                   jax.ShapeDtypeStruct((nb, t, MLA_ROPE), F32),
                   jax.ShapeDtypeStruct((nb, t, MLA_KEY_DIM), BF16)],
        compiler_params=_params(("parallel", "parallel"), 40),
        name="mla_q",
    )(x, gain, wd_ext, g_q, g_kv, wuq_ext, wuk_t, cos2, sin2)


def _mla_attn_prompt_kernel(q_ref, k_ref, vt_ref, o_ref, m_ref, l_ref, acc_ref, sa_ref, sb_ref,
                            *, tq, n_tiles):
    tk = 2 * tq
    cols = MLA_HEADS * tq
    score_refs = (sa_ref, sb_ref)

    def scores(step, dst_ref):
        kb, n_keys, first_masked_key = step
        q = q_ref[0].reshape(cols, MLA_KEY_DIM)
        st = _dot_nt(k_ref[0, kb * tk:kb * tk + n_keys, :], q)
        if first_masked_key is not None:
            key_idx = lax.broadcasted_iota(jnp.int32, (n_keys, cols), 0) - first_masked_key
            qry_idx = lax.broadcasted_iota(jnp.int32, (n_keys, cols), 1) & (tq - 1)
            st = jnp.where(key_idx <= qry_idx, st, -jnp.inf)
        dst_ref[0:n_keys, :] = st

    def accumulate(step, src_ref):
        kb, n_keys, _ = step
        st = src_ref[0:n_keys, :]
        m_old = m_ref[...]
        m_new = jnp.maximum(m_old, jnp.max(st, axis=0, keepdims=True))
        p = jnp.exp(st - m_new)
        alpha = jnp.exp(m_old - m_new)
        l_ref[...] = alpha * l_ref[...] + jnp.sum(p, axis=0, keepdims=True)
        acc_ref[...] = alpha * acc_ref[...] + _dot(vt_ref[0, kb, :, :n_keys], p.astype(BF16))
        m_ref[...] = m_new

    def query_tile(qi):
        steps = [(kb, tk, None) for kb in range(qi // 2)]
        steps.append((qi // 2, tk, tq) if qi % 2 else (qi // 2, tq, 0))
        m_ref[...] = jnp.full_like(m_ref, -jnp.inf)
        l_ref[...] = jnp.zeros_like(l_ref)
        acc_ref[...] = jnp.zeros_like(acc_ref)
        scores(steps[0], score_refs[0])
        for i, step in enumerate(steps):
            if i + 1 < len(steps):
                scores(steps[i + 1], score_refs[(i + 1) % 2])
            accumulate(step, score_refs[i % 2])
        o = acc_ref[...] / l_ref[...]
        o_ref[0] = o.T.reshape(MLA_HEADS, tq, MLA_KV_LORA).astype(BF16)

    for qi in range(n_tiles):
        pl.when(pl.program_id(1) == qi)(functools.partial(query_tile, qi))


def _mla_attn_prompt(q, kext, vt):
    b, _, t, _ = q.shape
    tq = vt.shape[-1] // 2
    assert t % (2 * tq) == 0 and tq & (tq - 1) == 0
    return pl.pallas_call(
        functools.partial(_mla_attn_prompt_kernel, tq=tq, n_tiles=t // tq),
        grid=(b, t // tq),
        in_specs=[pl.BlockSpec((1, MLA_HEADS, tq, MLA_KEY_DIM), lambda i, j: (i, 0, j, 0)),
                  pl.BlockSpec((1, t, MLA_KEY_DIM), lambda i, j: (i, 0, 0)),
                  pl.BlockSpec((1, t // (2 * tq), MLA_KV_LORA, 2 * tq), lambda i, j: (i, 0, 0, 0))],
        out_specs=pl.BlockSpec((1, MLA_HEADS, tq, MLA_KV_LORA), lambda i, j: (i, 0, j, 0)),
        out_shape=jax.ShapeDtypeStruct((b, MLA_HEADS, t, MLA_KV_LORA), BF16),
        scratch_shapes=[pltpu.VMEM((1, MLA_HEADS * tq), F32), pltpu.VMEM((1, MLA_HEADS * tq), F32),
                        pltpu.VMEM((MLA_KV_LORA, MLA_HEADS * tq), F32),
                        pltpu.VMEM((2 * tq, MLA_HEADS * tq), F32),
                        pltpu.VMEM((2 * tq, MLA_HEADS * tq), F32)],
        compiler_params=_params(("parallel", "arbitrary"), 40),
        name="mla_attn_prompt",
    )(q, kext, vt)


def _mla_attn_sample_kernel(pt_ref, q_ref, knew_ref, lat_hbm, krt_hbm, o_ref,
                            cbuf, rbuf, s_ref, sems, *, t_new, n_pages):
    g = pl.program_id(0)
    slot = g % 2
    n_rows = q_ref.shape[0]
    n_keys = n_pages * PAGE_SIZE
    n_chunks = n_keys // MLA_KEY_CHUNK

    def page_copies(group, slot_):
        copies = []
        for r in range(n_rows):
            for p in range(n_pages):
                page = pt_ref[(group * n_rows + r) * n_pages + p]
                keys = pl.ds(p * PAGE_SIZE, PAGE_SIZE)
                copies.append(pltpu.make_async_copy(lat_hbm.at[page], cbuf.at[slot_, r, keys, :],
                                                    sems.at[0, slot_]))
                copies.append(pltpu.make_async_copy(krt_hbm.at[page], rbuf.at[slot_, r, :, keys],
                                                    sems.at[1, slot_]))
        return copies

    @pl.when(g == 0)
    def _():
        for cp in page_copies(g, slot):
            cp.start()

    @pl.when(g + 1 < pl.num_programs(0))
    def _():
        for cp in page_copies(g + 1, 1 - slot):
            cp.start()

    for cp in page_copies(g, slot):
        cp.wait()

    def past_scores(r, j):
        keys = slice(j * MLA_KEY_CHUNK, (j + 1) * MLA_KEY_CHUNK)
        q = q_ref[r]
        c = cbuf[slot, r, keys, :].astype(BF16)
        rt = rbuf[slot, r, :, keys].astype(BF16)
        s_ref[r, :, keys] = (_dot_nt(q[:, :MLA_KV_LORA], c)
                             + _dot(q[:, MLA_KV_LORA:MLA_KV_LORA + MLA_ROPE], rt))

    for j in range(n_chunks):
        past_scores(0, j)
    for r in range(n_rows):
        kn = knew_ref[r]
        sn = _dot_nt(q_ref[r], kn)
        row_t = lax.broadcasted_iota(jnp.int32, sn.shape, 0) % t_new
        key = lax.broadcasted_iota(jnp.int32, sn.shape, 1)
        s_ref[r, :, n_keys:] = jnp.where((key < t_new) & (key <= row_t), sn, -jnp.inf)
        s = s_ref[r]
        e = jnp.exp(s - jnp.max(s, axis=-1, keepdims=True))
        denom = jnp.sum(e, axis=-1, keepdims=True)
        eb = e.astype(BF16)
        acc = _dot(eb[:, n_keys:], kn[:, :MLA_KV_LORA])
        for j in range(n_chunks):
            keys = slice(j * MLA_KEY_CHUNK, (j + 1) * MLA_KEY_CHUNK)
            acc = acc + _dot(eb[:, keys], cbuf[slot, r, keys, :].astype(BF16))
            if r + 1 < n_rows:
                past_scores(r + 1, j)
        o_ref[r] = acc / denom


def _mla_attn_sample(page_table, q, cache_lat, cache_krt, knew, t_new):
    b, rows, _ = q.shape
    n_pages = page_table.shape[1]
    n_keys = n_pages * PAGE_SIZE
    nr = MLA_SAMPLE_ROWS_PER_STEP
    assert n_keys % MLA_KEY_CHUNK == 0 and b % nr == 0
    grid_spec = pltpu.PrefetchScalarGridSpec(
        num_scalar_prefetch=1,
        grid=(b // nr,),
        in_specs=[pl.BlockSpec((nr, rows, MLA_KEY_DIM), lambda i, pt: (i, 0, 0)),
                  pl.BlockSpec((nr, PAGE_SIZE, MLA_KEY_DIM), lambda i, pt: (i, 0, 0)),
                  pl.BlockSpec(memory_space=pl.ANY),
                  pl.BlockSpec(memory_space=pl.ANY)],
        out_specs=pl.BlockSpec((nr, rows, MLA_KV_LORA), lambda i, pt: (i, 0, 0)),
        scratch_shapes=[pltpu.VMEM((2, nr, n_keys, MLA_KV_LORA), F32),
                        pltpu.VMEM((2, nr, MLA_ROPE, n_keys), F32),
                        pltpu.VMEM((nr, rows, n_keys + PAGE_SIZE), F32),
                        pltpu.SemaphoreType.DMA((2, 2))],
    )
    return pl.pallas_call(
        functools.partial(_mla_attn_sample_kernel, t_new=t_new, n_pages=n_pages),
        grid_spec=grid_spec,
        out_shape=jax.ShapeDtypeStruct((b, rows, MLA_KV_LORA), F32),
        compiler_params=_params(("arbitrary",), 56),
        name="mla_attn_sample",
    )(page_table.reshape(-1), q, knew, cache_lat, cache_krt)


def _mla_out_kernel(o_ref, wuv_ref, wout_ref, x_ref, y_ref, cat_ref):
    for hh in range(MLA_HEADS):
        cat_ref[:, hh * MLA_V:(hh + 1) * MLA_V] = _dot(o_ref[0, hh], wuv_ref[hh]).astype(BF16)
    y_ref[0] = x_ref[0] + _dot(cat_ref[...], wout_ref[...])


def _mla_out(o_lat, wuv, w_out, x, tm):
    nb, t, _ = x.shape
    assert t % tm == 0
    return pl.pallas_call(
        _mla_out_kernel,
        grid=(nb, t // tm),
        in_specs=[pl.BlockSpec((1, MLA_HEADS, tm, MLA_KV_LORA), lambda i, j: (i, 0, j, 0)),
                  _resident(wuv.shape), _resident(w_out.shape),
                  pl.BlockSpec((1, tm, D_MODEL), lambda i, j: (i, j, 0))],
        out_specs=pl.BlockSpec((1, tm, D_MODEL), lambda i, j: (i, j, 0)),
        out_shape=jax.ShapeDtypeStruct(x.shape, F32),
        scratch_shapes=[pltpu.VMEM((tm, MLA_HEADS * MLA_V), BF16)],
        compiler_params=_params(("parallel", "parallel"), 32),
        name="mla_out",
    )(o_lat, wuv, w_out, x)


def _pool_prompt_kernel(x_ref, xp_ref, g_ref, w_ref, sc_ref, o_ref, hl_ref, ext_ref, tmp_ref):
    j = pl.program_id(1)
    x = x_ref[0]
    tm = x.shape[0]
    h = _rms(x, g_ref[...])
    hp = _rms(xp_ref[0], g_ref[...])
    ext_ref[0:POOL_PREV, :] = jnp.where(j == 0, 0.0, hp)
    ext_ref[POOL_PREV:, :] = h
    tmp_ref[0:POOL_ALIGN, :] = jnp.zeros((POOL_ALIGN, D_MODEL), F32)
    n = POOL_PREV + tm - POOL_ALIGN
    pos = (j * tm + lax.broadcasted_iota(jnp.int32, (tm, 1), 0)).astype(F32)
    outs = []
    for gi, w in enumerate(POOL_WINDOWS):
        lanes = slice(gi * POOL_GROUP_DIM, (gi + 1) * POOL_GROUP_DIM)
        src, dst = ext_ref, tmp_ref
        shift = 1
        while shift < w:
            dst[POOL_ALIGN:, lanes] = (src[POOL_ALIGN:, lanes]
                                       + src[POOL_ALIGN - shift:POOL_ALIGN - shift + n, lanes])
            src, dst = dst, src
            shift *= 2
        pooled = src[POOL_PREV:, lanes] / jnp.minimum(pos + 1.0, float(w)) - h[:, lanes]
        outs.append(_dot(pooled.astype(BF16), w_ref[gi]))
    o_ref[0] = x + jnp.concatenate(outs, axis=-1) * sc_ref[...]

    @pl.when(j == pl.num_programs(1) - 1)
    def _():
        hl_ref[0] = h[tm - POOL_TAIL:]


def _pool_prompt(x, gain, w_pool, scale):
    b, t, _ = x.shape
    tm = 512
    assert t % tm == 0 and tm % POOL_PREV == 0
    per_tile = tm // POOL_PREV
    return pl.pallas_call(
        _pool_prompt_kernel,
        grid=(b, t // tm),
        in_specs=[pl.BlockSpec((1, tm, D_MODEL), lambda i, j: (i, j, 0)),
                  pl.BlockSpec((1, POOL_PREV, D_MODEL),
                               lambda i, j: (i, jnp.maximum(j * per_tile - 1, 0), 0)),
                  _resident(gain.shape), _resident(w_pool.shape), _resident(scale.shape)],
        out_specs=[pl.BlockSpec((1, tm, D_MODEL), lambda i, j: (i, j, 0)),
                   pl.BlockSpec((1, POOL_TAIL, D_MODEL), lambda i, j: (i, 0, 0))],
        out_shape=[jax.ShapeDtypeStruct(x.shape, F32),
                   jax.ShapeDtypeStruct((b, POOL_TAIL, D_MODEL), F32)],
        scratch_shapes=[pltpu.VMEM((POOL_PREV + tm, D_MODEL), F32),
                        pltpu.VMEM((POOL_PREV + tm, D_MODEL), F32)],
        compiler_params=_params(("parallel", "arbitrary"), 32),
        name="pool_prompt",
    )(x, x, gain, w_pool, scale)


def _pool_sample_kernel(x_ref, pre_ref, g_ref, w_ref, sc_ref, o_ref, h_ref, *, pos0):
    t_len = x_ref.shape[0]
    hs = [_rms(x_ref[t], g_ref[...]) for t in range(t_len)]
    for t in range(t_len):
        h_ref[t] = hs[t]

    def ext(e, lanes):
        return pre_ref[e, :, lanes] if e < POOL_BUF else hs[e - POOL_BUF][:, lanes]

    for t in range(t_len):
        outs = []
        for gi, w in enumerate(POOL_WINDOWS):
            lanes = slice(gi * POOL_GROUP_DIM, (gi + 1) * POOL_GROUP_DIM)
            acc = hs[t][:, lanes]
            for k in range(1, w):
                acc = acc + ext(POOL_BUF + t - k, lanes)
            pooled = acc / min(pos0 + t + 1.0, float(w)) - hs[t][:, lanes]
            outs.append(_dot(pooled.astype(BF16), w_ref[gi]))
        o_ref[t] = x_ref[t] + jnp.concatenate(outs, axis=-1) * sc_ref[...]


def _pool_sample(x_t, prefix_t, gain, w_pool, scale, pos0):
    vmem = pl.BlockSpec(memory_space=pltpu.VMEM)
    return pl.pallas_call(
        functools.partial(_pool_sample_kernel, pos0=pos0),
        in_specs=[vmem] * 5,
        out_specs=[vmem, vmem],
        out_shape=[jax.ShapeDtypeStruct(x_t.shape, F32), jax.ShapeDtypeStruct(x_t.shape, F32)],
        compiler_params=_params(None, 32),
        name="pool_sample",
    )(x_t, prefix_t, gain, w_pool, scale)


def _rope_tables(pos, half):
    inv = ROPE_BASE ** (-jnp.arange(half, dtype=F32) / half)
    ang = pos.astype(F32)[:, None] * inv[None, :]
    return jnp.cos(ang), jnp.sin(ang)


def _mla_rope_tables(pos):
    cos, sin = _rope_tables(pos, MLA_ROPE // 2)
    zeros = jnp.zeros((pos.shape[0], 128 - MLA_ROPE), F32)
    return (jnp.concatenate([cos, cos, zeros], axis=-1),
            jnp.concatenate([-sin, sin, zeros], axis=-1))


def _swap_halves(w):
    half = w.shape[-1] // 2
    return jnp.concatenate([w[..., half:], w[..., :half]], axis=-1)


def _pad_lanes(w, width):
    return jnp.pad(w, [(0, 0)] * (w.ndim - 1) + [(0, width - w.shape[-1])])


def _row(v):
    return v.reshape(1, -1).astype(F32)


def kernel(x_prompt, x_sample, state_ret, cache_mla_latent, cache_mla_krope, page_table, state_pool,
           norm_mix, norm_ffn, norm_final,
           ret_w_in, ret_gn_gain, ret_w_out,
           cm_w_in, cm_ln_gain, cm_w_spatial, cm_b_spatial, cm_w_out,
           mla_w_down, mla_q_norm, mla_kv_norm, mla_w_uq, mla_w_uk, mla_w_uv, mla_w_out,
           pool_w, pool_scale,
           ffn_w_gate_up, ffn_w_down):
    bp, tp, _ = x_prompt.shape
    bs, ts, _ = x_sample.shape
    xp = x_prompt
    xs = x_sample.reshape(bs * ts, D_MODEL)
    pos_p = jnp.arange(tp)
    pos_s = PAST_LEN + jnp.arange(ts)

    def ffn(xp_, xs_, i, final):
        yp, ys = _ffn(xp_.reshape(-1, D_MODEL), xs_, _row(norm_ffn[i]), ffn_w_gate_up, ffn_w_down, i,
                      _row(norm_final), final)
        return yp.reshape(xp_.shape), ys

    log_g = jnp.log1p(-jnp.exp2(-5.0 - jnp.arange(RET_HEADS, dtype=F32)))
    w_in = ret_w_in[0]
    w_out = ret_w_out[0]
    gn_gain = _row(ret_gn_gain[0])
    cos_p, sin_p = _rope_tables(pos_p, RET_QK_DIM // 2)
    cos_s, sin_s = _rope_tables(pos_s, RET_QK_DIM // 2)
    xp, ret_state_p = _ret_prompt(xp, _row(norm_mix[0]), w_in, cos_p, sin_p,
                                  _ret_tables(float(RET_PROMPT_CHUNK), RET_PROMPT_CHUNK, log_g),
                                  gn_gain, w_out)
    proj_s = _norm_matmul(xs, _row(norm_mix[0]), w_in, 1024)
    y_s, ret_state_s = _ret_sample(proj_s, state_ret[0], cos_s, sin_s,
                                   _ret_tables(float(ts), RET_CHUNK, log_g), gn_gain)
    xs = _matmul_residual(y_s, w_out, xs)
    xp, xs = ffn(xp, xs, 0, False)

    cw_in = cm_w_in[0]
    cw_out = cm_w_out[0]
    bias_full = jnp.repeat(jnp.transpose(cm_b_spatial[0]), CM_GROUP_DIM, axis=1)
    xp2, cm_v_p = _cm_prompt(xp.reshape(bp * tp, D_MODEL), _row(norm_mix[1]), cw_in,
                             _row(cm_ln_gain[0]), cm_w_spatial[0], bias_full, cw_out, tp)
    xp = xp2.reshape(bp, tp, D_MODEL)
    xs_t, cm_v_s_t = _cm_sample(xs.reshape(bs, ts, D_MODEL).transpose(1, 0, 2), _row(norm_mix[1]), cw_in,
                                _row(cm_ln_gain[0]),
                                cm_w_spatial[0][:, :ts, :ts].reshape(CM_GROUPS, ts * ts),
                                cm_b_spatial[0][:, :ts], cw_out)
    xs = xs_t.transpose(1, 0, 2).reshape(bs * ts, D_MODEL)
    cm_v_s = cm_v_s_t.transpose(1, 0, 2)
    xp, xs = ffn(xp, xs, 1, False)

    wd = mla_w_down[0]
    kr_w = wd[:, MLA_Q_LORA + MLA_KV_LORA:]
    wd_ext = jnp.concatenate([wd[:, :MLA_Q_LORA + MLA_KV_LORA], _pad_lanes(kr_w, 128),
                              _pad_lanes(_swap_halves(kr_w), 128)], axis=-1).astype(BF16)
    wuq = mla_w_uq[0].reshape(MLA_Q_LORA, MLA_HEADS, MLA_NOPE + MLA_ROPE)
    wuq_rope = wuq[:, :, MLA_NOPE:]
    wuq_ext = jnp.concatenate([
        wuq[:, :, :MLA_NOPE].reshape(MLA_Q_LORA, -1),
        _pad_lanes(wuq_rope, 128).reshape(MLA_Q_LORA, -1),
        _pad_lanes(_swap_halves(wuq_rope), 128).reshape(MLA_Q_LORA, -1)], axis=-1).astype(BF16)
    wuk_t = mla_w_uk[0].transpose(1, 2, 0).astype(BF16)
    wuv = mla_w_uv[0].transpose(1, 0, 2).astype(BF16)
    mw_out = mla_w_out[0].astype(BF16)
    cos2_p, sin2_p = _mla_rope_tables(pos_p)
    cos2_s, sin2_s = _mla_rope_tables(jnp.tile(pos_s, bs))
    gq, gkv = _row(mla_q_norm[0]), _row(mla_kv_norm[0])
    q_p, lat_p, kr_p, kext_p = _mla_q(xp, _row(norm_mix[2]), wd_ext, gq, gkv, wuq_ext, wuk_t,
                                      cos2_p, sin2_p, 512)
    tk = 2 * MLA_ATTN_TQ
    vt_p = kext_p[:, :, :MLA_KV_LORA].reshape(bp, tp // tk, tk, MLA_KV_LORA).swapaxes(2, 3)
    o_p = _mla_attn_prompt(q_p, kext_p, vt_p)
    xp = _mla_out(o_p, wuv, mw_out, xp, 512)
    ns = bs * ts
    q_s, lat_s, kr_s, kext_s = _mla_q(xs.reshape(1, ns, D_MODEL), _row(norm_mix[2]), wd_ext, gq, gkv,
                                      wuq_ext, wuk_t, cos2_s, sin2_s, ns)
    q_s = q_s.reshape(MLA_HEADS, bs, ts, MLA_KEY_DIM).transpose(1, 0, 2, 3).reshape(bs, MLA_HEADS * ts, MLA_KEY_DIM)
    knew = jnp.pad(kext_s.reshape(bs, ts, MLA_KEY_DIM), ((0, 0), (0, PAGE_SIZE - ts), (0, 0)))
    o_s = _mla_attn_sample(page_table, q_s, cache_mla_latent[0],
                           jnp.swapaxes(cache_mla_krope[0], 1, 2), knew, ts)
    o_s = o_s.reshape(bs, MLA_HEADS, ts, MLA_KV_LORA).transpose(1, 0, 2, 3).reshape(1, MLA_HEADS, ns, MLA_KV_LORA)
    xs = _mla_out(o_s.astype(BF16), wuv, mw_out, xs.reshape(1, ns, D_MODEL), ns).reshape(ns, D_MODEL)
    lat_s = lat_s.reshape(bs, ts, MLA_KV_LORA)
    kr_s = kr_s.reshape(bs, ts, MLA_ROPE)
    xp, xs = ffn(xp, xs, 2, False)

    pw = pool_w[0].astype(BF16)
    xp, h_last = _pool_prompt(xp, _row(norm_mix[3]), pw, _row(pool_scale[0]))
    pool_state_p = h_last[:, POOL_TAIL - POOL_BUF:]
    xs_t, h_s_t = _pool_sample(xs.reshape(bs, ts, D_MODEL).transpose(1, 0, 2),
                               state_pool[0].transpose(1, 0, 2), _row(norm_mix[3]), pw,
                               _row(pool_scale[0]), float(PAST_LEN))
    xs = xs_t.transpose(1, 0, 2).reshape(bs * ts, D_MODEL)
    pool_state_s = jnp.concatenate([state_pool[0], h_s_t.transpose(1, 0, 2)], axis=1)[:, -POOL_BUF:]
    xp, xs = ffn(xp, xs, 3, True)

    return (xp, xs.reshape(bs, ts, D_MODEL),
            ret_state_p[None], ret_state_s[None],
            cm_v_p.reshape(bp, CM_CHUNK, D_MODEL)[None], cm_v_s[None],
            lat_p[None], kr_p[None], lat_s[None], kr_s[None],
            pool_state_p[None], pool_state_s[None])
```

```python
import functools

import jax
import jax.numpy as jnp
from jax import lax
from jax.experimental import pallas as pl
from jax.experimental.pallas import tpu as pltpu

F32 = jnp.float32
BF16 = jnp.bfloat16

D_MODEL = 1024
NORM_EPS = 1e-6
ROPE_BASE = 10000.0
PAST_LEN = 8192
PAGE_SIZE = 128

RET_HEADS = 4
RET_QK_DIM = 256
RET_V_DIM = 512
RET_CHUNK = 128
RET_PROMPT_CHUNK = 256
RET_SAMPLE_ROWS_PER_STEP = 4

CM_CHUNK = 128
CM_GROUPS = 4
CM_GROUP_DIM = 256

MLA_HEADS = 8
MLA_Q_LORA = 384
MLA_KV_LORA = 256
MLA_NOPE = 128
MLA_ROPE = 64
MLA_V = 128
MLA_SCALE = (MLA_NOPE + MLA_ROPE) ** -0.5
MLA_KEY_DIM = MLA_KV_LORA + 128
MLA_KEY_CHUNK = 1024
MLA_SAMPLE_ROWS_PER_STEP = 2
MLA_ATTN_TQ = 256

POOL_WINDOWS = (2, 4, 8, 16)
POOL_GROUP_DIM = 256
POOL_BUF = 15
POOL_PREV = 32
POOL_ALIGN = 8
POOL_TAIL = 16

FFN_HIDDEN = 2816
FFN_CHUNK = 256

V7X_VMEM_BYTES = 64 * 1024 * 1024
MIB = 1024 * 1024


def _params(semantics, vmem_mib):
    assert vmem_mib * MIB < V7X_VMEM_BYTES
    return pltpu.CompilerParams(dimension_semantics=semantics,
                                vmem_limit_bytes=vmem_mib * MIB)


def _resident(shape):
    nd = len(shape)
    return pl.BlockSpec(shape, lambda *_: (0,) * nd, pipeline_mode=pl.Buffered(1))


def _rms(x, g):
    return x * lax.rsqrt(jnp.mean(x * x, axis=-1, keepdims=True) + NORM_EPS) * g


def _dot(a, b):
    return jnp.dot(a, b, preferred_element_type=F32)


def _wdot(a, w):
    return jnp.dot(a, w.astype(BF16), preferred_element_type=F32)


def _dot_nt(a, b):
    return lax.dot_general(a, b, (((1,), (1,)), ((), ())), preferred_element_type=F32)


def _ffn_kernel(xp_ref, xs_ref, g_ref, wgu_hbm, wd_hbm, gf_ref, op_ref, os_ref,
                wgu_ref, wd_ref, sems, *, final, layer):
    n_chunks = FFN_HIDDEN // FFN_CHUNK

    def chunk_copies(j):
        cols = pl.ds(j * FFN_CHUNK, FFN_CHUNK)
        ucols = pl.ds(FFN_HIDDEN + j * FFN_CHUNK, FFN_CHUNK)
        return (pltpu.make_async_copy(wgu_hbm.at[layer, :, cols], wgu_ref.at[:, cols], sems.at[0, j]),
                pltpu.make_async_copy(wgu_hbm.at[layer, :, ucols], wgu_ref.at[:, ucols], sems.at[1, j]),
                pltpu.make_async_copy(wd_hbm.at[layer, cols, :], wd_ref.at[cols, :], sems.at[2, j]))

    def tile(x_ref, o_ref, wait_for_weights):
        x = x_ref[...]
        h = _rms(x, g_ref[...]).astype(BF16)
        acc = x
        for j in range(n_chunks):
            if wait_for_weights:
                for cp in chunk_copies(j):
                    cp.wait()
            cols = slice(j * FFN_CHUNK, (j + 1) * FFN_CHUNK)
            ucols = slice(FFN_HIDDEN + j * FFN_CHUNK, FFN_HIDDEN + (j + 1) * FFN_CHUNK)
            g = _dot(h, wgu_ref[:, cols].astype(BF16))
            u = _dot(h, wgu_ref[:, ucols].astype(BF16))
            a = (g * jax.nn.sigmoid(g)) * u
            acc = acc + _dot(a.astype(BF16), wd_ref[cols, :].astype(BF16))
        o_ref[...] = _rms(acc, gf_ref[...]) if final else acc

    i = pl.program_id(0)
    is_prompt = i < pl.num_programs(0) - 1

    @pl.when(i == 0)
    def _():
        for j in range(n_chunks):
            for cp in chunk_copies(j):
                cp.start()
        tile(xp_ref, op_ref, True)

    @pl.when((i > 0) & is_prompt)
    def _():
        tile(xp_ref, op_ref, False)

    @pl.when(jnp.logical_not(is_prompt))
    def _():
        tile(xs_ref, os_ref, False)


def _ffn(xp, xs, gain, wgu_all, wd_all, layer, final_gain, final):
    tm = xs.shape[0]
    assert xp.shape[0] % tm == 0 and xp.shape[0] >= 2 * tm
    n_prompt = xp.shape[0] // tm
    hbm = pl.BlockSpec(memory_space=pl.ANY)
    prompt_tile = pl.BlockSpec((tm, D_MODEL), lambda i: (jnp.minimum(i, n_prompt - 1), 0))
    sample_tile = pl.BlockSpec((tm, D_MODEL), lambda i: (0, 0))
    return pl.pallas_call(
        functools.partial(_ffn_kernel, final=final, layer=layer),
        grid=(n_prompt + 1,),
        in_specs=[prompt_tile, sample_tile, _resident(gain.shape), hbm, hbm,
                  _resident(final_gain.shape)],
        out_specs=[prompt_tile, sample_tile],
        out_shape=[jax.ShapeDtypeStruct(xp.shape, F32), jax.ShapeDtypeStruct(xs.shape, F32)],
        scratch_shapes=[pltpu.VMEM(wgu_all.shape[1:], F32), pltpu.VMEM(wd_all.shape[1:], F32),
                        pltpu.SemaphoreType.DMA((3, FFN_HIDDEN // FFN_CHUNK))],
        compiler_params=_params(("arbitrary",), 58),
        name="ffn",
    )(xp, xs, gain, wgu_all, wd_all, final_gain)


def _norm_matmul_kernel(x_ref, g_ref, w_ref, o_ref):
    h = _rms(x_ref[...], g_ref[...]).astype(BF16)
    o_ref[...] = _wdot(h, w_ref[...])


def _norm_matmul(x, gain, w, tn):
    n, dout = x.shape[0], w.shape[1]
    assert dout % tn == 0
    return pl.pallas_call(
        _norm_matmul_kernel,
        grid=(dout // tn,),
        in_specs=[pl.BlockSpec((n, D_MODEL), lambda j: (0, 0)),
                  pl.BlockSpec(gain.shape, lambda j: (0, 0)),
                  pl.BlockSpec((D_MODEL, tn), lambda j: (0, j))],
        out_specs=pl.BlockSpec((n, tn), lambda j: (0, j)),
        out_shape=jax.ShapeDtypeStruct((n, dout), F32),
        compiler_params=_params(("parallel",), 32),
        name="norm_matmul",
    )(x, gain, w)


def _matmul_residual_kernel(y_ref, w_ref, x_ref, o_ref):
    o_ref[...] = x_ref[...] + _wdot(y_ref[...].astype(BF16), w_ref[...])


def _matmul_residual(y, w, x):
    n = x.shape[0]
    tm = min(n, 512)
    assert n % tm == 0
    return pl.pallas_call(
        _matmul_residual_kernel,
        grid=(n // tm,),
        in_specs=[pl.BlockSpec((tm, y.shape[1]), lambda i: (i, 0)),
                  _resident(w.shape),
                  pl.BlockSpec((tm, D_MODEL), lambda i: (i, 0))],
        out_specs=pl.BlockSpec((tm, D_MODEL), lambda i: (i, 0)),
        out_shape=jax.ShapeDtypeStruct(x.shape, F32),
        compiler_params=_params(("parallel",), 32),
        name="matmul_residual",
    )(y, w, x)


def _rope_halves(x, cos, sin):
    half = cos.shape[-1]
    x1, x2 = x[:, :half], x[:, half:]
    return jnp.concatenate([x1 * cos - x2 * sin, x1 * sin + x2 * cos], axis=-1)


def _ret_chunk(q, k, v, s, dmat, qdec, kdec, sdec):
    vb = v.astype(BF16)
    scores = _dot_nt(q.astype(BF16), k.astype(BF16)) * dmat
    o = _dot(scores.astype(BF16), vb) + _dot((q * qdec).astype(BF16), s.astype(BF16))
    s_new = sdec * s + _dot((k * kdec).T.astype(BF16), vb)
    return o, s_new


def _group_norm_gate(o, gate, gain):
    mu = jnp.mean(o, axis=-1, keepdims=True)
    oc = o - mu
    var = jnp.mean(oc * oc, axis=-1, keepdims=True)
    on = oc * lax.rsqrt(var + NORM_EPS)
    return (gate * jax.nn.sigmoid(gate)) * on * gain


def _ret_prompt_kernel(x_ref, g_ref, win_ref, cos_ref, sin_ref, dmat_ref, qdec_ref,
                       kdec_ref, sdec_ref, gng_ref, wout_ref, o_ref, st_ref,
                       s_ref, y_ref):
    j = pl.program_id(1)

    @pl.when(j == 0)
    def _():
        s_ref[...] = jnp.zeros_like(s_ref)

    x = x_ref[0]
    tm = x.shape[0]
    h = _rms(x, g_ref[...]).astype(BF16)
    for hh in range(RET_HEADS):
        qs = slice(hh * RET_QK_DIM, (hh + 1) * RET_QK_DIM)
        ks = slice(D_MODEL + hh * RET_QK_DIM, D_MODEL + (hh + 1) * RET_QK_DIM)
        vs = slice(2 * D_MODEL + hh * RET_V_DIM, 2 * D_MODEL + (hh + 1) * RET_V_DIM)
        gs = slice(4 * D_MODEL + hh * RET_V_DIM, 4 * D_MODEL + (hh + 1) * RET_V_DIM)
        q = _rope_halves(_wdot(h, win_ref[:, qs]), cos_ref[...], sin_ref[...]) * (RET_QK_DIM ** -0.5)
        k = _rope_halves(_wdot(h, win_ref[:, ks]), cos_ref[...], sin_ref[...])
        v = _wdot(h, win_ref[:, vs])
        gate = _wdot(h, win_ref[:, gs])
        gain = gng_ref[:, hh * RET_V_DIM:(hh + 1) * RET_V_DIM]
        chunk = dmat_ref.shape[1]
        for c in range(tm // chunk):
            rows = slice(c * chunk, (c + 1) * chunk)
            o, s_new = _ret_chunk(q[rows], k[rows], v[rows], s_ref[hh], dmat_ref[hh],
                                  qdec_ref[hh], kdec_ref[hh], sdec_ref[hh])
            s_ref[hh] = s_new
            y_ref[rows, hh * RET_V_DIM:(hh + 1) * RET_V_DIM] = (
                _group_norm_gate(o, gate[rows], gain).astype(BF16))
    o_ref[0] = x + _wdot(y_ref[...], wout_ref[...])

    @pl.when(j == pl.num_programs(1) - 1)
    def _():
        st_ref[0] = s_ref[...]


def _ret_tables(length, chunk, log_g):
    idx = jnp.arange(chunk, dtype=F32)
    valid = idx < length
    rel = idx[:, None] - idx[None, :]
    ok = (rel >= 0) & valid[:, None] & valid[None, :]
    dmat = jnp.where(ok[None], jnp.exp(jnp.maximum(rel, 0.0)[None] * log_g[:, None, None]), 0.0)
    qd = jnp.where(valid[None], jnp.exp((idx + 1.0)[None, :] * log_g[:, None]), 0.0)
    kd = jnp.where(valid[None], jnp.exp((length - 1.0 - idx)[None, :] * log_g[:, None]), 0.0)
    qdec = jnp.broadcast_to(qd[:, :, None], (RET_HEADS, chunk, RET_QK_DIM))
    kdec = jnp.broadcast_to(kd[:, :, None], (RET_HEADS, chunk, RET_QK_DIM))
    sdec = jnp.broadcast_to(jnp.exp(length * log_g)[:, None, None], (RET_HEADS, 1, RET_V_DIM))
    return dmat, qdec, kdec, sdec


def _ret_prompt(x, gain, w_in, cos, sin, tables, gn_gain, w_out):
    b, t, _ = x.shape
    tm = 256
    assert t % tm == 0
    dmat, qdec, kdec, sdec = tables
    return pl.pallas_call(
        _ret_prompt_kernel,
        grid=(b, t // tm),
        in_specs=[pl.BlockSpec((1, tm, D_MODEL), lambda i, j: (i, j, 0)),
                  _resident(gain.shape), _resident(w_in.shape),
                  pl.BlockSpec((tm, RET_QK_DIM // 2), lambda i, j: (j, 0)),
                  pl.BlockSpec((tm, RET_QK_DIM // 2), lambda i, j: (j, 0)),
                  _resident(dmat.shape), _resident(qdec.shape), _resident(kdec.shape),
                  _resident(sdec.shape), _resident(gn_gain.shape), _resident(w_out.shape)],
        out_specs=[pl.BlockSpec((1, tm, D_MODEL), lambda i, j: (i, j, 0)),
                   pl.BlockSpec((1, RET_HEADS, RET_QK_DIM, RET_V_DIM), lambda i, j: (i, 0, 0, 0))],
        out_shape=[jax.ShapeDtypeStruct(x.shape, F32),
                   jax.ShapeDtypeStruct((b, RET_HEADS, RET_QK_DIM, RET_V_DIM), F32)],
        scratch_shapes=[pltpu.VMEM((RET_HEADS, RET_QK_DIM, RET_V_DIM), F32),
                        pltpu.VMEM((tm, RET_HEADS * RET_V_DIM), BF16)],
        compiler_params=_params(("parallel", "arbitrary"), 56),
        name="ret_prompt",
    )(x, gain, w_in, cos, sin, dmat, qdec, kdec, sdec, gn_gain, w_out)


def _ret_sample_kernel(p_ref, s0_ref, cos_ref, sin_ref, dmat_ref, qdec_ref, kdec_ref,
                       sdec_ref, gng_ref, y_ref, st_ref, qp_ref, kp_ref, vp_ref):
    @pl.when(pl.program_id(0) == 0)
    def _():
        qp_ref[...] = jnp.zeros_like(qp_ref)
        kp_ref[...] = jnp.zeros_like(kp_ref)
        vp_ref[...] = jnp.zeros_like(vp_ref)

    t = cos_ref.shape[0]
    for bb in range(s0_ref.shape[0]):
        rows = slice(bb * t, (bb + 1) * t)
        for hh in range(RET_HEADS):
            qs = slice(hh * RET_QK_DIM, (hh + 1) * RET_QK_DIM)
            ks = slice(D_MODEL + hh * RET_QK_DIM, D_MODEL + (hh + 1) * RET_QK_DIM)
            vs = slice(2 * D_MODEL + hh * RET_V_DIM, 2 * D_MODEL + (hh + 1) * RET_V_DIM)
            gs = slice(4 * D_MODEL + hh * RET_V_DIM, 4 * D_MODEL + (hh + 1) * RET_V_DIM)
            qp_ref[0:t, :] = (_rope_halves(p_ref[rows, qs], cos_ref[...], sin_ref[...])
                              * (RET_QK_DIM ** -0.5))
            kp_ref[0:t, :] = _rope_halves(p_ref[rows, ks], cos_ref[...], sin_ref[...])
            vp_ref[0:t, :] = p_ref[rows, vs]
            o, s_new = _ret_chunk(qp_ref[...], kp_ref[...], vp_ref[...], s0_ref[bb, hh], dmat_ref[hh],
                                  qdec_ref[hh], kdec_ref[hh], sdec_ref[hh])
            st_ref[bb, hh] = s_new
            gain = gng_ref[:, hh * RET_V_DIM:(hh + 1) * RET_V_DIM]
            y_ref[rows, hh * RET_V_DIM:(hh + 1) * RET_V_DIM] = _group_norm_gate(o[0:t], p_ref[rows, gs], gain)


def _ret_sample(proj, s0, cos, sin, tables, gn_gain):
    b = s0.shape[0]
    t = cos.shape[0]
    nb = RET_SAMPLE_ROWS_PER_STEP
    assert b % nb == 0 and proj.shape[0] == b * t
    dmat, qdec, kdec, sdec = tables
    state_spec = pl.BlockSpec((nb, RET_HEADS, RET_QK_DIM, RET_V_DIM), lambda i: (i, 0, 0, 0))
    return pl.pallas_call(
        _ret_sample_kernel,
        grid=(b // nb,),
        in_specs=[pl.BlockSpec((nb * t, proj.shape[1]), lambda i: (i, 0)),
                  state_spec,
                  _resident(cos.shape), _resident(sin.shape),
                  _resident(dmat.shape), _resident(qdec.shape), _resident(kdec.shape),
                  _resident(sdec.shape), _resident(gn_gain.shape)],
        out_specs=[pl.BlockSpec((nb * t, RET_HEADS * RET_V_DIM), lambda i: (i, 0)),
                   state_spec],
        out_shape=[jax.ShapeDtypeStruct((b * t, RET_HEADS * RET_V_DIM), F32),
                   jax.ShapeDtypeStruct(s0.shape, F32)],
        scratch_shapes=[pltpu.VMEM((RET_CHUNK, RET_QK_DIM), F32),
                        pltpu.VMEM((RET_CHUNK, RET_QK_DIM), F32),
                        pltpu.VMEM((RET_CHUNK, RET_V_DIM), F32)],
        compiler_params=_params(("arbitrary",), 48),
        name="ret_sample",
    )(proj, s0, cos, sin, dmat, qdec, kdec, sdec, gn_gain)


def _gelu_tanh(x):
    return x * (0.5 * (1.0 + jnp.tanh(0.7978845608028654 * (x + 0.044715 * (x * x * x)))))


def _layer_norm(v, gain):
    mu = jnp.mean(v, axis=-1, keepdims=True)
    vc = v - mu
    var = jnp.mean(vc * vc, axis=-1, keepdims=True)
    return vc * lax.rsqrt(var + NORM_EPS) * gain


def _cm_prompt_kernel(x_ref, g_ref, win_ref, lng_ref, ws_ref, bias_ref, wout_ref,
                      o_ref, v_ref, z_ref):
    x = x_ref[...]
    tm = x.shape[0]
    h = _rms(x, g_ref[...]).astype(BF16)
    u = _gelu_tanh(_wdot(h, win_ref[:, :D_MODEL]))
    v = _layer_norm(_gelu_tanh(_wdot(h, win_ref[:, D_MODEL:])), lng_ref[...])
    v_ref[...] = v[tm - CM_CHUNK:]
    vb = v.astype(BF16)
    row = lax.broadcasted_iota(jnp.int32, (CM_CHUNK, CM_CHUNK), 0)
    col = lax.broadcasted_iota(jnp.int32, (CM_CHUNK, CM_CHUNK), 1)
    for gi in range(CM_GROUPS):
        lanes = slice(gi * CM_GROUP_DIM, (gi + 1) * CM_GROUP_DIM)
        w = jnp.where(row >= col, ws_ref[gi], 0.0).astype(BF16)
        for c in range(tm // CM_CHUNK):
            rows = slice(c * CM_CHUNK, (c + 1) * CM_CHUNK)
            mixed = _dot(w, vb[rows, lanes]) + bias_ref[:, lanes]
            z_ref[rows, lanes] = (u[rows, lanes] * mixed).astype(BF16)
    o_ref[...] = x + _wdot(z_ref[...], wout_ref[...])


def _cm_prompt(x, gain, w_in, ln_gain, w_s, bias_full, w_out, seq):
    n = x.shape[0]
    tm = 1024
    assert seq % tm == 0 and n % seq == 0
    per_seq = seq // tm
    return pl.pallas_call(
        _cm_prompt_kernel,
        grid=(n // tm,),
        in_specs=[pl.BlockSpec((tm, D_MODEL), lambda i: (i, 0)),
                  _resident(gain.shape), _resident(w_in.shape), _resident(ln_gain.shape),
                  _resident(w_s.shape), _resident(bias_full.shape), _resident(w_out.shape)],
        out_specs=[pl.BlockSpec((tm, D_MODEL), lambda i: (i, 0)),
                   pl.BlockSpec((CM_CHUNK, D_MODEL), lambda i: (i // per_seq, 0))],
        out_shape=[jax.ShapeDtypeStruct(x.shape, F32),
                   jax.ShapeDtypeStruct((n // seq * CM_CHUNK, D_MODEL), F32)],
        scratch_shapes=[pltpu.VMEM((tm, D_MODEL), BF16)],
        compiler_params=_params(("arbitrary",), 40),
        name="cm_prompt",
    )(x, gain, w_in, ln_gain, w_s, bias_full, w_out)


def _cm_sample_kernel(ws_ref, bs_ref, x_ref, g_ref, win_ref, lng_ref, wout_ref, o_ref, v_ref):
    t_len = x_ref.shape[0]
    us, vs = [], []
    for t in range(t_len):
        h = _rms(x_ref[t], g_ref[...]).astype(BF16)
        us.append(_gelu_tanh(_wdot(h, win_ref[:, :D_MODEL])))
        v = _layer_norm(_gelu_tanh(_wdot(h, win_ref[:, D_MODEL:])), lng_ref[...])
        v_ref[t] = v
        vs.append(v)
    for t in range(t_len):
        parts = []
        for gi in range(CM_GROUPS):
            lanes = slice(gi * CM_GROUP_DIM, (gi + 1) * CM_GROUP_DIM)
            mixed = jnp.full_like(vs[t][:, lanes], bs_ref[gi, t])
            for s in range(t + 1):
                mixed = mixed + ws_ref[gi, t * t_len + s] * vs[s][:, lanes]
            parts.append(us[t][:, lanes] * mixed)
        z = jnp.concatenate(parts, axis=-1).astype(BF16)
        o_ref[t] = x_ref[t] + _wdot(z, wout_ref[...])


def _cm_sample(x_t, gain, w_in, ln_gain, w_s_small, b_s_small, w_out):
    smem = pl.BlockSpec(memory_space=pltpu.SMEM)
    vmem = pl.BlockSpec(memory_space=pltpu.VMEM)
    return pl.pallas_call(
        _cm_sample_kernel,
        in_specs=[smem, smem, vmem, vmem, vmem, vmem, vmem],
        out_specs=[vmem, vmem],
        out_shape=[jax.ShapeDtypeStruct(x_t.shape, F32), jax.ShapeDtypeStruct(x_t.shape, F32)],
        compiler_params=_params(None, 32),
        name="cm_sample",
    )(w_s_small, b_s_small, x_t, gain, w_in, ln_gain, w_out)


def _mla_q_kernel(x_ref, g_ref, wd_ref, gq_ref, gkv_ref, wuq_ref, wuk_ref, cos_ref, sin_ref,
                  q_ref, c_ref, krt_ref, kext_ref, vt_ref):
    h = _rms(x_ref[0], g_ref[...]).astype(BF16)
    d = _dot(h, wd_ref[...])
    cq = _rms(d[:, :MLA_Q_LORA], gq_ref[...]).astype(BF16)
    ckv = _rms(d[:, MLA_Q_LORA:MLA_Q_LORA + MLA_KV_LORA], gkv_ref[...])
    cos, sin = cos_ref[...], sin_ref[...]
    base = MLA_Q_LORA + MLA_KV_LORA
    krp = d[:, base:base + 128] * cos + d[:, base + 128:base + 256] * sin
    c_ref[0] = ckv
    krt_ref[0] = krp.T[:MLA_ROPE]
    kext_ref[0] = jnp.concatenate([ckv.astype(BF16), krp.astype(BF16)], axis=-1)
    vt_ref[0, 0] = ckv.T.astype(BF16)
    nope_w = MLA_HEADS * MLA_NOPE
    for hh in range(MLA_HEADS):
        qn = _dot(cq, wuq_ref[:, hh * MLA_NOPE:(hh + 1) * MLA_NOPE])
        raw = _dot(cq, wuq_ref[:, nope_w + hh * 128:nope_w + (hh + 1) * 128])
        rot = _dot(cq, wuq_ref[:, 2 * nope_w + hh * 128:2 * nope_w + (hh + 1) * 128])
        ql = _dot(qn.astype(BF16), wuk_ref[hh]) * MLA_SCALE
        qr = (raw * cos + rot * sin) * MLA_SCALE
        q_ref[0, hh] = jnp.concatenate([ql.astype(BF16), qr.astype(BF16)], axis=-1)


def _mla_q(x, gain, wd_ext, g_q, g_kv, wuq_ext, wuk_t, cos2, sin2, tm):
    nb, t, _ = x.shape
    assert t % tm == 0
    return pl.pallas_call(
        _mla_q_kernel,
        grid=(nb, t // tm),
        in_specs=[pl.BlockSpec((1, tm, D_MODEL), lambda i, j: (i, j, 0)),
                  _resident(gain.shape), _resident(wd_ext.shape), _resident(g_q.shape),
                  _resident(g_kv.shape), _resident(wuq_ext.shape), _resident(wuk_t.shape),
                  pl.BlockSpec((tm, 128), lambda i, j: (j, 0)),
                  pl.BlockSpec((tm, 128), lambda i, j: (j, 0))],
        out_specs=[pl.BlockSpec((1, MLA_HEADS, tm, MLA_KEY_DIM), lambda i, j: (i, 0, j, 0)),
                   pl.BlockSpec((1, tm, MLA_KV_LORA), lambda i, j: (i, j, 0)),
                   pl.BlockSpec((1, MLA_ROPE, tm), lambda i, j: (i, 0, j)),
                   pl.BlockSpec((1, tm, MLA_KEY_DIM), lambda i, j: (i, j, 0)),
                   pl.BlockSpec((1, 1, MLA_KV_LORA, tm), lambda i, j: (i, j, 0, 0))],
        out_shape=[jax.ShapeDtypeStruct((nb, MLA_HEADS, t, MLA_KEY_DIM), BF16),
                   jax.ShapeDtypeStruct((nb, t, MLA_KV_LORA), F32),
                   jax.ShapeDtypeStruct((nb, MLA_ROPE, t), F32),
                   jax.ShapeDtypeStruct((nb, t, MLA_KEY_DIM), BF16),
                   jax.ShapeDtypeStruct((nb, t // tm, MLA_KV_LORA, tm), BF16)],
        compiler_params=_params(("parallel", "parallel"), 40),
        name="mla_q",
    )(x, gain, wd_ext, g_q, g_kv, wuq_ext, wuk_t, cos2, sin2)


def _mla_attn_prompt_kernel(q_ref, k_ref, vt_ref, o_ref, m_ref, l_ref, acc_ref, sa_ref, sb_ref,
                            *, tq, n_tiles):
    tk = 2 * tq
    cols = MLA_HEADS * tq
    score_refs = (sa_ref, sb_ref)

    def scores(step, dst_ref):
        kb, n_keys, first_masked_key = step
        q = q_ref[0].reshape(cols, MLA_KEY_DIM)
        st = _dot_nt(k_ref[0, kb * tk:kb * tk + n_keys, :], q)
        if first_masked_key is not None:
            key_idx = lax.broadcasted_iota(jnp.int32, (n_keys, cols), 0) - first_masked_key
            qry_idx = lax.broadcasted_iota(jnp.int32, (n_keys, cols), 1) & (tq - 1)
            st = jnp.where(key_idx <= qry_idx, st, -jnp.inf)
        dst_ref[0:n_keys, :] = st

    def accumulate(step, src_ref):
        kb, n_keys, _ = step
        st = src_ref[0:n_keys, :]
        m_old = m_ref[...]
        m_new = jnp.maximum(m_old, jnp.max(st, axis=0, keepdims=True))
        p = jnp.exp(st - m_new)
        alpha = jnp.exp(m_old - m_new)
        l_ref[...] = alpha * l_ref[...] + jnp.sum(p, axis=0, keepdims=True)
        acc_ref[...] = alpha * acc_ref[...] + _dot(vt_ref[0, kb, :, :n_keys], p.astype(BF16))
        m_ref[...] = m_new

    def query_tile(qi):
        steps = [(kb, tk, None) for kb in range(qi // 2)]
        steps.append((qi // 2, tk, tq) if qi % 2 else (qi // 2, tq, 0))
        m_ref[...] = jnp.full_like(m_ref, -jnp.inf)
        l_ref[...] = jnp.zeros_like(l_ref)
        acc_ref[...] = jnp.zeros_like(acc_ref)
        scores(steps[0], score_refs[0])
        for i, step in enumerate(steps):
            if i + 1 < len(steps):
                scores(steps[i + 1], score_refs[(i + 1) % 2])
            accumulate(step, score_refs[i % 2])
        o = acc_ref[...] / l_ref[...]
        o_ref[0] = o.T.reshape(MLA_HEADS, tq, MLA_KV_LORA).astype(BF16)

    for qi in range(n_tiles):
        pl.when(pl.program_id(1) == qi)(functools.partial(query_tile, qi))


def _mla_attn_prompt(q, kext, vt):
    b, _, t, _ = q.shape
    tq = vt.shape[-1] // 2
    assert t % (2 * tq) == 0 and tq & (tq - 1) == 0
    return pl.pallas_call(
        functools.partial(_mla_attn_prompt_kernel, tq=tq, n_tiles=t // tq),
        grid=(b, t // tq),
        in_specs=[pl.BlockSpec((1, MLA_HEADS, tq, MLA_KEY_DIM), lambda i, j: (i, 0, j, 0)),
                  pl.BlockSpec((1, t, MLA_KEY_DIM), lambda i, j: (i, 0, 0)),
                  pl.BlockSpec((1, t // (2 * tq), MLA_KV_LORA, 2 * tq), lambda i, j: (i, 0, 0, 0))],
        out_specs=pl.BlockSpec((1, MLA_HEADS, tq, MLA_KV_LORA), lambda i, j: (i, 0, j, 0)),
        out_shape=jax.ShapeDtypeStruct((b, MLA_HEADS, t, MLA_KV_LORA), BF16),
        scratch_shapes=[pltpu.VMEM((1, MLA_HEADS * tq), F32), pltpu.VMEM((1, MLA_HEADS * tq), F32),
                        pltpu.VMEM((MLA_KV_LORA, MLA_HEADS * tq), F32),
                        pltpu.VMEM((2 * tq, MLA_HEADS * tq), F32),
                        pltpu.VMEM((2 * tq, MLA_HEADS * tq), F32)],
        compiler_params=_params(("parallel", "arbitrary"), 40),
        name="mla_attn_prompt",
    )(q, kext, vt)


def _mla_attn_sample_kernel(pt_ref, q_ref, knew_ref, lat_hbm, krt_hbm, o_ref,
                            cbuf, rbuf, s_ref, sems, *, t_new, n_pages):
    g = pl.program_id(0)
    slot = g % 2
    n_rows = q_ref.shape[0]
    n_keys = n_pages * PAGE_SIZE
    n_chunks = n_keys // MLA_KEY_CHUNK

    def page_copies(group, slot_):
        copies = []
        for r in range(n_rows):
            for p in range(n_pages):
                page = pt_ref[(group * n_rows + r) * n_pages + p]
                keys = pl.ds(p * PAGE_SIZE, PAGE_SIZE)
                copies.append(pltpu.make_async_copy(lat_hbm.at[page], cbuf.at[slot_, r, keys, :],
                                                    sems.at[0, slot_]))
                copies.append(pltpu.make_async_copy(krt_hbm.at[page], rbuf.at[slot_, r, :, keys],
                                                    sems.at[1, slot_]))
        return copies

    @pl.when(g == 0)
    def _():
        for cp in page_copies(g, slot):
            cp.start()

    @pl.when(g + 1 < pl.num_programs(0))
    def _():
        for cp in page_copies(g + 1, 1 - slot):
            cp.start()

    for cp in page_copies(g, slot):
        cp.wait()

    def past_scores(r, j):
        keys = slice(j * MLA_KEY_CHUNK, (j + 1) * MLA_KEY_CHUNK)
        q = q_ref[r]
        c = cbuf[slot, r, keys, :].astype(BF16)
        rt = rbuf[slot, r, :, keys].astype(BF16)
        s_ref[r, :, keys] = (_dot_nt(q[:, :MLA_KV_LORA], c)
                             + _dot(q[:, MLA_KV_LORA:MLA_KV_LORA + MLA_ROPE], rt))

    for j in range(n_chunks):
        past_scores(0, j)
    for r in range(n_rows):
        kn = knew_ref[r]
        sn = _dot_nt(q_ref[r], kn)
        row_t = lax.broadcasted_iota(jnp.int32, sn.shape, 0) % t_new
        key = lax.broadcasted_iota(jnp.int32, sn.shape, 1)
        s_ref[r, :, n_keys:] = jnp.where((key < t_new) & (key <= row_t), sn, -jnp.inf)
        s = s_ref[r]
        e = jnp.exp(s - jnp.max(s, axis=-1, keepdims=True))
        denom = jnp.sum(e, axis=-1, keepdims=True)
        eb = e.astype(BF16)
        acc = _dot(eb[:, n_keys:], kn[:, :MLA_KV_LORA])
        for j in range(n_chunks):
            keys = slice(j * MLA_KEY_CHUNK, (j + 1) * MLA_KEY_CHUNK)
            acc = acc + _dot(eb[:, keys], cbuf[slot, r, keys, :].astype(BF16))
            if r + 1 < n_rows:
                past_scores(r + 1, j)
        o_ref[r] = acc / denom


def _mla_attn_sample(page_table, q, cache_lat, cache_krt, knew, t_new):
    b, rows, _ = q.shape
    n_pages = page_table.shape[1]
    n_keys = n_pages * PAGE_SIZE
    nr = MLA_SAMPLE_ROWS_PER_STEP
    assert n_keys % MLA_KEY_CHUNK == 0 and b % nr == 0
    grid_spec = pltpu.PrefetchScalarGridSpec(
        num_scalar_prefetch=1,
        grid=(b // nr,),
        in_specs=[pl.BlockSpec((nr, rows, MLA_KEY_DIM), lambda i, pt: (i, 0, 0)),
                  pl.BlockSpec((nr, PAGE_SIZE, MLA_KEY_DIM), lambda i, pt: (i, 0, 0)),
                  pl.BlockSpec(memory_space=pl.ANY),
                  pl.BlockSpec(memory_space=pl.ANY)],
        out_specs=pl.BlockSpec((nr, rows, MLA_KV_LORA), lambda i, pt: (i, 0, 0)),
        scratch_shapes=[pltpu.VMEM((2, nr, n_keys, MLA_KV_LORA), F32),
                        pltpu.VMEM((2, nr, MLA_ROPE, n_keys), F32),
                        pltpu.VMEM((nr, rows, n_keys + PAGE_SIZE), F32),
                        pltpu.SemaphoreType.DMA((2, 2))],
    )
    return pl.pallas_call(
        functools.partial(_mla_attn_sample_kernel, t_new=t_new, n_pages=n_pages),
        grid_spec=grid_spec,
        out_shape=jax.ShapeDtypeStruct((b, rows, MLA_KV_LORA), F32),
        compiler_params=_params(("arbitrary",), 56),
        name="mla_attn_sample",
    )(page_table.reshape(-1), q, knew, cache_lat, cache_krt)


def _mla_out_kernel(o_ref, wuv_ref, wout_ref, x_ref, y_ref, cat_ref):
    for hh in range(MLA_HEADS):
        cat_ref[:, hh * MLA_V:(hh + 1) * MLA_V] = _dot(o_ref[0, hh], wuv_ref[hh]).astype(BF16)
    y_ref[0] = x_ref[0] + _dot(cat_ref[...], wout_ref[...])


def _mla_out(o_lat, wuv, w_out, x, tm):
    nb, t, _ = x.shape
    assert t % tm == 0
    return pl.pallas_call(
        _mla_out_kernel,
        grid=(nb, t // tm),
        in_specs=[pl.BlockSpec((1, MLA_HEADS, tm, MLA_KV_LORA), lambda i, j: (i, 0, j, 0)),
                  _resident(wuv.shape), _resident(w_out.shape),
                  pl.BlockSpec((1, tm, D_MODEL), lambda i, j: (i, j, 0))],
        out_specs=pl.BlockSpec((1, tm, D_MODEL), lambda i, j: (i, j, 0)),
        out_shape=jax.ShapeDtypeStruct(x.shape, F32),
        scratch_shapes=[pltpu.VMEM((tm, MLA_HEADS * MLA_V), BF16)],
        compiler_params=_params(("parallel", "parallel"), 32),
        name="mla_out",
    )(o_lat, wuv, w_out, x)


def _pool_prompt_kernel(x_ref, xp_ref, g_ref, w_ref, sc_ref, o_ref, hl_ref, ext_ref, tmp_ref):
    j = pl.program_id(1)
    x = x_ref[0]
    tm = x.shape[0]
    h = _rms(x, g_ref[...])
    hp = _rms(xp_ref[0], g_ref[...])
    ext_ref[0:POOL_PREV, :] = jnp.where(j == 0, 0.0, hp)
    ext_ref[POOL_PREV:, :] = h
    tmp_ref[0:POOL_ALIGN, :] = jnp.zeros((POOL_ALIGN, D_MODEL), F32)
    n = POOL_PREV + tm - POOL_ALIGN
    pos = (j * tm + lax.broadcasted_iota(jnp.int32, (tm, 1), 0)).astype(F32)
    outs = []
    for gi, w in enumerate(POOL_WINDOWS):
        lanes = slice(gi * POOL_GROUP_DIM, (gi + 1) * POOL_GROUP_DIM)
        src, dst = ext_ref, tmp_ref
        shift = 1
        while shift < w:
            dst[POOL_ALIGN:, lanes] = (src[POOL_ALIGN:, lanes]
                                       + src[POOL_ALIGN - shift:POOL_ALIGN - shift + n, lanes])
            src, dst = dst, src
            shift *= 2
        pooled = src[POOL_PREV:, lanes] / jnp.minimum(pos + 1.0, float(w)) - h[:, lanes]
        outs.append(_dot(pooled.astype(BF16), w_ref[gi]))
    o_ref[0] = x + jnp.concatenate(outs, axis=-1) * sc_ref[...]

    @pl.when(j == pl.num_programs(1) - 1)
    def _():
        hl_ref[0] = h[tm - POOL_TAIL:]


def _pool_prompt(x, gain, w_pool, scale):
    b, t, _ = x.shape
    tm = 512
    assert t % tm == 0 and tm % POOL_PREV == 0
    per_tile = tm // POOL_PREV
    return pl.pallas_call(
        _pool_prompt_kernel,
        grid=(b, t // tm),
        in_specs=[pl.BlockSpec((1, tm, D_MODEL), lambda i, j: (i, j, 0)),
                  pl.BlockSpec((1, POOL_PREV, D_MODEL),
                               lambda i, j: (i, jnp.maximum(j * per_tile - 1, 0), 0)),
                  _resident(gain.shape), _resident(w_pool.shape), _resident(scale.shape)],
        out_specs=[pl.BlockSpec((1, tm, D_MODEL), lambda i, j: (i, j, 0)),
                   pl.BlockSpec((1, POOL_TAIL, D_MODEL), lambda i, j: (i, 0, 0))],
        out_shape=[jax.ShapeDtypeStruct(x.shape, F32),
                   jax.ShapeDtypeStruct((b, POOL_TAIL, D_MODEL), F32)],
        scratch_shapes=[pltpu.VMEM((POOL_PREV + tm, D_MODEL), F32),
                        pltpu.VMEM((POOL_PREV + tm, D_MODEL), F32)],
        compiler_params=_params(("parallel", "arbitrary"), 32),
        name="pool_prompt",
    )(x, x, gain, w_pool, scale)


def _pool_sample_kernel(x_ref, pre_ref, g_ref, w_ref, sc_ref, o_ref, h_ref, *, pos0):
    t_len = x_ref.shape[0]
    hs = [_rms(x_ref[t], g_ref[...]) for t in range(t_len)]
    for t in range(t_len):
        h_ref[t] = hs[t]

    def ext(e, lanes):
        return pre_ref[e, :, lanes] if e < POOL_BUF else hs[e - POOL_BUF][:, lanes]

    for t in range(t_len):
        outs = []
        for gi, w in enumerate(POOL_WINDOWS):
            lanes = slice(gi * POOL_GROUP_DIM, (gi + 1) * POOL_GROUP_DIM)
            acc = hs[t][:, lanes]
            for k in range(1, w):
                acc = acc + ext(POOL_BUF + t - k, lanes)
            pooled = acc / min(pos0 + t + 1.0, float(w)) - hs[t][:, lanes]
            outs.append(_dot(pooled.astype(BF16), w_ref[gi]))
        o_ref[t] = x_ref[t] + jnp.concatenate(outs, axis=-1) * sc_ref[...]


def _pool_sample(x_t, prefix_t, gain, w_pool, scale, pos0):
    vmem = pl.BlockSpec(memory_space=pltpu.VMEM)
    return pl.pallas_call(
        functools.partial(_pool_sample_kernel, pos0=pos0),
        in_specs=[vmem] * 5,
        out_specs=[vmem, vmem],
        out_shape=[jax.ShapeDtypeStruct(x_t.shape, F32), jax.ShapeDtypeStruct(x_t.shape, F32)],
        compiler_params=_params(None, 32),
        name="pool_sample",
    )(x_t, prefix_t, gain, w_pool, scale)


def _rope_tables(pos, half):
    inv = ROPE_BASE ** (-jnp.arange(half, dtype=F32) / half)
    ang = pos.astype(F32)[:, None] * inv[None, :]
    return jnp.cos(ang), jnp.sin(ang)


def _mla_rope_tables(pos):
    cos, sin = _rope_tables(pos, MLA_ROPE // 2)
    zeros = jnp.zeros((pos.shape[0], 128 - MLA_ROPE), F32)
    return (jnp.concatenate([cos, cos, zeros], axis=-1),
            jnp.concatenate([-sin, sin, zeros], axis=-1))


def _swap_halves(w):
    half = w.shape[-1] // 2
    return jnp.concatenate([w[..., half:], w[..., :half]], axis=-1)


def _pad_lanes(w, width):
    return jnp.pad(w, [(0, 0)] * (w.ndim - 1) + [(0, width - w.shape[-1])])


def _row(v):
    return v.reshape(1, -1).astype(F32)


def kernel(x_prompt, x_sample, state_ret, cache_mla_latent, cache_mla_krope, page_table, state_pool,
           norm_mix, norm_ffn, norm_final,
           ret_w_in, ret_gn_gain, ret_w_out,
           cm_w_in, cm_ln_gain, cm_w_spatial, cm_b_spatial, cm_w_out,
           mla_w_down, mla_q_norm, mla_kv_norm, mla_w_uq, mla_w_uk, mla_w_uv, mla_w_out,
           pool_w, pool_scale,
           ffn_w_gate_up, ffn_w_down):
    bp, tp, _ = x_prompt.shape
    bs, ts, _ = x_sample.shape
    xp = x_prompt
    xs = x_sample.reshape(bs * ts, D_MODEL)
    pos_p = jnp.arange(tp)
    pos_s = PAST_LEN + jnp.arange(ts)

    def ffn(xp_, xs_, i, final):
        yp, ys = _ffn(xp_.reshape(-1, D_MODEL), xs_, _row(norm_ffn[i]), ffn_w_gate_up, ffn_w_down, i,
                      _row(norm_final), final)
        return yp.reshape(xp_.shape), ys

    log_g = jnp.log1p(-jnp.exp2(-5.0 - jnp.arange(RET_HEADS, dtype=F32)))
    w_in = ret_w_in[0]
    w_out = ret_w_out[0]
    gn_gain = _row(ret_gn_gain[0])
    cos_p, sin_p = _rope_tables(pos_p, RET_QK_DIM // 2)
    cos_s, sin_s = _rope_tables(pos_s, RET_QK_DIM // 2)
    xp, ret_state_p = _ret_prompt(xp, _row(norm_mix[0]), w_in, cos_p, sin_p,
                                  _ret_tables(float(RET_PROMPT_CHUNK), RET_PROMPT_CHUNK, log_g),
                                  gn_gain, w_out)
    proj_s = _norm_matmul(xs, _row(norm_mix[0]), w_in, 1024)
    y_s, ret_state_s = _ret_sample(proj_s, state_ret[0], cos_s, sin_s,
                                   _ret_tables(float(ts), RET_CHUNK, log_g), gn_gain)
    xs = _matmul_residual(y_s, w_out, xs)
    xp, xs = ffn(xp, xs, 0, False)

    cw_in = cm_w_in[0]
    cw_out = cm_w_out[0]
    bias_full = jnp.repeat(jnp.transpose(cm_b_spatial[0]), CM_GROUP_DIM, axis=1)
    xp2, cm_v_p = _cm_prompt(xp.reshape(bp * tp, D_MODEL), _row(norm_mix[1]), cw_in,
                             _row(cm_ln_gain[0]), cm_w_spatial[0], bias_full, cw_out, tp)
    xp = xp2.reshape(bp, tp, D_MODEL)
    xs_t, cm_v_s_t = _cm_sample(xs.reshape(bs, ts, D_MODEL).transpose(1, 0, 2), _row(norm_mix[1]), cw_in,
                                _row(cm_ln_gain[0]),
                                cm_w_spatial[0][:, :ts, :ts].reshape(CM_GROUPS, ts * ts),
                                cm_b_spatial[0][:, :ts], cw_out)
    xs = xs_t.transpose(1, 0, 2).reshape(bs * ts, D_MODEL)
    cm_v_s = cm_v_s_t.transpose(1, 0, 2)
    xp, xs = ffn(xp, xs, 1, False)

    wd = mla_w_down[0]
    kr_w = wd[:, MLA_Q_LORA + MLA_KV_LORA:]
    wd_ext = jnp.concatenate([wd[:, :MLA_Q_LORA + MLA_KV_LORA], _pad_lanes(kr_w, 128),
                              _pad_lanes(_swap_halves(kr_w), 128)], axis=-1).astype(BF16)
    wuq = mla_w_uq[0].reshape(MLA_Q_LORA, MLA_HEADS, MLA_NOPE + MLA_ROPE)
    wuq_rope = wuq[:, :, MLA_NOPE:]
    wuq_ext = jnp.concatenate([
        wuq[:, :, :MLA_NOPE].reshape(MLA_Q_LORA, -1),
        _pad_lanes(wuq_rope, 128).reshape(MLA_Q_LORA, -1),
        _pad_lanes(_swap_halves(wuq_rope), 128).reshape(MLA_Q_LORA, -1)], axis=-1).astype(BF16)
    wuk_t = mla_w_uk[0].transpose(1, 2, 0).astype(BF16)
    wuv = mla_w_uv[0].transpose(1, 0, 2).astype(BF16)
    mw_out = mla_w_out[0].astype(BF16)
    cos2_p, sin2_p = _mla_rope_tables(pos_p)
    cos2_s, sin2_s = _mla_rope_tables(jnp.tile(pos_s, bs))
    gq, gkv = _row(mla_q_norm[0]), _row(mla_kv_norm[0])
    q_p, lat_p, krt_p, kext_p, vt_p = _mla_q(xp, _row(norm_mix[2]), wd_ext, gq, gkv, wuq_ext, wuk_t,
                                             cos2_p, sin2_p, 2 * MLA_ATTN_TQ)
    kr_p = jnp.swapaxes(krt_p, 1, 2)
    o_p = _mla_attn_prompt(q_p, kext_p, vt_p)
    xp = _mla_out(o_p, wuv, mw_out, xp, 512)
    ns = bs * ts
    q_s, lat_s, krt_s, kext_s, _ = _mla_q(xs.reshape(1, ns, D_MODEL), _row(norm_mix[2]), wd_ext, gq, gkv,
                                          wuq_ext, wuk_t, cos2_s, sin2_s, ns)
    q_s = q_s.reshape(MLA_HEADS, bs, ts, MLA_KEY_DIM).transpose(1, 0, 2, 3).reshape(bs, MLA_HEADS * ts, MLA_KEY_DIM)
    knew = jnp.pad(kext_s.reshape(bs, ts, MLA_KEY_DIM), ((0, 0), (0, PAGE_SIZE - ts), (0, 0)))
    o_s = _mla_attn_sample(page_table, q_s, cache_mla_latent[0],
                           jnp.swapaxes(cache_mla_krope[0], 1, 2), knew, ts)
    o_s = o_s.reshape(bs, MLA_HEADS, ts, MLA_KV_LORA).transpose(1, 0, 2, 3).reshape(1, MLA_HEADS, ns, MLA_KV_LORA)
    xs = _mla_out(o_s.astype(BF16), wuv, mw_out, xs.reshape(1, ns, D_MODEL), ns).reshape(ns, D_MODEL)
    lat_s = lat_s.reshape(bs, ts, MLA_KV_LORA)
    kr_s = jnp.swapaxes(krt_s, 1, 2).reshape(bs, ts, MLA_ROPE)
    xp, xs = ffn(xp, xs, 2, False)

    pw = pool_w[0].astype(BF16)
    xp, h_last = _pool_prompt(xp, _row(norm_mix[3]), pw, _row(pool_scale[0]))
    pool_state_p = h_last[:, POOL_TAIL - POOL_BUF:]
    xs_t, h_s_t = _pool_sample(xs.reshape(bs, ts, D_MODEL).transpose(1, 0, 2),
                               state_pool[0].transpose(1, 0, 2), _row(norm_mix[3]), pw,
                               _row(pool_scale[0]), float(PAST_LEN))
    xs = xs_t.transpose(1, 0, 2).reshape(bs * ts, D_MODEL)
    pool_state_s = jnp.concatenate([state_pool[0], h_s_t.transpose(1, 0, 2)], axis=1)[:, -POOL_BUF:]
    xp, xs = ffn(xp, xs, 3, True)

    return (xp, xs.reshape(bs, ts, D_MODEL),
            ret_state_p[None], ret_state_s[None],
            cm_v_p.reshape(bp, CM_CHUNK, D_MODEL)[None], cm_v_s[None],
            lat_p[None], kr_p[None], lat_s[None], kr_s[None],
            pool_state_p[None], pool_state_s[None])
```

```python
import functools

import jax
import jax.numpy as jnp
from jax import lax
from jax.experimental import pallas as pl
from jax.experimental.pallas import tpu as pltpu

F32 = jnp.float32
BF16 = jnp.bfloat16

D_MODEL = 1024
NORM_EPS = 1e-6
ROPE_BASE = 10000.0
PAST_LEN = 8192
PAGE_SIZE = 128

RET_HEADS = 4
RET_QK_DIM = 256
RET_V_DIM = 512
RET_CHUNK = 128
RET_PROMPT_CHUNK = 256
RET_STAGE_WIDTH = 512
RET_SAMPLE_ROWS_PER_STEP = 4

CM_CHUNK = 128
CM_GROUPS = 4
CM_GROUP_DIM = 256

MLA_HEADS = 8
MLA_Q_LORA = 384
MLA_KV_LORA = 256
MLA_NOPE = 128
MLA_ROPE = 64
MLA_V = 128
MLA_SCALE = (MLA_NOPE + MLA_ROPE) ** -0.5
MLA_KEY_DIM = MLA_KV_LORA + 128
MLA_KEY_CHUNK = 1024
MLA_SAMPLE_ROWS_PER_STEP = 2
MLA_ATTN_TQ = 256

POOL_WINDOWS = (2, 4, 8, 16)
POOL_GROUP_DIM = 256
POOL_BUF = 15
POOL_PREV = 32
POOL_ALIGN = 8
POOL_TAIL = 16

FFN_HIDDEN = 2816
FFN_CHUNK = 256

V7X_VMEM_BYTES = 64 * 1024 * 1024
MIB = 1024 * 1024


def _params(semantics, vmem_mib):
    assert vmem_mib * MIB < V7X_VMEM_BYTES
    return pltpu.CompilerParams(dimension_semantics=semantics,
                                vmem_limit_bytes=vmem_mib * MIB)


def _resident(shape):
    nd = len(shape)
    return pl.BlockSpec(shape, lambda *_: (0,) * nd, pipeline_mode=pl.Buffered(1))


def _rms(x, g):
    return x * lax.rsqrt(jnp.mean(x * x, axis=-1, keepdims=True) + NORM_EPS) * g


def _dot(a, b):
    return jnp.dot(a, b, preferred_element_type=F32)


def _wdot(a, w):
    return jnp.dot(a, w.astype(BF16), preferred_element_type=F32)


def _dot_nt(a, b):
    return lax.dot_general(a, b, (((1,), (1,)), ((), ())), preferred_element_type=F32)


def _ffn_kernel(xp_ref, xs_ref, g_ref, wgu_hbm, wd_hbm, gf_ref, op_ref, os_ref,
                wgu_ref, wd_ref, sems, *, final, layer):
    n_chunks = FFN_HIDDEN // FFN_CHUNK

    def chunk_copies(j):
        cols = pl.ds(j * FFN_CHUNK, FFN_CHUNK)
        ucols = pl.ds(FFN_HIDDEN + j * FFN_CHUNK, FFN_CHUNK)
        return (pltpu.make_async_copy(wgu_hbm.at[layer, :, cols], wgu_ref.at[:, cols], sems.at[0, j]),
                pltpu.make_async_copy(wgu_hbm.at[layer, :, ucols], wgu_ref.at[:, ucols], sems.at[1, j]),
                pltpu.make_async_copy(wd_hbm.at[layer, cols, :], wd_ref.at[cols, :], sems.at[2, j]))

    def tile(x_ref, o_ref, wait_for_weights):
        x = x_ref[...]
        h = _rms(x, g_ref[...]).astype(BF16)
        acc = x
        for j in range(n_chunks):
            if wait_for_weights:
                for cp in chunk_copies(j):
                    cp.wait()
            cols = slice(j * FFN_CHUNK, (j + 1) * FFN_CHUNK)
            ucols = slice(FFN_HIDDEN + j * FFN_CHUNK, FFN_HIDDEN + (j + 1) * FFN_CHUNK)
            g = _dot(h, wgu_ref[:, cols].astype(BF16))
            u = _dot(h, wgu_ref[:, ucols].astype(BF16))
            a = (g * jax.nn.sigmoid(g)) * u
            acc = acc + _dot(a.astype(BF16), wd_ref[cols, :].astype(BF16))
        o_ref[...] = _rms(acc, gf_ref[...]) if final else acc

    i = pl.program_id(0)
    is_prompt = i < pl.num_programs(0) - 1

    @pl.when(i == 0)
    def _():
        for j in range(n_chunks):
            for cp in chunk_copies(j):
                cp.start()
        tile(xp_ref, op_ref, True)

    @pl.when((i > 0) & is_prompt)
    def _():
        tile(xp_ref, op_ref, False)

    @pl.when(jnp.logical_not(is_prompt))
    def _():
        tile(xs_ref, os_ref, False)


def _ffn(xp, xs, gain, wgu_all, wd_all, layer, final_gain, final):
    tm = xs.shape[0]
    assert xp.shape[0] % tm == 0 and xp.shape[0] >= 2 * tm
    n_prompt = xp.shape[0] // tm
    hbm = pl.BlockSpec(memory_space=pl.ANY)
    prompt_tile = pl.BlockSpec((tm, D_MODEL), lambda i: (jnp.minimum(i, n_prompt - 1), 0))
    sample_tile = pl.BlockSpec((tm, D_MODEL), lambda i: (0, 0))
    return pl.pallas_call(
        functools.partial(_ffn_kernel, final=final, layer=layer),
        grid=(n_prompt + 1,),
        in_specs=[prompt_tile, sample_tile, _resident(gain.shape), hbm, hbm,
                  _resident(final_gain.shape)],
        out_specs=[prompt_tile, sample_tile],
        out_shape=[jax.ShapeDtypeStruct(xp.shape, F32), jax.ShapeDtypeStruct(xs.shape, F32)],
        scratch_shapes=[pltpu.VMEM(wgu_all.shape[1:], F32), pltpu.VMEM(wd_all.shape[1:], F32),
                        pltpu.SemaphoreType.DMA((3, FFN_HIDDEN // FFN_CHUNK))],
        compiler_params=_params(("arbitrary",), 58),
        name="ffn",
    )(xp, xs, gain, wgu_all, wd_all, final_gain)


def _norm_matmul_kernel(x_ref, g_ref, w_ref, o_ref):
    h = _rms(x_ref[...], g_ref[...]).astype(BF16)
    o_ref[...] = _wdot(h, w_ref[...])


def _norm_matmul(x, gain, w, tn):
    n, dout = x.shape[0], w.shape[1]
    assert dout % tn == 0
    return pl.pallas_call(
        _norm_matmul_kernel,
        grid=(dout // tn,),
        in_specs=[pl.BlockSpec((n, D_MODEL), lambda j: (0, 0)),
                  pl.BlockSpec(gain.shape, lambda j: (0, 0)),
                  pl.BlockSpec((D_MODEL, tn), lambda j: (0, j))],
        out_specs=pl.BlockSpec((n, tn), lambda j: (0, j)),
        out_shape=jax.ShapeDtypeStruct((n, dout), F32),
        compiler_params=_params(("parallel",), 32),
        name="norm_matmul",
    )(x, gain, w)


def _matmul_residual_kernel(y_ref, w_ref, x_ref, o_ref):
    o_ref[...] = x_ref[...] + _wdot(y_ref[...].astype(BF16), w_ref[...])


def _matmul_residual(y, w, x):
    n = x.shape[0]
    tm = min(n, 512)
    assert n % tm == 0
    return pl.pallas_call(
        _matmul_residual_kernel,
        grid=(n // tm,),
        in_specs=[pl.BlockSpec((tm, y.shape[1]), lambda i: (i, 0)),
                  _resident(w.shape),
                  pl.BlockSpec((tm, D_MODEL), lambda i: (i, 0))],
        out_specs=pl.BlockSpec((tm, D_MODEL), lambda i: (i, 0)),
        out_shape=jax.ShapeDtypeStruct(x.shape, F32),
        compiler_params=_params(("parallel",), 32),
        name="matmul_residual",
    )(y, w, x)


def _rope_halves(x, cos, sin):
    half = cos.shape[-1]
    x1, x2 = x[:, :half], x[:, half:]
    return jnp.concatenate([x1 * cos - x2 * sin, x1 * sin + x2 * cos], axis=-1)


def _ret_chunk(q, k, v, s, dmat, qdec, kdec, sdec):
    vb = v.astype(BF16)
    scores = _dot_nt(q.astype(BF16), k.astype(BF16)) * dmat
    o = _dot(scores.astype(BF16), vb) + _dot((q * qdec).astype(BF16), s.astype(BF16))
    s_new = sdec * s + _dot((k * kdec).T.astype(BF16), vb)
    return o, s_new


def _group_norm_gate(o, gate, gain):
    mu = jnp.mean(o, axis=-1, keepdims=True)
    oc = o - mu
    var = jnp.mean(oc * oc, axis=-1, keepdims=True)
    on = oc * lax.rsqrt(var + NORM_EPS)
    return (gate * jax.nn.sigmoid(gate)) * on * gain


def _ret_prompt_kernel(x_ref, g_ref, win_hbm, cos_ref, sin_ref, dmat_ref, qdec_ref,
                       kdec_ref, sdec_ref, gng_ref, wout_hbm, o_ref, st_ref,
                       s_ref, y_ref, win_ref, wout_ref, stage_in, stage_out, sems):
    j = pl.program_id(1)

    def load_weight(w_hbm, w_ref, stage, sem_row, axis):
        width = stage.shape[1 + axis]
        n = w_hbm.shape[axis] // width

        def window(ref, c):
            return ref.at[pl.ds(c * width, width), :] if axis == 0 else ref.at[:, pl.ds(c * width, width)]

        def copy(c):
            return pltpu.make_async_copy(window(w_hbm, c), stage.at[c % 2], sems.at[sem_row, c % 2])

        copy(0).start()
        for c in range(n):
            if c + 1 < n:
                copy(c + 1).start()
            copy(c).wait()
            window(w_ref, c)[...] = stage[c % 2].astype(BF16)

    @pl.when((pl.program_id(0) == 0) & (j == 0))
    def _():
        load_weight(win_hbm, win_ref, stage_in, 0, 1)
        load_weight(wout_hbm, wout_ref, stage_out, 1, 0)

    @pl.when(j == 0)
    def _():
        s_ref[...] = jnp.zeros_like(s_ref)

    x = x_ref[0]
    tm = x.shape[0]
    h = _rms(x, g_ref[...]).astype(BF16)
    for hh in range(RET_HEADS):
        qs = slice(hh * RET_QK_DIM, (hh + 1) * RET_QK_DIM)
        ks = slice(D_MODEL + hh * RET_QK_DIM, D_MODEL + (hh + 1) * RET_QK_DIM)
        vs = slice(2 * D_MODEL + hh * RET_V_DIM, 2 * D_MODEL + (hh + 1) * RET_V_DIM)
        gs = slice(4 * D_MODEL + hh * RET_V_DIM, 4 * D_MODEL + (hh + 1) * RET_V_DIM)
        q = _rope_halves(_wdot(h, win_ref[:, qs]), cos_ref[...], sin_ref[...]) * (RET_QK_DIM ** -0.5)
        k = _rope_halves(_wdot(h, win_ref[:, ks]), cos_ref[...], sin_ref[...])
        v = _wdot(h, win_ref[:, vs])
        gate = _wdot(h, win_ref[:, gs])
        gain = gng_ref[:, hh * RET_V_DIM:(hh + 1) * RET_V_DIM]
        chunk = dmat_ref.shape[1]
        for c in range(tm // chunk):
            rows = slice(c * chunk, (c + 1) * chunk)
            o, s_new = _ret_chunk(q[rows], k[rows], v[rows], s_ref[hh], dmat_ref[hh],
                                  qdec_ref[hh], kdec_ref[hh], sdec_ref[hh])
            s_ref[hh] = s_new
            y_ref[rows, hh * RET_V_DIM:(hh + 1) * RET_V_DIM] = (
                _group_norm_gate(o, gate[rows], gain).astype(BF16))
    o_ref[0] = x + _wdot(y_ref[...], wout_ref[...])

    @pl.when(j == pl.num_programs(1) - 1)
    def _():
        st_ref[0] = s_ref[...]


def _ret_tables(length, chunk, log_g):
    idx = jnp.arange(chunk, dtype=F32)
    valid = idx < length
    rel = idx[:, None] - idx[None, :]
    ok = (rel >= 0) & valid[:, None] & valid[None, :]
    dmat = jnp.where(ok[None], jnp.exp(jnp.maximum(rel, 0.0)[None] * log_g[:, None, None]), 0.0)
    qd = jnp.where(valid[None], jnp.exp((idx + 1.0)[None, :] * log_g[:, None]), 0.0)
    kd = jnp.where(valid[None], jnp.exp((length - 1.0 - idx)[None, :] * log_g[:, None]), 0.0)
    qdec = jnp.broadcast_to(qd[:, :, None], (RET_HEADS, chunk, RET_QK_DIM))
    kdec = jnp.broadcast_to(kd[:, :, None], (RET_HEADS, chunk, RET_QK_DIM))
    sdec = jnp.broadcast_to(jnp.exp(length * log_g)[:, None, None], (RET_HEADS, 1, RET_V_DIM))
    return dmat, qdec, kdec, sdec


def _ret_prompt(x, gain, w_in, cos, sin, tables, gn_gain, w_out):
    b, t, _ = x.shape
    tm = 512
    assert t % tm == 0
    dmat, qdec, kdec, sdec = tables
    hbm = pl.BlockSpec(memory_space=pl.ANY)
    return pl.pallas_call(
        _ret_prompt_kernel,
        grid=(b, t // tm),
        in_specs=[pl.BlockSpec((1, tm, D_MODEL), lambda i, j: (i, j, 0)),
                  _resident(gain.shape), hbm,
                  pl.BlockSpec((tm, RET_QK_DIM // 2), lambda i, j: (j, 0)),
                  pl.BlockSpec((tm, RET_QK_DIM // 2), lambda i, j: (j, 0)),
                  _resident(dmat.shape), _resident(qdec.shape), _resident(kdec.shape),
                  _resident(sdec.shape), _resident(gn_gain.shape), hbm],
        out_specs=[pl.BlockSpec((1, tm, D_MODEL), lambda i, j: (i, j, 0)),
                   pl.BlockSpec((1, RET_HEADS, RET_QK_DIM, RET_V_DIM), lambda i, j: (i, 0, 0, 0))],
        out_shape=[jax.ShapeDtypeStruct(x.shape, F32),
                   jax.ShapeDtypeStruct((b, RET_HEADS, RET_QK_DIM, RET_V_DIM), F32)],
        scratch_shapes=[pltpu.VMEM((RET_HEADS, RET_QK_DIM, RET_V_DIM), F32),
                        pltpu.VMEM((tm, RET_HEADS * RET_V_DIM), BF16),
                        pltpu.VMEM(w_in.shape, BF16), pltpu.VMEM(w_out.shape, BF16),
                        pltpu.VMEM((2, w_in.shape[0], RET_STAGE_WIDTH), F32),
                        pltpu.VMEM((2, RET_STAGE_WIDTH, w_out.shape[1]), F32),
                        pltpu.SemaphoreType.DMA((2, 2))],
        compiler_params=_params(("arbitrary", "arbitrary"), 56),
        name="ret_prompt",
    )(x, gain, w_in, cos, sin, dmat, qdec, kdec, sdec, gn_gain, w_out)


def _ret_sample_kernel(p_ref, s0_ref, cos_ref, sin_ref, dmat_ref, qdec_ref, kdec_ref,
                       sdec_ref, gng_ref, y_ref, st_ref, qp_ref, kp_ref, vp_ref):
    @pl.when(pl.program_id(0) == 0)
    def _():
        qp_ref[...] = jnp.zeros_like(qp_ref)
        kp_ref[...] = jnp.zeros_like(kp_ref)
        vp_ref[...] = jnp.zeros_like(vp_ref)

    t = cos_ref.shape[0]
    for bb in range(s0_ref.shape[0]):
        rows = slice(bb * t, (bb + 1) * t)
        for hh in range(RET_HEADS):
            qs = slice(hh * RET_QK_DIM, (hh + 1) * RET_QK_DIM)
            ks = slice(D_MODEL + hh * RET_QK_DIM, D_MODEL + (hh + 1) * RET_QK_DIM)
            vs = slice(2 * D_MODEL + hh * RET_V_DIM, 2 * D_MODEL + (hh + 1) * RET_V_DIM)
            gs = slice(4 * D_MODEL + hh * RET_V_DIM, 4 * D_MODEL + (hh + 1) * RET_V_DIM)
            qp_ref[0:t, :] = (_rope_halves(p_ref[rows, qs], cos_ref[...], sin_ref[...])
                              * (RET_QK_DIM ** -0.5))
            kp_ref[0:t, :] = _rope_halves(p_ref[rows, ks], cos_ref[...], sin_ref[...])
            vp_ref[0:t, :] = p_ref[rows, vs]
            o, s_new = _ret_chunk(qp_ref[...], kp_ref[...], vp_ref[...], s0_ref[bb, hh], dmat_ref[hh],
                                  qdec_ref[hh], kdec_ref[hh], sdec_ref[hh])
            st_ref[bb, hh] = s_new
            gain = gng_ref[:, hh * RET_V_DIM:(hh + 1) * RET_V_DIM]
            y_ref[rows, hh * RET_V_DIM:(hh + 1) * RET_V_DIM] = _group_norm_gate(o[0:t], p_ref[rows, gs], gain)


def _ret_sample(proj, s0, cos, sin, tables, gn_gain):
    b = s0.shape[0]
    t = cos.shape[0]
    nb = RET_SAMPLE_ROWS_PER_STEP
    assert b % nb == 0 and proj.shape[0] == b * t
    dmat, qdec, kdec, sdec = tables
    state_spec = pl.BlockSpec((nb, RET_HEADS, RET_QK_DIM, RET_V_DIM), lambda i: (i, 0, 0, 0))
    return pl.pallas_call(
        _ret_sample_kernel,
        grid=(b // nb,),
        in_specs=[pl.BlockSpec((nb * t, proj.shape[1]), lambda i: (i, 0)),
                  state_spec,
                  _resident(cos.shape), _resident(sin.shape),
                  _resident(dmat.shape), _resident(qdec.shape), _resident(kdec.shape),
                  _resident(sdec.shape), _resident(gn_gain.shape)],
        out_specs=[pl.BlockSpec((nb * t, RET_HEADS * RET_V_DIM), lambda i: (i, 0)),
                   state_spec],
        out_shape=[jax.ShapeDtypeStruct((b * t, RET_HEADS * RET_V_DIM), F32),
                   jax.ShapeDtypeStruct(s0.shape, F32)],
        scratch_shapes=[pltpu.VMEM((RET_CHUNK, RET_QK_DIM), F32),
                        pltpu.VMEM((RET_CHUNK, RET_QK_DIM), F32),
                        pltpu.VMEM((RET_CHUNK, RET_V_DIM), F32)],
        compiler_params=_params(("arbitrary",), 48),
        name="ret_sample",
    )(proj, s0, cos, sin, dmat, qdec, kdec, sdec, gn_gain)


def _gelu_tanh(x):
    return x * (0.5 * (1.0 + jnp.tanh(0.7978845608028654 * (x + 0.044715 * (x * x * x)))))


def _layer_norm(v, gain):
    mu = jnp.mean(v, axis=-1, keepdims=True)
    vc = v - mu
    var = jnp.mean(vc * vc, axis=-1, keepdims=True)
    return vc * lax.rsqrt(var + NORM_EPS) * gain


def _cm_prompt_kernel(x_ref, g_ref, win_ref, lng_ref, ws_ref, bias_ref, wout_ref,
                      o_ref, v_ref, z_ref):
    x = x_ref[...]
    tm = x.shape[0]
    h = _rms(x, g_ref[...]).astype(BF16)
    u = _gelu_tanh(_wdot(h, win_ref[:, :D_MODEL]))
    v = _layer_norm(_gelu_tanh(_wdot(h, win_ref[:, D_MODEL:])), lng_ref[...])
    v_ref[...] = v[tm - CM_CHUNK:]
    vb = v.astype(BF16)
    row = lax.broadcasted_iota(jnp.int32, (CM_CHUNK, CM_CHUNK), 0)
    col = lax.broadcasted_iota(jnp.int32, (CM_CHUNK, CM_CHUNK), 1)
    for gi in range(CM_GROUPS):
        lanes = slice(gi * CM_GROUP_DIM, (gi + 1) * CM_GROUP_DIM)
        w = jnp.where(row >= col, ws_ref[gi], 0.0).astype(BF16)
        for c in range(tm // CM_CHUNK):
            rows = slice(c * CM_CHUNK, (c + 1) * CM_CHUNK)
            mixed = _dot(w, vb[rows, lanes]) + bias_ref[:, lanes]
            z_ref[rows, lanes] = (u[rows, lanes] * mixed).astype(BF16)
    o_ref[...] = x + _wdot(z_ref[...], wout_ref[...])


def _cm_prompt(x, gain, w_in, ln_gain, w_s, bias_full, w_out, seq):
    n = x.shape[0]
    tm = 1024
    assert seq % tm == 0 and n % seq == 0
    per_seq = seq // tm
    return pl.pallas_call(
        _cm_prompt_kernel,
        grid=(n // tm,),
        in_specs=[pl.BlockSpec((tm, D_MODEL), lambda i: (i, 0)),
                  _resident(gain.shape), _resident(w_in.shape), _resident(ln_gain.shape),
                  _resident(w_s.shape), _resident(bias_full.shape), _resident(w_out.shape)],
        out_specs=[pl.BlockSpec((tm, D_MODEL), lambda i: (i, 0)),
                   pl.BlockSpec((CM_CHUNK, D_MODEL), lambda i: (i // per_seq, 0))],
        out_shape=[jax.ShapeDtypeStruct(x.shape, F32),
                   jax.ShapeDtypeStruct((n // seq * CM_CHUNK, D_MODEL), F32)],
        scratch_shapes=[pltpu.VMEM((tm, D_MODEL), BF16)],
        compiler_params=_params(("arbitrary",), 40),
        name="cm_prompt",
    )(x, gain, w_in, ln_gain, w_s, bias_full, w_out)


def _cm_sample_kernel(ws_ref, bs_ref, x_ref, g_ref, win_ref, lng_ref, wout_ref, o_ref, v_ref):
    t_len = x_ref.shape[0]
    us, vs = [], []
    for t in range(t_len):
        h = _rms(x_ref[t], g_ref[...]).astype(BF16)
        us.append(_gelu_tanh(_wdot(h, win_ref[:, :D_MODEL])))
        v = _layer_norm(_gelu_tanh(_wdot(h, win_ref[:, D_MODEL:])), lng_ref[...])
        v_ref[t] = v
        vs.append(v)
    for t in range(t_len):
        parts = []
        for gi in range(CM_GROUPS):
            lanes = slice(gi * CM_GROUP_DIM, (gi + 1) * CM_GROUP_DIM)
            mixed = jnp.full_like(vs[t][:, lanes], bs_ref[gi, t])
            for s in range(t + 1):
                mixed = mixed + ws_ref[gi, t * t_len + s] * vs[s][:, lanes]
            parts.append(us[t][:, lanes] * mixed)
        z = jnp.concatenate(parts, axis=-1).astype(BF16)
        o_ref[t] = x_ref[t] + _wdot(z, wout_ref[...])


def _cm_sample(x_t, gain, w_in, ln_gain, w_s_small, b_s_small, w_out):
    smem = pl.BlockSpec(memory_space=pltpu.SMEM)
    vmem = pl.BlockSpec(memory_space=pltpu.VMEM)
    return pl.pallas_call(
        _cm_sample_kernel,
        in_specs=[smem, smem, vmem, vmem, vmem, vmem, vmem],
        out_specs=[vmem, vmem],
        out_shape=[jax.ShapeDtypeStruct(x_t.shape, F32), jax.ShapeDtypeStruct(x_t.shape, F32)],
        compiler_params=_params(None, 32),
        name="cm_sample",
    )(w_s_small, b_s_small, x_t, gain, w_in, ln_gain, w_out)


def _mla_q_kernel(x_ref, g_ref, wd_ref, gq_ref, gkv_ref, wuq_ref, wuk_ref, cos_ref, sin_ref,
                  q_ref, c_ref, krt_ref, kext_ref, vt_ref):
    h = _rms(x_ref[0], g_ref[...]).astype(BF16)
    d = _dot(h, wd_ref[...])
    cq = _rms(d[:, :MLA_Q_LORA], gq_ref[...]).astype(BF16)
    ckv = _rms(d[:, MLA_Q_LORA:MLA_Q_LORA + MLA_KV_LORA], gkv_ref[...])
    cos, sin = cos_ref[...], sin_ref[...]
    base = MLA_Q_LORA + MLA_KV_LORA
    krp = d[:, base:base + 128] * cos + d[:, base + 128:base + 256] * sin
    c_ref[0] = ckv
    krt_ref[0] = krp.T[:MLA_ROPE]
    kext_ref[0] = jnp.concatenate([ckv.astype(BF16), krp.astype(BF16)], axis=-1)
    vt_ref[0, 0] = ckv.T.astype(BF16)
    nope_w = MLA_HEADS * MLA_NOPE
    for hh in range(MLA_HEADS):
        qn = _dot(cq, wuq_ref[:, hh * MLA_NOPE:(hh + 1) * MLA_NOPE])
        raw = _dot(cq, wuq_ref[:, nope_w + hh * 128:nope_w + (hh + 1) * 128])
        rot = _dot(cq, wuq_ref[:, 2 * nope_w + hh * 128:2 * nope_w + (hh + 1) * 128])
        ql = _dot(qn.astype(BF16), wuk_ref[hh]) * MLA_SCALE
        qr = (raw * cos + rot * sin) * MLA_SCALE
        q_ref[0, hh] = jnp.concatenate([ql.astype(BF16), qr.astype(BF16)], axis=-1)


def _mla_q(x, gain, wd_ext, g_q, g_kv, wuq_ext, wuk_t, cos2, sin2, tm):
    nb, t, _ = x.shape
    assert t % tm == 0
    return pl.pallas_call(
        _mla_q_kernel,
        grid=(nb, t // tm),
        in_specs=[pl.BlockSpec((1, tm, D_MODEL), lambda i, j: (i, j, 0)),
                  _resident(gain.shape), _resident(wd_ext.shape), _resident(g_q.shape),
                  _resident(g_kv.shape), _resident(wuq_ext.shape), _resident(wuk_t.shape),
                  pl.BlockSpec((tm, 128), lambda i, j: (j, 0)),
                  pl.BlockSpec((tm, 128), lambda i, j: (j, 0))],
        out_specs=[pl.BlockSpec((1, MLA_HEADS, tm, MLA_KEY_DIM), lambda i, j: (i, 0, j, 0)),
                   pl.BlockSpec((1, tm, MLA_KV_LORA), lambda i, j: (i, j, 0)),
                   pl.BlockSpec((1, MLA_ROPE, tm), lambda i, j: (i, 0, j)),
                   pl.BlockSpec((1, tm, MLA_KEY_DIM), lambda i, j: (i, j, 0)),
                   pl.BlockSpec((1, 1, MLA_KV_LORA, tm), lambda i, j: (i, j, 0, 0))],
        out_shape=[jax.ShapeDtypeStruct((nb, MLA_HEADS, t, MLA_KEY_DIM), BF16),
                   jax.ShapeDtypeStruct((nb, t, MLA_KV_LORA), F32),
                   jax.ShapeDtypeStruct((nb, MLA_ROPE, t), F32),
                   jax.ShapeDtypeStruct((nb, t, MLA_KEY_DIM), BF16),
                   jax.ShapeDtypeStruct((nb, t // tm, MLA_KV_LORA, tm), BF16)],
        compiler_params=_params(("parallel", "parallel"), 40),
        name="mla_q",
    )(x, gain, wd_ext, g_q, g_kv, wuq_ext, wuk_t, cos2, sin2)


def _mla_attn_prompt_kernel(q_ref, k_ref, vt_ref, x_ref, wuv_ref, wout_ref, y_ref,
                            m_ref, l_ref, acc_ref, sa_ref, sb_ref, cat_ref, *, tq, n_tiles):
    tk = 2 * tq
    cols = MLA_HEADS * tq
    score_refs = (sa_ref, sb_ref)

    def scores(step, dst_ref):
        kb, n_keys, first_masked_key = step
        q = q_ref[0].reshape(cols, MLA_KEY_DIM)
        st = _dot_nt(k_ref[0, kb * tk:kb * tk + n_keys, :], q)
        if first_masked_key is not None:
            key_idx = lax.broadcasted_iota(jnp.int32, (n_keys, cols), 0) - first_masked_key
            qry_idx = lax.broadcasted_iota(jnp.int32, (n_keys, cols), 1) & (tq - 1)
            st = jnp.where(key_idx <= qry_idx, st, -jnp.inf)
        dst_ref[0:n_keys, :] = st

    def accumulate(step, src_ref):
        kb, n_keys, _ = step
        st = src_ref[0:n_keys, :]
        m_old = m_ref[...]
        m_new = jnp.maximum(m_old, jnp.max(st, axis=0, keepdims=True))
        p = jnp.exp(st - m_new)
        alpha = jnp.exp(m_old - m_new)
        l_ref[...] = alpha * l_ref[...] + jnp.sum(p, axis=0, keepdims=True)
        acc_ref[...] = alpha * acc_ref[...] + _dot(vt_ref[0, kb, :, :n_keys], p.astype(BF16))
        m_ref[...] = m_new

    def query_tile(qi):
        steps = [(kb, tk, None) for kb in range(qi // 2)]
        steps.append((qi // 2, tk, tq) if qi % 2 else (qi // 2, tq, 0))
        m_ref[...] = jnp.full_like(m_ref, -jnp.inf)
        l_ref[...] = jnp.zeros_like(l_ref)
        acc_ref[...] = jnp.zeros_like(acc_ref)
        scores(steps[0], score_refs[0])
        for i, step in enumerate(steps):
            if i + 1 < len(steps):
                scores(steps[i + 1], score_refs[(i + 1) % 2])
            accumulate(step, score_refs[i % 2])
        o = (acc_ref[...] / l_ref[...]).T.astype(BF16)
        for hh in range(MLA_HEADS):
            cat_ref[:, hh * MLA_V:(hh + 1) * MLA_V] = _dot(o[hh * tq:(hh + 1) * tq],
                                                           wuv_ref[hh]).astype(BF16)
        y_ref[0] = x_ref[0] + _dot(cat_ref[...], wout_ref[...])

    for qi in range(n_tiles):
        pl.when(pl.program_id(1) == qi)(functools.partial(query_tile, qi))


def _mla_attn_prompt(q, kext, vt, x, wuv, w_out):
    b, _, t, _ = q.shape
    tq = vt.shape[-1] // 2
    assert t % (2 * tq) == 0 and tq & (tq - 1) == 0
    return pl.pallas_call(
        functools.partial(_mla_attn_prompt_kernel, tq=tq, n_tiles=t // tq),
        grid=(b, t // tq),
        in_specs=[pl.BlockSpec((1, MLA_HEADS, tq, MLA_KEY_DIM), lambda i, j: (i, 0, j, 0)),
                  pl.BlockSpec((1, t, MLA_KEY_DIM), lambda i, j: (i, 0, 0)),
                  pl.BlockSpec((1, t // (2 * tq), MLA_KV_LORA, 2 * tq), lambda i, j: (i, 0, 0, 0)),
                  pl.BlockSpec((1, tq, D_MODEL), lambda i, j: (i, j, 0)),
                  _resident(wuv.shape), _resident(w_out.shape)],
        out_specs=pl.BlockSpec((1, tq, D_MODEL), lambda i, j: (i, j, 0)),
        out_shape=jax.ShapeDtypeStruct(x.shape, F32),
        scratch_shapes=[pltpu.VMEM((1, MLA_HEADS * tq), F32), pltpu.VMEM((1, MLA_HEADS * tq), F32),
                        pltpu.VMEM((MLA_KV_LORA, MLA_HEADS * tq), F32),
                        pltpu.VMEM((2 * tq, MLA_HEADS * tq), F32),
                        pltpu.VMEM((2 * tq, MLA_HEADS * tq), F32),
                        pltpu.VMEM((tq, MLA_HEADS * MLA_V), BF16)],
        compiler_params=_params(("parallel", "arbitrary"), 40),
        name="mla_attn_prompt",
    )(q, kext, vt, x, wuv, w_out)


def _mla_attn_sample_kernel(pt_ref, q_ref, knew_ref, lat_hbm, krt_hbm, o_ref,
                            cbuf, rbuf, s_ref, sems, *, t_new, n_pages):
    g = pl.program_id(0)
    slot = g % 2
    n_rows = q_ref.shape[0]
    n_keys = n_pages * PAGE_SIZE
    n_chunks = n_keys // MLA_KEY_CHUNK

    def page_copies(group, slot_):
        copies = []
        for r in range(n_rows):
            for p in range(n_pages):
                page = pt_ref[(group * n_rows + r) * n_pages + p]
                keys = pl.ds(p * PAGE_SIZE, PAGE_SIZE)
                copies.append(pltpu.make_async_copy(lat_hbm.at[page], cbuf.at[slot_, r, keys, :],
                                                    sems.at[0, slot_]))
                copies.append(pltpu.make_async_copy(krt_hbm.at[page], rbuf.at[slot_, r, :, keys],
                                                    sems.at[1, slot_]))
        return copies

    @pl.when(g == 0)
    def _():
        for cp in page_copies(g, slot):
            cp.start()

    @pl.when(g + 1 < pl.num_programs(0))
    def _():
        for cp in page_copies(g + 1, 1 - slot):
            cp.start()

    for cp in page_copies(g, slot):
        cp.wait()

    def past_scores(r, j):
        keys = slice(j * MLA_KEY_CHUNK, (j + 1) * MLA_KEY_CHUNK)
        q = q_ref[r]
        c = cbuf[slot, r, keys, :].astype(BF16)
        rt = rbuf[slot, r, :, keys].astype(BF16)
        s_ref[r, :, keys] = (_dot_nt(q[:, :MLA_KV_LORA], c)
                             + _dot(q[:, MLA_KV_LORA:MLA_KV_LORA + MLA_ROPE], rt))

    for j in range(n_chunks):
        past_scores(0, j)
    for r in range(n_rows):
        kn = knew_ref[r]
        sn = _dot_nt(q_ref[r], kn)
        row_t = lax.broadcasted_iota(jnp.int32, sn.shape, 0) % t_new
        key = lax.broadcasted_iota(jnp.int32, sn.shape, 1)
        s_ref[r, :, n_keys:] = jnp.where((key < t_new) & (key <= row_t), sn, -jnp.inf)
        s = s_ref[r]
        e = jnp.exp(s - jnp.max(s, axis=-1, keepdims=True))
        denom = jnp.sum(e, axis=-1, keepdims=True)
        eb = e.astype(BF16)
        acc = _dot(eb[:, n_keys:], kn[:, :MLA_KV_LORA])
        for j in range(n_chunks):
            keys = slice(j * MLA_KEY_CHUNK, (j + 1) * MLA_KEY_CHUNK)
            acc = acc + _dot(eb[:, keys], cbuf[slot, r, keys, :].astype(BF16))
            if r + 1 < n_rows:
                past_scores(r + 1, j)
        o_ref[r] = acc / denom


def _mla_attn_sample(page_table, q, cache_lat, cache_krt, knew, t_new):
    b, rows, _ = q.shape
    n_pages = page_table.shape[1]
    n_keys = n_pages * PAGE_SIZE
    nr = MLA_SAMPLE_ROWS_PER_STEP
    assert n_keys % MLA_KEY_CHUNK == 0 and b % nr == 0
    grid_spec = pltpu.PrefetchScalarGridSpec(
        num_scalar_prefetch=1,
        grid=(b // nr,),
        in_specs=[pl.BlockSpec((nr, rows, MLA_KEY_DIM), lambda i, pt: (i, 0, 0)),
                  pl.BlockSpec((nr, PAGE_SIZE, MLA_KEY_DIM), lambda i, pt: (i, 0, 0)),
                  pl.BlockSpec(memory_space=pl.ANY),
                  pl.BlockSpec(memory_space=pl.ANY)],
        out_specs=pl.BlockSpec((nr, rows, MLA_KV_LORA), lambda i, pt: (i, 0, 0)),
        scratch_shapes=[pltpu.VMEM((2, nr, n_keys, MLA_KV_LORA), F32),
                        pltpu.VMEM((2, nr, MLA_ROPE, n_keys), F32),
                        pltpu.VMEM((nr, rows, n_keys + PAGE_SIZE), F32),
                        pltpu.SemaphoreType.DMA((2, 2))],
    )
    return pl.pallas_call(
        functools.partial(_mla_attn_sample_kernel, t_new=t_new, n_pages=n_pages),
        grid_spec=grid_spec,
        out_shape=jax.ShapeDtypeStruct((b, rows, MLA_KV_LORA), F32),
        compiler_params=_params(("arbitrary",), 56),
        name="mla_attn_sample",
    )(page_table.reshape(-1), q, knew, cache_lat, cache_krt)


def _mla_out_kernel(o_ref, wuv_ref, wout_ref, x_ref, y_ref, cat_ref):
    for hh in range(MLA_HEADS):
        cat_ref[:, hh * MLA_V:(hh + 1) * MLA_V] = _dot(o_ref[0, hh], wuv_ref[hh]).astype(BF16)
    y_ref[0] = x_ref[0] + _dot(cat_ref[...], wout_ref[...])


def _mla_out(o_lat, wuv, w_out, x, tm):
    nb, t, _ = x.shape
    assert t % tm == 0
    return pl.pallas_call(
        _mla_out_kernel,
        grid=(nb, t // tm),
        in_specs=[pl.BlockSpec((1, MLA_HEADS, tm, MLA_KV_LORA), lambda i, j: (i, 0, j, 0)),
                  _resident(wuv.shape), _resident(w_out.shape),
                  pl.BlockSpec((1, tm, D_MODEL), lambda i, j: (i, j, 0))],
        out_specs=pl.BlockSpec((1, tm, D_MODEL), lambda i, j: (i, j, 0)),
        out_shape=jax.ShapeDtypeStruct(x.shape, F32),
        scratch_shapes=[pltpu.VMEM((tm, MLA_HEADS * MLA_V), BF16)],
        compiler_params=_params(("parallel", "parallel"), 32),
        name="mla_out",
    )(o_lat, wuv, w_out, x)


def _pool_prompt_kernel(x_ref, xp_ref, g_ref, w_ref, sc_ref, o_ref, hl_ref, ext_ref, tmp_ref):
    j = pl.program_id(1)
    x = x_ref[0]
    tm = x.shape[0]
    h = _rms(x, g_ref[...])
    hp = _rms(xp_ref[0], g_ref[...])
    ext_ref[0:POOL_PREV, :] = jnp.where(j == 0, 0.0, hp)
    ext_ref[POOL_PREV:, :] = h
    tmp_ref[0:POOL_ALIGN, :] = jnp.zeros((POOL_ALIGN, D_MODEL), F32)
    n = POOL_PREV + tm - POOL_ALIGN
    pos = (j * tm + lax.broadcasted_iota(jnp.int32, (tm, 1), 0)).astype(F32)
    outs = []
    for gi, w in enumerate(POOL_WINDOWS):
        lanes = slice(gi * POOL_GROUP_DIM, (gi + 1) * POOL_GROUP_DIM)
        src, dst = ext_ref, tmp_ref
        shift = 1
        while shift < w:
            dst[POOL_ALIGN:, lanes] = (src[POOL_ALIGN:, lanes]
                                       + src[POOL_ALIGN - shift:POOL_ALIGN - shift + n, lanes])
            src, dst = dst, src
            shift *= 2
        pooled = src[POOL_PREV:, lanes] / jnp.minimum(pos + 1.0, float(w)) - h[:, lanes]
        outs.append(_dot(pooled.astype(BF16), w_ref[gi]))
    o_ref[0] = x + jnp.concatenate(outs, axis=-1) * sc_ref[...]

    @pl.when(j == pl.num_programs(1) - 1)
    def _():
        hl_ref[0] = h[tm - POOL_TAIL:]


def _pool_prompt(x, gain, w_pool, scale):
    b, t, _ = x.shape
    tm = 512
    assert t % tm == 0 and tm % POOL_PREV == 0
    per_tile = tm // POOL_PREV
    return pl.pallas_call(
        _pool_prompt_kernel,
        grid=(b, t // tm),
        in_specs=[pl.BlockSpec((1, tm, D_MODEL), lambda i, j: (i, j, 0)),
                  pl.BlockSpec((1, POOL_PREV, D_MODEL),
                               lambda i, j: (i, jnp.maximum(j * per_tile - 1, 0), 0)),
                  _resident(gain.shape), _resident(w_pool.shape), _resident(scale.shape)],
        out_specs=[pl.BlockSpec((1, tm, D_MODEL), lambda i, j: (i, j, 0)),
                   pl.BlockSpec((1, POOL_TAIL, D_MODEL), lambda i, j: (i, 0, 0))],
        out_shape=[jax.ShapeDtypeStruct(x.shape, F32),
                   jax.ShapeDtypeStruct((b, POOL_TAIL, D_MODEL), F32)],
        scratch_shapes=[pltpu.VMEM((POOL_PREV + tm, D_MODEL), F32),
                        pltpu.VMEM((POOL_PREV + tm, D_MODEL), F32)],
        compiler_params=_params(("parallel", "arbitrary"), 32),
        name="pool_prompt",
    )(x, x, gain, w_pool, scale)


def _pool_sample_kernel(x_ref, pre_ref, g_ref, w_ref, sc_ref, o_ref, h_ref, *, pos0):
    t_len = x_ref.shape[0]
    hs = [_rms(x_ref[t], g_ref[...]) for t in range(t_len)]
    for t in range(t_len):
        h_ref[t] = hs[t]

    def ext(e, lanes):
        return pre_ref[e, :, lanes] if e < POOL_BUF else hs[e - POOL_BUF][:, lanes]

    for t in range(t_len):
        outs = []
        for gi, w in enumerate(POOL_WINDOWS):
            lanes = slice(gi * POOL_GROUP_DIM, (gi + 1) * POOL_GROUP_DIM)
            acc = hs[t][:, lanes]
            for k in range(1, w):
                acc = acc + ext(POOL_BUF + t - k, lanes)
            pooled = acc / min(pos0 + t + 1.0, float(w)) - hs[t][:, lanes]
            outs.append(_dot(pooled.astype(BF16), w_ref[gi]))
        o_ref[t] = x_ref[t] + jnp.concatenate(outs, axis=-1) * sc_ref[...]


def _pool_sample(x_t, prefix_t, gain, w_pool, scale, pos0):
    vmem = pl.BlockSpec(memory_space=pltpu.VMEM)
    return pl.pallas_call(
        functools.partial(_pool_sample_kernel, pos0=pos0),
        in_specs=[vmem] * 5,
        out_specs=[vmem, vmem],
        out_shape=[jax.ShapeDtypeStruct(x_t.shape, F32), jax.ShapeDtypeStruct(x_t.shape, F32)],
        compiler_params=_params(None, 32),
        name="pool_sample",
    )(x_t, prefix_t, gain, w_pool, scale)


def _rope_tables(pos, half):
    inv = ROPE_BASE ** (-jnp.arange(half, dtype=F32) / half)
    ang = pos.astype(F32)[:, None] * inv[None, :]
    return jnp.cos(ang), jnp.sin(ang)


def _mla_rope_tables(pos):
    cos, sin = _rope_tables(pos, MLA_ROPE // 2)
    zeros = jnp.zeros((pos.shape[0], 128 - MLA_ROPE), F32)
    return (jnp.concatenate([cos, cos, zeros], axis=-1),
            jnp.concatenate([-sin, sin, zeros], axis=-1))


def _swap_halves(w):
    half = w.shape[-1] // 2
    return jnp.concatenate([w[..., half:], w[..., :half]], axis=-1)


def _pad_lanes(w, width):
    return jnp.pad(w, [(0, 0)] * (w.ndim - 1) + [(0, width - w.shape[-1])])


def _row(v):
    return v.reshape(1, -1).astype(F32)


def kernel(x_prompt, x_sample, state_ret, cache_mla_latent, cache_mla_krope, page_table, state_pool,
           norm_mix, norm_ffn, norm_final,
           ret_w_in, ret_gn_gain, ret_w_out,
           cm_w_in, cm_ln_gain, cm_w_spatial, cm_b_spatial, cm_w_out,
           mla_w_down, mla_q_norm, mla_kv_norm, mla_w_uq, mla_w_uk, mla_w_uv, mla_w_out,
           pool_w, pool_scale,
           ffn_w_gate_up, ffn_w_down):
    bp, tp, _ = x_prompt.shape
    bs, ts, _ = x_sample.shape
    xp = x_prompt
    xs = x_sample.reshape(bs * ts, D_MODEL)
    pos_p = jnp.arange(tp)
    pos_s = PAST_LEN + jnp.arange(ts)

    def ffn(xp_, xs_, i, final):
        yp, ys = _ffn(xp_.reshape(-1, D_MODEL), xs_, _row(norm_ffn[i]), ffn_w_gate_up, ffn_w_down, i,
                      _row(norm_final), final)
        return yp.reshape(xp_.shape), ys

    log_g = jnp.log1p(-jnp.exp2(-5.0 - jnp.arange(RET_HEADS, dtype=F32)))
    w_in = ret_w_in[0]
    w_out = ret_w_out[0]
    gn_gain = _row(ret_gn_gain[0])
    cos_p, sin_p = _rope_tables(pos_p, RET_QK_DIM // 2)
    cos_s, sin_s = _rope_tables(pos_s, RET_QK_DIM // 2)
    xp, ret_state_p = _ret_prompt(xp, _row(norm_mix[0]), w_in, cos_p, sin_p,
                                  _ret_tables(float(RET_PROMPT_CHUNK), RET_PROMPT_CHUNK, log_g),
                                  gn_gain, w_out)
    proj_s = _norm_matmul(xs, _row(norm_mix[0]), w_in, 1024)
    y_s, ret_state_s = _ret_sample(proj_s, state_ret[0], cos_s, sin_s,
                                   _ret_tables(float(ts), RET_CHUNK, log_g), gn_gain)
    xs = _matmul_residual(y_s, w_out, xs)
    xp, xs = ffn(xp, xs, 0, False)

    cw_in = cm_w_in[0]
    cw_out = cm_w_out[0]
    bias_full = jnp.repeat(jnp.transpose(cm_b_spatial[0]), CM_GROUP_DIM, axis=1)
    xp2, cm_v_p = _cm_prompt(xp.reshape(bp * tp, D_MODEL), _row(norm_mix[1]), cw_in,
                             _row(cm_ln_gain[0]), cm_w_spatial[0], bias_full, cw_out, tp)
    xp = xp2.reshape(bp, tp, D_MODEL)
    xs_t, cm_v_s_t = _cm_sample(xs.reshape(bs, ts, D_MODEL).transpose(1, 0, 2), _row(norm_mix[1]), cw_in,
                                _row(cm_ln_gain[0]),
                                cm_w_spatial[0][:, :ts, :ts].reshape(CM_GROUPS, ts * ts),
                                cm_b_spatial[0][:, :ts], cw_out)
    xs = xs_t.transpose(1, 0, 2).reshape(bs * ts, D_MODEL)
    cm_v_s = cm_v_s_t.transpose(1, 0, 2)
    xp, xs = ffn(xp, xs, 1, False)

    wd = mla_w_down[0]
    kr_w = wd[:, MLA_Q_LORA + MLA_KV_LORA:]
    wd_ext = jnp.concatenate([wd[:, :MLA_Q_LORA + MLA_KV_LORA], _pad_lanes(kr_w, 128),
                              _pad_lanes(_swap_halves(kr_w), 128)], axis=-1).astype(BF16)
    wuq = mla_w_uq[0].reshape(MLA_Q_LORA, MLA_HEADS, MLA_NOPE + MLA_ROPE)
    wuq_rope = wuq[:, :, MLA_NOPE:]
    wuq_ext = jnp.concatenate([
        wuq[:, :, :MLA_NOPE].reshape(MLA_Q_LORA, -1),
        _pad_lanes(wuq_rope, 128).reshape(MLA_Q_LORA, -1),
        _pad_lanes(_swap_halves(wuq_rope), 128).reshape(MLA_Q_LORA, -1)], axis=-1).astype(BF16)
    wuk_t = mla_w_uk[0].transpose(1, 2, 0).astype(BF16)
    wuv = mla_w_uv[0].transpose(1, 0, 2).astype(BF16)
    mw_out = mla_w_out[0].astype(BF16)
    cos2_p, sin2_p = _mla_rope_tables(pos_p)
    cos2_s, sin2_s = _mla_rope_tables(jnp.tile(pos_s, bs))
    gq, gkv = _row(mla_q_norm[0]), _row(mla_kv_norm[0])
    q_p, lat_p, krt_p, kext_p, vt_p = _mla_q(xp, _row(norm_mix[2]), wd_ext, gq, gkv, wuq_ext, wuk_t,
                                             cos2_p, sin2_p, 2 * MLA_ATTN_TQ)
    kr_p = jnp.swapaxes(krt_p, 1, 2)
    xp = _mla_attn_prompt(q_p, kext_p, vt_p, xp, wuv, mw_out)
    ns = bs * ts
    q_s, lat_s, krt_s, kext_s, _ = _mla_q(xs.reshape(1, ns, D_MODEL), _row(norm_mix[2]), wd_ext, gq, gkv,
                                          wuq_ext, wuk_t, cos2_s, sin2_s, ns)
    q_s = q_s.reshape(MLA_HEADS, bs, ts, MLA_KEY_DIM).transpose(1, 0, 2, 3).reshape(bs, MLA_HEADS * ts, MLA_KEY_DIM)
    knew = jnp.pad(kext_s.reshape(bs, ts, MLA_KEY_DIM), ((0, 0), (0, PAGE_SIZE - ts), (0, 0)))
    o_s = _mla_attn_sample(page_table, q_s, cache_mla_latent[0],
                           jnp.swapaxes(cache_mla_krope[0], 1, 2), knew, ts)
    o_s = o_s.reshape(bs, MLA_HEADS, ts, MLA_KV_LORA).transpose(1, 0, 2, 3).reshape(1, MLA_HEADS, ns, MLA_KV_LORA)
    xs = _mla_out(o_s.astype(BF16), wuv, mw_out, xs.reshape(1, ns, D_MODEL), ns).reshape(ns, D_MODEL)
    lat_s = lat_s.reshape(bs, ts, MLA_KV_LORA)
    kr_s = jnp.swapaxes(krt_s, 1, 2).reshape(bs, ts, MLA_ROPE)
    xp, xs = ffn(xp, xs, 2, False)

    pw = pool_w[0].astype(BF16)
    xp, h_last = _pool_prompt(xp, _row(norm_mix[3]), pw, _row(pool_scale[0]))
    pool_state_p = h_last[:, POOL_TAIL - POOL_BUF:]
    xs_t, h_s_t = _pool_sample(xs.reshape(bs, ts, D_MODEL).transpose(1, 0, 2),
                               state_pool[0].transpose(1, 0, 2), _row(norm_mix[3]), pw,
                               _row(pool_scale[0]), float(PAST_LEN))
    xs = xs_t.transpose(1, 0, 2).reshape(bs * ts, D_MODEL)
    pool_state_s = jnp.concatenate([state_pool[0], h_s_t.transpose(1, 0, 2)], axis=1)[:, -POOL_BUF:]
    xp, xs = ffn(xp, xs, 3, True)

    return (xp, xs.reshape(bs, ts, D_MODEL),
            ret_state_p[None], ret_state_s[None],
            cm_v_p.reshape(bp, CM_CHUNK, D_MODEL)[None], cm_v_s[None],
            lat_p[None], kr_p[None], lat_s[None], kr_s[None],
            pool_state_p[None], pool_state_s[None])
```

```python
import functools

import jax
import jax.numpy as jnp
from jax import lax
from jax.experimental import pallas as pl
from jax.experimental.pallas import tpu as pltpu

F32 = jnp.float32
BF16 = jnp.bfloat16

D_MODEL = 1024
NORM_EPS = 1e-6
ROPE_BASE = 10000.0
PAST_LEN = 8192
PAGE_SIZE = 128

RET_HEADS = 4
RET_QK_DIM = 256
RET_V_DIM = 512
RET_CHUNK = 128
RET_PROMPT_CHUNK = 256
RET_STAGE_WIDTH = 512
RET_SAMPLE_ROWS_PER_STEP = 4

CM_CHUNK = 128
CM_GROUPS = 4
CM_GROUP_DIM = 256

MLA_HEADS = 8
MLA_Q_LORA = 384
MLA_KV_LORA = 256
MLA_NOPE = 128
MLA_ROPE = 64
MLA_V = 128
MLA_SCALE = (MLA_NOPE + MLA_ROPE) ** -0.5
MLA_KEY_DIM = MLA_KV_LORA + 128
MLA_KEY_CHUNK = 1024
MLA_SAMPLE_ROWS_PER_STEP = 2
MLA_ATTN_TQ = 256

POOL_WINDOWS = (2, 4, 8, 16)
POOL_GROUP_DIM = 256
POOL_BUF = 15
POOL_PREV = 32
POOL_ALIGN = 8
POOL_TAIL = 16

FFN_HIDDEN = 2816
FFN_CHUNK = 256

V7X_VMEM_BYTES = 64 * 1024 * 1024
MIB = 1024 * 1024


def _params(semantics, vmem_mib):
    assert vmem_mib * MIB < V7X_VMEM_BYTES
    return pltpu.CompilerParams(dimension_semantics=semantics,
                                vmem_limit_bytes=vmem_mib * MIB)


def _resident(shape):
    nd = len(shape)
    return pl.BlockSpec(shape, lambda *_: (0,) * nd, pipeline_mode=pl.Buffered(1))


def _rms(x, g):
    return x * lax.rsqrt(jnp.mean(x * x, axis=-1, keepdims=True) + NORM_EPS) * g


def _dot(a, b):
    return jnp.dot(a, b, preferred_element_type=F32)


def _wdot(a, w):
    return jnp.dot(a, w.astype(BF16), preferred_element_type=F32)


def _dot_nt(a, b):
    return lax.dot_general(a, b, (((1,), (1,)), ((), ())), preferred_element_type=F32)


def _ffn_kernel(xp_ref, xs_ref, g_ref, wgu_hbm, wd_hbm, gf_ref, op_ref, os_ref,
                wgu_ref, wd_ref, sems, *, final, layer):
    n_chunks = FFN_HIDDEN // FFN_CHUNK

    def chunk_copies(j):
        cols = pl.ds(j * FFN_CHUNK, FFN_CHUNK)
        ucols = pl.ds(FFN_HIDDEN + j * FFN_CHUNK, FFN_CHUNK)
        return (pltpu.make_async_copy(wgu_hbm.at[layer, :, cols], wgu_ref.at[:, cols], sems.at[0, j]),
                pltpu.make_async_copy(wgu_hbm.at[layer, :, ucols], wgu_ref.at[:, ucols], sems.at[1, j]),
                pltpu.make_async_copy(wd_hbm.at[layer, cols, :], wd_ref.at[cols, :], sems.at[2, j]))

    def tile(x_ref, o_ref, wait_for_weights):
        x = x_ref[...]
        h = _rms(x, g_ref[...]).astype(BF16)
        acc = x
        for j in range(n_chunks):
            if wait_for_weights:
                for cp in chunk_copies(j):
                    cp.wait()
            cols = slice(j * FFN_CHUNK, (j + 1) * FFN_CHUNK)
            ucols = slice(FFN_HIDDEN + j * FFN_CHUNK, FFN_HIDDEN + (j + 1) * FFN_CHUNK)
            g = _dot(h, wgu_ref[:, cols].astype(BF16))
            u = _dot(h, wgu_ref[:, ucols].astype(BF16))
            a = (g * jax.nn.sigmoid(g)) * u
            acc = acc + _dot(a.astype(BF16), wd_ref[cols, :].astype(BF16))
        o_ref[...] = _rms(acc, gf_ref[...]) if final else acc

    i = pl.program_id(0)
    is_prompt = i < pl.num_programs(0) - 1

    @pl.when(i == 0)
    def _():
        for j in range(n_chunks):
            for cp in chunk_copies(j):
                cp.start()
        tile(xp_ref, op_ref, True)

    @pl.when((i > 0) & is_prompt)
    def _():
        tile(xp_ref, op_ref, False)

    @pl.when(jnp.logical_not(is_prompt))
    def _():
        tile(xs_ref, os_ref, False)


def _ffn(xp, xs, gain, wgu_all, wd_all, layer, final_gain, final):
    tm = xs.shape[0]
    assert xp.shape[0] % tm == 0 and xp.shape[0] >= 2 * tm
    n_prompt = xp.shape[0] // tm
    hbm = pl.BlockSpec(memory_space=pl.ANY)
    prompt_tile = pl.BlockSpec((tm, D_MODEL), lambda i: (jnp.minimum(i, n_prompt - 1), 0))
    sample_tile = pl.BlockSpec((tm, D_MODEL), lambda i: (0, 0))
    return pl.pallas_call(
        functools.partial(_ffn_kernel, final=final, layer=layer),
        grid=(n_prompt + 1,),
        in_specs=[prompt_tile, sample_tile, _resident(gain.shape), hbm, hbm,
                  _resident(final_gain.shape)],
        out_specs=[prompt_tile, sample_tile],
        out_shape=[jax.ShapeDtypeStruct(xp.shape, F32), jax.ShapeDtypeStruct(xs.shape, F32)],
        scratch_shapes=[pltpu.VMEM(wgu_all.shape[1:], F32), pltpu.VMEM(wd_all.shape[1:], F32),
                        pltpu.SemaphoreType.DMA((3, FFN_HIDDEN // FFN_CHUNK))],
        compiler_params=_params(("arbitrary",), 58),
        name="ffn",
    )(xp, xs, gain, wgu_all, wd_all, final_gain)


def _norm_matmul_kernel(x_ref, g_ref, w_ref, o_ref):
    h = _rms(x_ref[...], g_ref[...]).astype(BF16)
    o_ref[...] = _wdot(h, w_ref[...])


def _norm_matmul(x, gain, w, tn):
    n, dout = x.shape[0], w.shape[1]
    assert dout % tn == 0
    return pl.pallas_call(
        _norm_matmul_kernel,
        grid=(dout // tn,),
        in_specs=[pl.BlockSpec((n, D_MODEL), lambda j: (0, 0)),
                  pl.BlockSpec(gain.shape, lambda j: (0, 0)),
                  pl.BlockSpec((D_MODEL, tn), lambda j: (0, j))],
        out_specs=pl.BlockSpec((n, tn), lambda j: (0, j)),
        out_shape=jax.ShapeDtypeStruct((n, dout), F32),
        compiler_params=_params(("parallel",), 32),
        name="norm_matmul",
    )(x, gain, w)


def _matmul_residual_kernel(y_ref, w_ref, x_ref, o_ref):
    o_ref[...] = x_ref[...] + _wdot(y_ref[...].astype(BF16), w_ref[...])


def _matmul_residual(y, w, x):
    n = x.shape[0]
    tm = min(n, 512)
    assert n % tm == 0
    return pl.pallas_call(
        _matmul_residual_kernel,
        grid=(n // tm,),
        in_specs=[pl.BlockSpec((tm, y.shape[1]), lambda i: (i, 0)),
                  _resident(w.shape),
                  pl.BlockSpec((tm, D_MODEL), lambda i: (i, 0))],
        out_specs=pl.BlockSpec((tm, D_MODEL), lambda i: (i, 0)),
        out_shape=jax.ShapeDtypeStruct(x.shape, F32),
        compiler_params=_params(("parallel",), 32),
        name="matmul_residual",
    )(y, w, x)


def _rope_halves(x, cos, sin):
    half = cos.shape[-1]
    x1, x2 = x[:, :half], x[:, half:]
    return jnp.concatenate([x1 * cos - x2 * sin, x1 * sin + x2 * cos], axis=-1)


def _ret_chunk(q, k, v, s, dmat, qdec, kdec, sdec):
    vb = v.astype(BF16)
    scores = _dot_nt(q.astype(BF16), k.astype(BF16)) * dmat
    o = _dot(scores.astype(BF16), vb) + _dot((q * qdec).astype(BF16), s.astype(BF16))
    s_new = sdec * s + _dot((k * kdec).T.astype(BF16), vb)
    return o, s_new


def _group_norm_gate(o, gate, gain):
    mu = jnp.mean(o, axis=-1, keepdims=True)
    oc = o - mu
    var = jnp.mean(oc * oc, axis=-1, keepdims=True)
    on = oc * lax.rsqrt(var + NORM_EPS)
    return (gate * jax.nn.sigmoid(gate)) * on * gain


def _ret_prompt_kernel(x_ref, g_ref, win_hbm, cos_ref, sin_ref, dmat_ref, qdec_ref,
                       kdec_ref, sdec_ref, gng_ref, wout_hbm, o_ref, st_ref,
                       s_ref, y_ref, win_ref, wout_ref, stage_in, stage_out, sems):
    j = pl.program_id(1)

    def load_weight(w_hbm, w_ref, stage, sem_row, axis):
        width = stage.shape[1 + axis]
        n = w_hbm.shape[axis] // width

        def window(ref, c):
            return ref.at[pl.ds(c * width, width), :] if axis == 0 else ref.at[:, pl.ds(c * width, width)]

        def copy(c):
            return pltpu.make_async_copy(window(w_hbm, c), stage.at[c % 2], sems.at[sem_row, c % 2])

        copy(0).start()
        for c in range(n):
            if c + 1 < n:
                copy(c + 1).start()
            copy(c).wait()
            window(w_ref, c)[...] = stage[c % 2].astype(BF16)

    @pl.when((pl.program_id(0) == 0) & (j == 0))
    def _():
        load_weight(win_hbm, win_ref, stage_in, 0, 1)
        load_weight(wout_hbm, wout_ref, stage_out, 1, 0)

    @pl.when(j == 0)
    def _():
        s_ref[...] = jnp.zeros_like(s_ref)

    x = x_ref[0]
    tm = x.shape[0]
    h = _rms(x, g_ref[...]).astype(BF16)
    for hh in range(RET_HEADS):
        qs = slice(hh * RET_QK_DIM, (hh + 1) * RET_QK_DIM)
        ks = slice(D_MODEL + hh * RET_QK_DIM, D_MODEL + (hh + 1) * RET_QK_DIM)
        vs = slice(2 * D_MODEL + hh * RET_V_DIM, 2 * D_MODEL + (hh + 1) * RET_V_DIM)
        gs = slice(4 * D_MODEL + hh * RET_V_DIM, 4 * D_MODEL + (hh + 1) * RET_V_DIM)
        q = _rope_halves(_wdot(h, win_ref[:, qs]), cos_ref[...], sin_ref[...]) * (RET_QK_DIM ** -0.5)
        k = _rope_halves(_wdot(h, win_ref[:, ks]), cos_ref[...], sin_ref[...])
        v = _wdot(h, win_ref[:, vs])
        gate = _wdot(h, win_ref[:, gs])
        gain = gng_ref[:, hh * RET_V_DIM:(hh + 1) * RET_V_DIM]
        chunk = dmat_ref.shape[1]
        for c in range(tm // chunk):
            rows = slice(c * chunk, (c + 1) * chunk)
            o, s_new = _ret_chunk(q[rows], k[rows], v[rows], s_ref[hh], dmat_ref[hh],
                                  qdec_ref[hh], kdec_ref[hh], sdec_ref[hh])
            s_ref[hh] = s_new
            y_ref[rows, hh * RET_V_DIM:(hh + 1) * RET_V_DIM] = (
                _group_norm_gate(o, gate[rows], gain).astype(BF16))
    o_ref[0] = x + _wdot(y_ref[...], wout_ref[...])

    @pl.when(j == pl.num_programs(1) - 1)
    def _():
        st_ref[0] = s_ref[...]


def _ret_tables(length, chunk, log_g):
    idx = jnp.arange(chunk, dtype=F32)
    valid = idx < length
    rel = idx[:, None] - idx[None, :]
    ok = (rel >= 0) & valid[:, None] & valid[None, :]
    dmat = jnp.where(ok[None], jnp.exp(jnp.maximum(rel, 0.0)[None] * log_g[:, None, None]), 0.0)
    qd = jnp.where(valid[None], jnp.exp((idx + 1.0)[None, :] * log_g[:, None]), 0.0)
    kd = jnp.where(valid[None], jnp.exp((length - 1.0 - idx)[None, :] * log_g[:, None]), 0.0)
    qdec = jnp.broadcast_to(qd[:, :, None], (RET_HEADS, chunk, RET_QK_DIM))
    kdec = jnp.broadcast_to(kd[:, :, None], (RET_HEADS, chunk, RET_QK_DIM))
    sdec = jnp.broadcast_to(jnp.exp(length * log_g)[:, None, None], (RET_HEADS, 1, RET_V_DIM))
    return dmat, qdec, kdec, sdec


def _ret_prompt(x, gain, w_in, cos, sin, tables, gn_gain, w_out):
    b, t, _ = x.shape
    tm = 512
    assert t % tm == 0
    dmat, qdec, kdec, sdec = tables
    hbm = pl.BlockSpec(memory_space=pl.ANY)
    return pl.pallas_call(
        _ret_prompt_kernel,
        grid=(b, t // tm),
        in_specs=[pl.BlockSpec((1, tm, D_MODEL), lambda i, j: (i, j, 0)),
                  _resident(gain.shape), hbm,
                  pl.BlockSpec((tm, RET_QK_DIM // 2), lambda i, j: (j, 0)),
                  pl.BlockSpec((tm, RET_QK_DIM // 2), lambda i, j: (j, 0)),
                  _resident(dmat.shape), _resident(qdec.shape), _resident(kdec.shape),
                  _resident(sdec.shape), _resident(gn_gain.shape), hbm],
        out_specs=[pl.BlockSpec((1, tm, D_MODEL), lambda i, j: (i, j, 0)),
                   pl.BlockSpec((1, RET_HEADS, RET_QK_DIM, RET_V_DIM), lambda i, j: (i, 0, 0, 0))],
        out_shape=[jax.ShapeDtypeStruct(x.shape, F32),
                   jax.ShapeDtypeStruct((b, RET_HEADS, RET_QK_DIM, RET_V_DIM), F32)],
        scratch_shapes=[pltpu.VMEM((RET_HEADS, RET_QK_DIM, RET_V_DIM), F32),
                        pltpu.VMEM((tm, RET_HEADS * RET_V_DIM), BF16),
                        pltpu.VMEM(w_in.shape, BF16), pltpu.VMEM(w_out.shape, BF16),
                        pltpu.VMEM((2, w_in.shape[0], RET_STAGE_WIDTH), F32),
                        pltpu.VMEM((2, RET_STAGE_WIDTH, w_out.shape[1]), F32),
                        pltpu.SemaphoreType.DMA((2, 2))],
        compiler_params=_params(("arbitrary", "arbitrary"), 56),
        name="ret_prompt",
    )(x, gain, w_in, cos, sin, dmat, qdec, kdec, sdec, gn_gain, w_out)


def _ret_sample_kernel(p_ref, s0_ref, cos_ref, sin_ref, dmat_ref, qdec_ref, kdec_ref,
                       sdec_ref, gng_ref, y_ref, st_ref, qp_ref, kp_ref, vp_ref):
    @pl.when(pl.program_id(0) == 0)
    def _():
        qp_ref[...] = jnp.zeros_like(qp_ref)
        kp_ref[...] = jnp.zeros_like(kp_ref)
        vp_ref[...] = jnp.zeros_like(vp_ref)

    t = cos_ref.shape[0]
    for bb in range(s0_ref.shape[0]):
        rows = slice(bb * t, (bb + 1) * t)
        for hh in range(RET_HEADS):
            qs = slice(hh * RET_QK_DIM, (hh + 1) * RET_QK_DIM)
            ks = slice(D_MODEL + hh * RET_QK_DIM, D_MODEL + (hh + 1) * RET_QK_DIM)
            vs = slice(2 * D_MODEL + hh * RET_V_DIM, 2 * D_MODEL + (hh + 1) * RET_V_DIM)
            gs = slice(4 * D_MODEL + hh * RET_V_DIM, 4 * D_MODEL + (hh + 1) * RET_V_DIM)
            qp_ref[0:t, :] = (_rope_halves(p_ref[rows, qs], cos_ref[...], sin_ref[...])
                              * (RET_QK_DIM ** -0.5))
            kp_ref[0:t, :] = _rope_halves(p_ref[rows, ks], cos_ref[...], sin_ref[...])
            vp_ref[0:t, :] = p_ref[rows, vs]
            o, s_new = _ret_chunk(qp_ref[...], kp_ref[...], vp_ref[...], s0_ref[bb, hh], dmat_ref[hh],
                                  qdec_ref[hh], kdec_ref[hh], sdec_ref[hh])
            st_ref[bb, hh] = s_new
            gain = gng_ref[:, hh * RET_V_DIM:(hh + 1) * RET_V_DIM]
            y_ref[rows, hh * RET_V_DIM:(hh + 1) * RET_V_DIM] = _group_norm_gate(o[0:t], p_ref[rows, gs], gain)


def _ret_sample(proj, s0, cos, sin, tables, gn_gain):
    b = s0.shape[0]
    t = cos.shape[0]
    nb = RET_SAMPLE_ROWS_PER_STEP
    assert b % nb == 0 and proj.shape[0] == b * t
    dmat, qdec, kdec, sdec = tables
    state_spec = pl.BlockSpec((nb, RET_HEADS, RET_QK_DIM, RET_V_DIM), lambda i: (i, 0, 0, 0))
    return pl.pallas_call(
        _ret_sample_kernel,
        grid=(b // nb,),
        in_specs=[pl.BlockSpec((nb * t, proj.shape[1]), lambda i: (i, 0)),
                  state_spec,
                  _resident(cos.shape), _resident(sin.shape),
                  _resident(dmat.shape), _resident(qdec.shape), _resident(kdec.shape),
                  _resident(sdec.shape), _resident(gn_gain.shape)],
        out_specs=[pl.BlockSpec((nb * t, RET_HEADS * RET_V_DIM), lambda i: (i, 0)),
                   state_spec],
        out_shape=[jax.ShapeDtypeStruct((b * t, RET_HEADS * RET_V_DIM), F32),
                   jax.ShapeDtypeStruct(s0.shape, F32)],
        scratch_shapes=[pltpu.VMEM((RET_CHUNK, RET_QK_DIM), F32),
                        pltpu.VMEM((RET_CHUNK, RET_QK_DIM), F32),
                        pltpu.VMEM((RET_CHUNK, RET_V_DIM), F32)],
        compiler_params=_params(("arbitrary",), 48),
        name="ret_sample",
    )(proj, s0, cos, sin, dmat, qdec, kdec, sdec, gn_gain)


def _gelu_tanh(x):
    return x * (0.5 * (1.0 + jnp.tanh(0.7978845608028654 * (x + 0.044715 * (x * x * x)))))


def _layer_norm(v, gain):
    mu = jnp.mean(v, axis=-1, keepdims=True)
    vc = v - mu
    var = jnp.mean(vc * vc, axis=-1, keepdims=True)
    return vc * lax.rsqrt(var + NORM_EPS) * gain


def _cm_prompt_kernel(x_ref, g_ref, win_ref, lng_ref, ws_ref, bias_ref, wout_ref,
                      o_ref, v_ref, z_ref):
    x = x_ref[...]
    tm = x.shape[0]
    h = _rms(x, g_ref[...]).astype(BF16)
    u = _gelu_tanh(_wdot(h, win_ref[:, :D_MODEL]))
    v = _layer_norm(_gelu_tanh(_wdot(h, win_ref[:, D_MODEL:])), lng_ref[...])
    v_ref[...] = v[tm - CM_CHUNK:]
    vb = v.astype(BF16)
    row = lax.broadcasted_iota(jnp.int32, (CM_CHUNK, CM_CHUNK), 0)
    col = lax.broadcasted_iota(jnp.int32, (CM_CHUNK, CM_CHUNK), 1)
    for gi in range(CM_GROUPS):
        lanes = slice(gi * CM_GROUP_DIM, (gi + 1) * CM_GROUP_DIM)
        w = jnp.where(row >= col, ws_ref[gi], 0.0).astype(BF16)
        for c in range(tm // CM_CHUNK):
            rows = slice(c * CM_CHUNK, (c + 1) * CM_CHUNK)
            mixed = _dot(w, vb[rows, lanes]) + bias_ref[:, lanes]
            z_ref[rows, lanes] = (u[rows, lanes] * mixed).astype(BF16)
    o_ref[...] = x + _wdot(z_ref[...], wout_ref[...])


def _cm_prompt(x, gain, w_in, ln_gain, w_s, bias_full, w_out, seq):
    n = x.shape[0]
    tm = 1024
    assert seq % tm == 0 and n % seq == 0
    per_seq = seq // tm
    return pl.pallas_call(
        _cm_prompt_kernel,
        grid=(n // tm,),
        in_specs=[pl.BlockSpec((tm, D_MODEL), lambda i: (i, 0)),
                  _resident(gain.shape), _resident(w_in.shape), _resident(ln_gain.shape),
                  _resident(w_s.shape), _resident(bias_full.shape), _resident(w_out.shape)],
        out_specs=[pl.BlockSpec((tm, D_MODEL), lambda i: (i, 0)),
                   pl.BlockSpec((CM_CHUNK, D_MODEL), lambda i: (i // per_seq, 0))],
        out_shape=[jax.ShapeDtypeStruct(x.shape, F32),
                   jax.ShapeDtypeStruct((n // seq * CM_CHUNK, D_MODEL), F32)],
        scratch_shapes=[pltpu.VMEM((tm, D_MODEL), BF16)],
        compiler_params=_params(("arbitrary",), 40),
        name="cm_prompt",
    )(x, gain, w_in, ln_gain, w_s, bias_full, w_out)


def _cm_sample_kernel(ws_ref, bs_ref, x_ref, g_ref, win_ref, lng_ref, wout_ref, o_ref, v_ref):
    t_len = x_ref.shape[0]
    us, vs = [], []
    for t in range(t_len):
        h = _rms(x_ref[t], g_ref[...]).astype(BF16)
        us.append(_gelu_tanh(_wdot(h, win_ref[:, :D_MODEL])))
        v = _layer_norm(_gelu_tanh(_wdot(h, win_ref[:, D_MODEL:])), lng_ref[...])
        v_ref[t] = v
        vs.append(v)
    for t in range(t_len):
        parts = []
        for gi in range(CM_GROUPS):
            lanes = slice(gi * CM_GROUP_DIM, (gi + 1) * CM_GROUP_DIM)
            mixed = jnp.full_like(vs[t][:, lanes], bs_ref[gi, t])
            for s in range(t + 1):
                mixed = mixed + ws_ref[gi, t * t_len + s] * vs[s][:, lanes]
            parts.append(us[t][:, lanes] * mixed)
        z = jnp.concatenate(parts, axis=-1).astype(BF16)
        o_ref[t] = x_ref[t] + _wdot(z, wout_ref[...])


def _cm_sample(x_t, gain, w_in, ln_gain, w_s_small, b_s_small, w_out):
    smem = pl.BlockSpec(memory_space=pltpu.SMEM)
    vmem = pl.BlockSpec(memory_space=pltpu.VMEM)
    return pl.pallas_call(
        _cm_sample_kernel,
        in_specs=[smem, smem, vmem, vmem, vmem, vmem, vmem],
        out_specs=[vmem, vmem],
        out_shape=[jax.ShapeDtypeStruct(x_t.shape, F32), jax.ShapeDtypeStruct(x_t.shape, F32)],
        compiler_params=_params(None, 32),
        name="cm_sample",
    )(w_s_small, b_s_small, x_t, gain, w_in, ln_gain, w_out)


def _mla_q_kernel(x_ref, g_ref, wd_ref, gq_ref, gkv_ref, wuq_ref, wuk_ref, cos_ref, sin_ref,
                  q_ref, c_ref, krt_ref, kext_ref, vt_ref):
    h = _rms(x_ref[0], g_ref[...]).astype(BF16)
    d = _dot(h, wd_ref[...])
    cq = _rms(d[:, :MLA_Q_LORA], gq_ref[...]).astype(BF16)
    ckv = _rms(d[:, MLA_Q_LORA:MLA_Q_LORA + MLA_KV_LORA], gkv_ref[...])
    cos, sin = cos_ref[...], sin_ref[...]
    base = MLA_Q_LORA + MLA_KV_LORA
    krp = d[:, base:base + 128] * cos + d[:, base + 128:base + 256] * sin
    c_ref[0] = ckv
    krt_ref[0] = krp.T[:MLA_ROPE]
    kext_ref[0] = jnp.concatenate([ckv.astype(BF16), krp.astype(BF16)], axis=-1)
    vt_ref[0, 0] = ckv.T.astype(BF16)
    nope_w = MLA_HEADS * MLA_NOPE
    for hh in range(MLA_HEADS):
        qn = _dot(cq, wuq_ref[:, hh * MLA_NOPE:(hh + 1) * MLA_NOPE])
        raw = _dot(cq, wuq_ref[:, nope_w + hh * 128:nope_w + (hh + 1) * 128])
        rot = _dot(cq, wuq_ref[:, 2 * nope_w + hh * 128:2 * nope_w + (hh + 1) * 128])
        ql = _dot(qn.astype(BF16), wuk_ref[hh]) * MLA_SCALE
        qr = (raw * cos + rot * sin) * MLA_SCALE
        q_ref[0, hh] = jnp.concatenate([ql.astype(BF16), qr.astype(BF16)], axis=-1)


def _mla_q(x, gain, wd_ext, g_q, g_kv, wuq_ext, wuk_t, cos2, sin2, tm):
    nb, t, _ = x.shape
    assert t % tm == 0
    return pl.pallas_call(
        _mla_q_kernel,
        grid=(nb, t // tm),
        in_specs=[pl.BlockSpec((1, tm, D_MODEL), lambda i, j: (i, j, 0)),
                  _resident(gain.shape), _resident(wd_ext.shape), _resident(g_q.shape),
                  _resident(g_kv.shape), _resident(wuq_ext.shape), _resident(wuk_t.shape),
                  pl.BlockSpec((tm, 128), lambda i, j: (j, 0)),
                  pl.BlockSpec((tm, 128), lambda i, j: (j, 0))],
        out_specs=[pl.BlockSpec((1, MLA_HEADS, tm, MLA_KEY_DIM), lambda i, j: (i, 0, j, 0)),
                   pl.BlockSpec((1, tm, MLA_KV_LORA), lambda i, j: (i, j, 0)),
                   pl.BlockSpec((1, MLA_ROPE, tm), lambda i, j: (i, 0, j)),
                   pl.BlockSpec((1, tm, MLA_KEY_DIM), lambda i, j: (i, j, 0)),
                   pl.BlockSpec((1, 1, MLA_KV_LORA, tm), lambda i, j: (i, j, 0, 0))],
        out_shape=[jax.ShapeDtypeStruct((nb, MLA_HEADS, t, MLA_KEY_DIM), BF16),
                   jax.ShapeDtypeStruct((nb, t, MLA_KV_LORA), F32),
                   jax.ShapeDtypeStruct((nb, MLA_ROPE, t), F32),
                   jax.ShapeDtypeStruct((nb, t, MLA_KEY_DIM), BF16),
                   jax.ShapeDtypeStruct((nb, t // tm, MLA_KV_LORA, tm), BF16)],
        compiler_params=_params(("parallel", "parallel"), 40),
        name="mla_q",
    )(x, gain, wd_ext, g_q, g_kv, wuq_ext, wuk_t, cos2, sin2)


def _mla_attn_prompt_kernel(q_ref, k_ref, vt_ref, x_ref, wuv_ref, wout_ref, y_ref,
                            m_ref, l_ref, acc_ref, sa_ref, sb_ref, cat_ref, *, tq, n_tiles):
    tk = 2 * tq
    cols = MLA_HEADS * tq
    score_refs = (sa_ref, sb_ref)

    def scores(step, dst_ref):
        kb, n_keys, first_masked_key = step
        q = q_ref[0].reshape(cols, MLA_KEY_DIM)
        st = _dot_nt(k_ref[0, kb * tk:kb * tk + n_keys, :], q)
        if first_masked_key is not None:
            key_idx = lax.broadcasted_iota(jnp.int32, (n_keys, cols), 0) - first_masked_key
            qry_idx = lax.broadcasted_iota(jnp.int32, (n_keys, cols), 1) & (tq - 1)
            st = jnp.where(key_idx <= qry_idx, st, -jnp.inf)
        dst_ref[0:n_keys, :] = st

    def accumulate(step, src_ref):
        kb, n_keys, _ = step
        st = src_ref[0:n_keys, :]
        m_old = m_ref[...]
        m_new = jnp.maximum(m_old, jnp.max(st, axis=0, keepdims=True))
        p = jnp.exp(st - m_new)
        alpha = jnp.exp(m_old - m_new)
        l_ref[...] = alpha * l_ref[...] + jnp.sum(p, axis=0, keepdims=True)
        acc_ref[...] = alpha * acc_ref[...] + _dot(vt_ref[0, kb, :, :n_keys], p.astype(BF16))
        m_ref[...] = m_new

    def query_tile(qi):
        steps = [(kb, tk, None) for kb in range(qi // 2)]
        steps.append((qi // 2, tk, tq) if qi % 2 else (qi // 2, tq, 0))
        m_ref[...] = jnp.full_like(m_ref, -jnp.inf)
        l_ref[...] = jnp.zeros_like(l_ref)
        acc_ref[...] = jnp.zeros_like(acc_ref)
        scores(steps[0], score_refs[0])
        for i, step in enumerate(steps):
            if i + 1 < len(steps):
                scores(steps[i + 1], score_refs[(i + 1) % 2])
            accumulate(step, score_refs[i % 2])
    for qi in range(n_tiles):
        pl.when(pl.program_id(1) == qi)(functools.partial(query_tile, qi))

    o = (acc_ref[...] / l_ref[...]).T.astype(BF16)
    for hh in range(MLA_HEADS):
        cat_ref[:, hh * MLA_V:(hh + 1) * MLA_V] = _dot(o[hh * tq:(hh + 1) * tq],
                                                       wuv_ref[hh]).astype(BF16)
    y_ref[0] = x_ref[0] + _dot(cat_ref[...], wout_ref[...])


def _mla_attn_prompt(q, kext, vt, x, wuv, w_out):
    b, _, t, _ = q.shape
    tq = vt.shape[-1] // 2
    assert t % (2 * tq) == 0 and tq & (tq - 1) == 0
    return pl.pallas_call(
        functools.partial(_mla_attn_prompt_kernel, tq=tq, n_tiles=t // tq),
        grid=(b, t // tq),
        in_specs=[pl.BlockSpec((1, MLA_HEADS, tq, MLA_KEY_DIM), lambda i, j: (i, 0, j, 0)),
                  pl.BlockSpec((1, t, MLA_KEY_DIM), lambda i, j: (i, 0, 0)),
                  pl.BlockSpec((1, t // (2 * tq), MLA_KV_LORA, 2 * tq), lambda i, j: (i, 0, 0, 0)),
                  pl.BlockSpec((1, tq, D_MODEL), lambda i, j: (i, j, 0)),
                  _resident(wuv.shape), _resident(w_out.shape)],
        out_specs=pl.BlockSpec((1, tq, D_MODEL), lambda i, j: (i, j, 0)),
        out_shape=jax.ShapeDtypeStruct(x.shape, F32),
        scratch_shapes=[pltpu.VMEM((1, MLA_HEADS * tq), F32), pltpu.VMEM((1, MLA_HEADS * tq), F32),
                        pltpu.VMEM((MLA_KV_LORA, MLA_HEADS * tq), F32),
                        pltpu.VMEM((2 * tq, MLA_HEADS * tq), F32),
                        pltpu.VMEM((2 * tq, MLA_HEADS * tq), F32),
                        pltpu.VMEM((tq, MLA_HEADS * MLA_V), BF16)],
        compiler_params=_params(("parallel", "arbitrary"), 40),
        name="mla_attn_prompt",
    )(q, kext, vt, x, wuv, w_out)


def _mla_attn_sample_kernel(pt_ref, q_ref, knew_ref, lat_hbm, krt_hbm, o_ref,
                            cbuf, rbuf, s_ref, sems, *, t_new, n_pages):
    g = pl.program_id(0)
    slot = g % 2
    n_rows = q_ref.shape[0]
    n_keys = n_pages * PAGE_SIZE
    n_chunks = n_keys // MLA_KEY_CHUNK

    def page_copies(group, slot_):
        copies = []
        for r in range(n_rows):
            for p in range(n_pages):
                page = pt_ref[(group * n_rows + r) * n_pages + p]
                keys = pl.ds(p * PAGE_SIZE, PAGE_SIZE)
                copies.append(pltpu.make_async_copy(lat_hbm.at[page], cbuf.at[slot_, r, keys, :],
                                                    sems.at[0, slot_]))
                copies.append(pltpu.make_async_copy(krt_hbm.at[page], rbuf.at[slot_, r, :, keys],
                                                    sems.at[1, slot_]))
        return copies

    @pl.when(g == 0)
    def _():
        for cp in page_copies(g, slot):
            cp.start()

    @pl.when(g + 1 < pl.num_programs(0))
    def _():
        for cp in page_copies(g + 1, 1 - slot):
            cp.start()

    for cp in page_copies(g, slot):
        cp.wait()

    def past_scores(r, j):
        keys = slice(j * MLA_KEY_CHUNK, (j + 1) * MLA_KEY_CHUNK)
        q = q_ref[r]
        c = cbuf[slot, r, keys, :].astype(BF16)
        rt = rbuf[slot, r, :, keys].astype(BF16)
        s_ref[r, :, keys] = (_dot_nt(q[:, :MLA_KV_LORA], c)
                             + _dot(q[:, MLA_KV_LORA:MLA_KV_LORA + MLA_ROPE], rt))

    for j in range(n_chunks):
        past_scores(0, j)
    for r in range(n_rows):
        kn = knew_ref[r]
        sn = _dot_nt(q_ref[r], kn)
        row_t = lax.broadcasted_iota(jnp.int32, sn.shape, 0) % t_new
        key = lax.broadcasted_iota(jnp.int32, sn.shape, 1)
        s_ref[r, :, n_keys:] = jnp.where((key < t_new) & (key <= row_t), sn, -jnp.inf)
        s = s_ref[r]
        e = jnp.exp(s - jnp.max(s, axis=-1, keepdims=True))
        denom = jnp.sum(e, axis=-1, keepdims=True)
        eb = e.astype(BF16)
        acc = _dot(eb[:, n_keys:], kn[:, :MLA_KV_LORA])
        for j in range(n_chunks):
            keys = slice(j * MLA_KEY_CHUNK, (j + 1) * MLA_KEY_CHUNK)
            acc = acc + _dot(eb[:, keys], cbuf[slot, r, keys, :].astype(BF16))
            if r + 1 < n_rows:
                past_scores(r + 1, j)
        o_ref[r] = acc / denom


def _mla_attn_sample(page_table, q, cache_lat, cache_krt, knew, t_new):
    b, rows, _ = q.shape
    n_pages = page_table.shape[1]
    n_keys = n_pages * PAGE_SIZE
    nr = MLA_SAMPLE_ROWS_PER_STEP
    assert n_keys % MLA_KEY_CHUNK == 0 and b % nr == 0
    grid_spec = pltpu.PrefetchScalarGridSpec(
        num_scalar_prefetch=1,
        grid=(b // nr,),
        in_specs=[pl.BlockSpec((nr, rows, MLA_KEY_DIM), lambda i, pt: (i, 0, 0)),
                  pl.BlockSpec((nr, PAGE_SIZE, MLA_KEY_DIM), lambda i, pt: (i, 0, 0)),
                  pl.BlockSpec(memory_space=pl.ANY),
                  pl.BlockSpec(memory_space=pl.ANY)],
        out_specs=pl.BlockSpec((nr, rows, MLA_KV_LORA), lambda i, pt: (i, 0, 0)),
        scratch_shapes=[pltpu.VMEM((2, nr, n_keys, MLA_KV_LORA), F32),
                        pltpu.VMEM((2, nr, MLA_ROPE, n_keys), F32),
                        pltpu.VMEM((nr, rows, n_keys + PAGE_SIZE), F32),
                        pltpu.SemaphoreType.DMA((2, 2))],
    )
    return pl.pallas_call(
        functools.partial(_mla_attn_sample_kernel, t_new=t_new, n_pages=n_pages),
        grid_spec=grid_spec,
        out_shape=jax.ShapeDtypeStruct((b, rows, MLA_KV_LORA), F32),
        compiler_params=_params(("arbitrary",), 56),
        name="mla_attn_sample",
    )(page_table.reshape(-1), q, knew, cache_lat, cache_krt)


def _mla_out_kernel(o_ref, wuv_ref, wout_ref, x_ref, y_ref, cat_ref):
    for hh in range(MLA_HEADS):
        cat_ref[:, hh * MLA_V:(hh + 1) * MLA_V] = _dot(o_ref[0, hh], wuv_ref[hh]).astype(BF16)
    y_ref[0] = x_ref[0] + _dot(cat_ref[...], wout_ref[...])


def _mla_out(o_lat, wuv, w_out, x, tm):
    nb, t, _ = x.shape
    assert t % tm == 0
    return pl.pallas_call(
        _mla_out_kernel,
        grid=(nb, t // tm),
        in_specs=[pl.BlockSpec((1, MLA_HEADS, tm, MLA_KV_LORA), lambda i, j: (i, 0, j, 0)),
                  _resident(wuv.shape), _resident(w_out.shape),
                  pl.BlockSpec((1, tm, D_MODEL), lambda i, j: (i, j, 0))],
        out_specs=pl.BlockSpec((1, tm, D_MODEL), lambda i, j: (i, j, 0)),
        out_shape=jax.ShapeDtypeStruct(x.shape, F32),
        scratch_shapes=[pltpu.VMEM((tm, MLA_HEADS * MLA_V), BF16)],
        compiler_params=_params(("parallel", "parallel"), 32),
        name="mla_out",
    )(o_lat, wuv, w_out, x)


def _pool_prompt_kernel(x_ref, xp_ref, g_ref, w_ref, sc_ref, o_ref, hl_ref, ext_ref, tmp_ref):
    j = pl.program_id(1)
    x = x_ref[0]
    tm = x.shape[0]
    h = _rms(x, g_ref[...])
    hp = _rms(xp_ref[0], g_ref[...])
    ext_ref[0:POOL_PREV, :] = jnp.where(j == 0, 0.0, hp)
    ext_ref[POOL_PREV:, :] = h
    tmp_ref[0:POOL_ALIGN, :] = jnp.zeros((POOL_ALIGN, D_MODEL), F32)
    n = POOL_PREV + tm - POOL_ALIGN
    pos = (j * tm + lax.broadcasted_iota(jnp.int32, (tm, 1), 0)).astype(F32)
    outs = []
    for gi, w in enumerate(POOL_WINDOWS):
        lanes = slice(gi * POOL_GROUP_DIM, (gi + 1) * POOL_GROUP_DIM)
        src, dst = ext_ref, tmp_ref
        shift = 1
        while shift < w:
            dst[POOL_ALIGN:, lanes] = (src[POOL_ALIGN:, lanes]
                                       + src[POOL_ALIGN - shift:POOL_ALIGN - shift + n, lanes])
            src, dst = dst, src
            shift *= 2
        pooled = src[POOL_PREV:, lanes] / jnp.minimum(pos + 1.0, float(w)) - h[:, lanes]
        outs.append(_dot(pooled.astype(BF16), w_ref[gi]))
    o_ref[0] = x + jnp.concatenate(outs, axis=-1) * sc_ref[...]

    @pl.when(j == pl.num_programs(1) - 1)
    def _():
        hl_ref[0] = h[tm - POOL_TAIL:]


def _pool_prompt(x, gain, w_pool, scale):
    b, t, _ = x.shape
    tm = 512
    assert t % tm == 0 and tm % POOL_PREV == 0
    per_tile = tm // POOL_PREV
    return pl.pallas_call(
        _pool_prompt_kernel,
        grid=(b, t // tm),
        in_specs=[pl.BlockSpec((1, tm, D_MODEL), lambda i, j: (i, j, 0)),
                  pl.BlockSpec((1, POOL_PREV, D_MODEL),
                               lambda i, j: (i, jnp.maximum(j * per_tile - 1, 0), 0)),
                  _resident(gain.shape), _resident(w_pool.shape), _resident(scale.shape)],
        out_specs=[pl.BlockSpec((1, tm, D_MODEL), lambda i, j: (i, j, 0)),
                   pl.BlockSpec((1, POOL_TAIL, D_MODEL), lambda i, j: (i, 0, 0))],
        out_shape=[jax.ShapeDtypeStruct(x.shape, F32),
                   jax.ShapeDtypeStruct((b, POOL_TAIL, D_MODEL), F32)],
        scratch_shapes=[pltpu.VMEM((POOL_PREV + tm, D_MODEL), F32),
                        pltpu.VMEM((POOL_PREV + tm, D_MODEL), F32)],
        compiler_params=_params(("parallel", "arbitrary"), 32),
        name="pool_prompt",
    )(x, x, gain, w_pool, scale)


def _pool_sample_kernel(x_ref, pre_ref, g_ref, w_ref, sc_ref, o_ref, h_ref, *, pos0):
    t_len = x_ref.shape[0]
    hs = [_rms(x_ref[t], g_ref[...]) for t in range(t_len)]
    for t in range(t_len):
        h_ref[t] = hs[t]

    def ext(e, lanes):
        return pre_ref[e, :, lanes] if e < POOL_BUF else hs[e - POOL_BUF][:, lanes]

    for t in range(t_len):
        outs = []
        for gi, w in enumerate(POOL_WINDOWS):
            lanes = slice(gi * POOL_GROUP_DIM, (gi + 1) * POOL_GROUP_DIM)
            acc = hs[t][:, lanes]
            for k in range(1, w):
                acc = acc + ext(POOL_BUF + t - k, lanes)
            pooled = acc / min(pos0 + t + 1.0, float(w)) - hs[t][:, lanes]
            outs.append(_dot(pooled.astype(BF16), w_ref[gi]))
        o_ref[t] = x_ref[t] + jnp.concatenate(outs, axis=-1) * sc_ref[...]


def _pool_sample(x_t, prefix_t, gain, w_pool, scale, pos0):
    vmem = pl.BlockSpec(memory_space=pltpu.VMEM)
    return pl.pallas_call(
        functools.partial(_pool_sample_kernel, pos0=pos0),
        in_specs=[vmem] * 5,
        out_specs=[vmem, vmem],
        out_shape=[jax.ShapeDtypeStruct(x_t.shape, F32), jax.ShapeDtypeStruct(x_t.shape, F32)],
        compiler_params=_params(None, 32),
        name="pool_sample",
    )(x_t, prefix_t, gain, w_pool, scale)


def _rope_tables(pos, half):
    inv = ROPE_BASE ** (-jnp.arange(half, dtype=F32) / half)
    ang = pos.astype(F32)[:, None] * inv[None, :]
    return jnp.cos(ang), jnp.sin(ang)


def _mla_rope_tables(pos):
    cos, sin = _rope_tables(pos, MLA_ROPE // 2)
    zeros = jnp.zeros((pos.shape[0], 128 - MLA_ROPE), F32)
    return (jnp.concatenate([cos, cos, zeros], axis=-1),
            jnp.concatenate([-sin, sin, zeros], axis=-1))


def _swap_halves(w):
    half = w.shape[-1] // 2
    return jnp.concatenate([w[..., half:], w[..., :half]], axis=-1)


def _pad_lanes(w, width):
    return jnp.pad(w, [(0, 0)] * (w.ndim - 1) + [(0, width - w.shape[-1])])


def _row(v):
    return v.reshape(1, -1).astype(F32)


def kernel(x_prompt, x_sample, state_ret, cache_mla_latent, cache_mla_krope, page_table, state_pool,
           norm_mix, norm_ffn, norm_final,
           ret_w_in, ret_gn_gain, ret_w_out,
           cm_w_in, cm_ln_gain, cm_w_spatial, cm_b_spatial, cm_w_out,
           mla_w_down, mla_q_norm, mla_kv_norm, mla_w_uq, mla_w_uk, mla_w_uv, mla_w_out,
           pool_w, pool_scale,
           ffn_w_gate_up, ffn_w_down):
    bp, tp, _ = x_prompt.shape
    bs, ts, _ = x_sample.shape
    xp = x_prompt
    xs = x_sample.reshape(bs * ts, D_MODEL)
    pos_p = jnp.arange(tp)
    pos_s = PAST_LEN + jnp.arange(ts)

    def ffn(xp_, xs_, i, final):
        yp, ys = _ffn(xp_.reshape(-1, D_MODEL), xs_, _row(norm_ffn[i]), ffn_w_gate_up, ffn_w_down, i,
                      _row(norm_final), final)
        return yp.reshape(xp_.shape), ys

    log_g = jnp.log1p(-jnp.exp2(-5.0 - jnp.arange(RET_HEADS, dtype=F32)))
    w_in = ret_w_in[0]
    w_out = ret_w_out[0]
    gn_gain = _row(ret_gn_gain[0])
    cos_p, sin_p = _rope_tables(pos_p, RET_QK_DIM // 2)
    cos_s, sin_s = _rope_tables(pos_s, RET_QK_DIM // 2)
    xp, ret_state_p = _ret_prompt(xp, _row(norm_mix[0]), w_in, cos_p, sin_p,
                                  _ret_tables(float(RET_PROMPT_CHUNK), RET_PROMPT_CHUNK, log_g),
                                  gn_gain, w_out)
    proj_s = _norm_matmul(xs, _row(norm_mix[0]), w_in, 1024)
    y_s, ret_state_s = _ret_sample(proj_s, state_ret[0], cos_s, sin_s,
                                   _ret_tables(float(ts), RET_CHUNK, log_g), gn_gain)
    xs = _matmul_residual(y_s, w_out, xs)
    xp, xs = ffn(xp, xs, 0, False)

    cw_in = cm_w_in[0]
    cw_out = cm_w_out[0]
    bias_full = jnp.repeat(jnp.transpose(cm_b_spatial[0]), CM_GROUP_DIM, axis=1)
    xp2, cm_v_p = _cm_prompt(xp.reshape(bp * tp, D_MODEL), _row(norm_mix[1]), cw_in,
                             _row(cm_ln_gain[0]), cm_w_spatial[0], bias_full, cw_out, tp)
    xp = xp2.reshape(bp, tp, D_MODEL)
    xs_t, cm_v_s_t = _cm_sample(xs.reshape(bs, ts, D_MODEL).transpose(1, 0, 2), _row(norm_mix[1]), cw_in,
                                _row(cm_ln_gain[0]),
                                cm_w_spatial[0][:, :ts, :ts].reshape(CM_GROUPS, ts * ts),
                                cm_b_spatial[0][:, :ts], cw_out)
    xs = xs_t.transpose(1, 0, 2).reshape(bs * ts, D_MODEL)
    cm_v_s = cm_v_s_t.transpose(1, 0, 2)
    xp, xs = ffn(xp, xs, 1, False)

    wd = mla_w_down[0]
    kr_w = wd[:, MLA_Q_LORA + MLA_KV_LORA:]
    wd_ext = jnp.concatenate([wd[:, :MLA_Q_LORA + MLA_KV_LORA], _pad_lanes(kr_w, 128),
                              _pad_lanes(_swap_halves(kr_w), 128)], axis=-1).astype(BF16)
    wuq = mla_w_uq[0].reshape(MLA_Q_LORA, MLA_HEADS, MLA_NOPE + MLA_ROPE)
    wuq_rope = wuq[:, :, MLA_NOPE:]
    wuq_ext = jnp.concatenate([
        wuq[:, :, :MLA_NOPE].reshape(MLA_Q_LORA, -1),
        _pad_lanes(wuq_rope, 128).reshape(MLA_Q_LORA, -1),
        _pad_lanes(_swap_halves(wuq_rope), 128).reshape(MLA_Q_LORA, -1)], axis=-1).astype(BF16)
    wuk_t = mla_w_uk[0].transpose(1, 2, 0).astype(BF16)
    wuv = mla_w_uv[0].transpose(1, 0, 2).astype(BF16)
    mw_out = mla_w_out[0].astype(BF16)
    cos2_p, sin2_p = _mla_rope_tables(pos_p)
    cos2_s, sin2_s = _mla_rope_tables(jnp.tile(pos_s, bs))
    gq, gkv = _row(mla_q_norm[0]), _row(mla_kv_norm[0])
    q_p, lat_p, krt_p, kext_p, vt_p = _mla_q(xp, _row(norm_mix[2]), wd_ext, gq, gkv, wuq_ext, wuk_t,
                                             cos2_p, sin2_p, 2 * MLA_ATTN_TQ)
    kr_p = jnp.swapaxes(krt_p, 1, 2)
    xp = _mla_attn_prompt(q_p, kext_p, vt_p, xp, wuv, mw_out)
    ns = bs * ts
    q_s, lat_s, krt_s, kext_s, _ = _mla_q(xs.reshape(1, ns, D_MODEL), _row(norm_mix[2]), wd_ext, gq, gkv,
                                          wuq_ext, wuk_t, cos2_s, sin2_s, ns)
    q_s = q_s.reshape(MLA_HEADS, bs, ts, MLA_KEY_DIM).transpose(1, 0, 2, 3).reshape(bs, MLA_HEADS * ts, MLA_KEY_DIM)
    knew = jnp.pad(kext_s.reshape(bs, ts, MLA_KEY_DIM), ((0, 0), (0, PAGE_SIZE - ts), (0, 0)))
    o_s = _mla_attn_sample(page_table, q_s, cache_mla_latent[0],
                           jnp.swapaxes(cache_mla_krope[0], 1, 2), knew, ts)
    o_s = o_s.reshape(bs, MLA_HEADS, ts, MLA_KV_LORA).transpose(1, 0, 2, 3).reshape(1, MLA_HEADS, ns, MLA_KV_LORA)
    xs = _mla_out(o_s.astype(BF16), wuv, mw_out, xs.reshape(1, ns, D_MODEL), ns).reshape(ns, D_MODEL)
    lat_s = lat_s.reshape(bs, ts, MLA_KV_LORA)
    kr_s = jnp.swapaxes(krt_s, 1, 2).reshape(bs, ts, MLA_ROPE)
    xp, xs = ffn(xp, xs, 2, False)

    pw = pool_w[0].astype(BF16)
    xp, h_last = _pool_prompt(xp, _row(norm_mix[3]), pw, _row(pool_scale[0]))
    pool_state_p = h_last[:, POOL_TAIL - POOL_BUF:]
    xs_t, h_s_t = _pool_sample(xs.reshape(bs, ts, D_MODEL).transpose(1, 0, 2),
                               state_pool[0].transpose(1, 0, 2), _row(norm_mix[3]), pw,
                               _row(pool_scale[0]), float(PAST_LEN))
    xs = xs_t.transpose(1, 0, 2).reshape(bs * ts, D_MODEL)
    pool_state_s = jnp.concatenate([state_pool[0], h_s_t.transpose(1, 0, 2)], axis=1)[:, -POOL_BUF:]
    xp, xs = ffn(xp, xs, 3, True)

    return (xp, xs.reshape(bs, ts, D_MODEL),
            ret_state_p[None], ret_state_s[None],
            cm_v_p.reshape(bp, CM_CHUNK, D_MODEL)[None], cm_v_s[None],
            lat_p[None], kr_p[None], lat_s[None], kr_s[None],
            pool_state_p[None], pool_state_s[None])
```

```python
import functools

import jax
import jax.numpy as jnp
from jax import lax
from jax.experimental import pallas as pl
from jax.experimental.pallas import tpu as pltpu

F32 = jnp.float32
BF16 = jnp.bfloat16

D_MODEL = 1024
NORM_EPS = 1e-6
ROPE_BASE = 10000.0
PAST_LEN = 8192
PAGE_SIZE = 128

RET_HEADS = 4
RET_QK_DIM = 256
RET_V_DIM = 512
RET_CHUNK = 128
RET_PROMPT_CHUNK = 256
RET_STAGE_WIDTH = 512
RET_SAMPLE_ROWS_PER_STEP = 4

CM_CHUNK = 128
CM_GROUPS = 4
CM_GROUP_DIM = 256

MLA_HEADS = 8
MLA_Q_LORA = 384
MLA_KV_LORA = 256
MLA_NOPE = 128
MLA_ROPE = 64
MLA_V = 128
MLA_SCALE = (MLA_NOPE + MLA_ROPE) ** -0.5
MLA_KEY_DIM = MLA_KV_LORA + 128
MLA_KEY_CHUNK = 1024
MLA_SAMPLE_ROWS_PER_STEP = 2
MLA_ATTN_TQ = 256

POOL_WINDOWS = (2, 4, 8, 16)
POOL_GROUP_DIM = 256
POOL_BUF = 15
POOL_PREV = 32
POOL_ALIGN = 8
POOL_TAIL = 16

FFN_HIDDEN = 2816
FFN_CHUNK = 256

V7X_VMEM_BYTES = 64 * 1024 * 1024
MIB = 1024 * 1024


def _params(semantics, vmem_mib):
    assert vmem_mib * MIB < V7X_VMEM_BYTES
    return pltpu.CompilerParams(dimension_semantics=semantics,
                                vmem_limit_bytes=vmem_mib * MIB)


def _resident(shape):
    nd = len(shape)
    return pl.BlockSpec(shape, lambda *_: (0,) * nd, pipeline_mode=pl.Buffered(1))


def _rms(x, g):
    return x * lax.rsqrt(jnp.mean(x * x, axis=-1, keepdims=True) + NORM_EPS) * g


def _dot(a, b):
    return jnp.dot(a, b, preferred_element_type=F32)


def _wdot(a, w):
    return jnp.dot(a, w.astype(BF16), preferred_element_type=F32)


def _dot_nt(a, b):
    return lax.dot_general(a, b, (((1,), (1,)), ((), ())), preferred_element_type=F32)


def _ffn_kernel(xp_ref, xs_ref, g_ref, wgu_hbm, wd_hbm, gf_ref, op_ref, os_ref,
                wgu_ref, wd_ref, sems, *, final, layer):
    n_chunks = FFN_HIDDEN // FFN_CHUNK

    def chunk_copies(j):
        cols = pl.ds(j * FFN_CHUNK, FFN_CHUNK)
        ucols = pl.ds(FFN_HIDDEN + j * FFN_CHUNK, FFN_CHUNK)
        return (pltpu.make_async_copy(wgu_hbm.at[layer, :, cols], wgu_ref.at[:, cols], sems.at[0, j]),
                pltpu.make_async_copy(wgu_hbm.at[layer, :, ucols], wgu_ref.at[:, ucols], sems.at[1, j]),
                pltpu.make_async_copy(wd_hbm.at[layer, cols, :], wd_ref.at[cols, :], sems.at[2, j]))

    def tile(x_ref, o_ref, wait_for_weights):
        x = x_ref[...]
        h = _rms(x, g_ref[...]).astype(BF16)
        acc = x
        for j in range(n_chunks):
            if wait_for_weights:
                for cp in chunk_copies(j):
                    cp.wait()
            cols = slice(j * FFN_CHUNK, (j + 1) * FFN_CHUNK)
            ucols = slice(FFN_HIDDEN + j * FFN_CHUNK, FFN_HIDDEN + (j + 1) * FFN_CHUNK)
            g = _dot(h, wgu_ref[:, cols].astype(BF16))
            u = _dot(h, wgu_ref[:, ucols].astype(BF16))
            a = (g * jax.nn.sigmoid(g)) * u
            acc = acc + _dot(a.astype(BF16), wd_ref[cols, :].astype(BF16))
        o_ref[...] = _rms(acc, gf_ref[...]) if final else acc

    i = pl.program_id(0)
    is_prompt = i < pl.num_programs(0) - 1

    @pl.when(i == 0)
    def _():
        for j in range(n_chunks):
            for cp in chunk_copies(j):
                cp.start()
        tile(xp_ref, op_ref, True)

    @pl.when((i > 0) & is_prompt)
    def _():
        tile(xp_ref, op_ref, False)

    @pl.when(jnp.logical_not(is_prompt))
    def _():
        tile(xs_ref, os_ref, False)


def _ffn(xp, xs, gain, wgu_all, wd_all, layer, final_gain, final):
    tm = xs.shape[0]
    assert xp.shape[0] % tm == 0 and xp.shape[0] >= 2 * tm
    n_prompt = xp.shape[0] // tm
    hbm = pl.BlockSpec(memory_space=pl.ANY)
    prompt_tile = pl.BlockSpec((tm, D_MODEL), lambda i: (jnp.minimum(i, n_prompt - 1), 0))
    sample_tile = pl.BlockSpec((tm, D_MODEL), lambda i: (0, 0))
    return pl.pallas_call(
        functools.partial(_ffn_kernel, final=final, layer=layer),
        grid=(n_prompt + 1,),
        in_specs=[prompt_tile, sample_tile, _resident(gain.shape), hbm, hbm,
                  _resident(final_gain.shape)],
        out_specs=[prompt_tile, sample_tile],
        out_shape=[jax.ShapeDtypeStruct(xp.shape, F32), jax.ShapeDtypeStruct(xs.shape, F32)],
        scratch_shapes=[pltpu.VMEM(wgu_all.shape[1:], F32), pltpu.VMEM(wd_all.shape[1:], F32),
                        pltpu.SemaphoreType.DMA((3, FFN_HIDDEN // FFN_CHUNK))],
        compiler_params=_params(("arbitrary",), 58),
        name="ffn",
    )(xp, xs, gain, wgu_all, wd_all, final_gain)


def _norm_matmul_kernel(x_ref, g_ref, w_ref, o_ref):
    h = _rms(x_ref[...], g_ref[...]).astype(BF16)
    o_ref[...] = _wdot(h, w_ref[...])


def _norm_matmul(x, gain, w, tn):
    n, dout = x.shape[0], w.shape[1]
    assert dout % tn == 0
    return pl.pallas_call(
        _norm_matmul_kernel,
        grid=(dout // tn,),
        in_specs=[pl.BlockSpec((n, D_MODEL), lambda j: (0, 0)),
                  pl.BlockSpec(gain.shape, lambda j: (0, 0)),
                  pl.BlockSpec((D_MODEL, tn), lambda j: (0, j))],
        out_specs=pl.BlockSpec((n, tn), lambda j: (0, j)),
        out_shape=jax.ShapeDtypeStruct((n, dout), F32),
        compiler_params=_params(("parallel",), 32),
        name="norm_matmul",
    )(x, gain, w)


def _matmul_residual_kernel(y_ref, w_ref, x_ref, o_ref):
    o_ref[...] = x_ref[...] + _wdot(y_ref[...].astype(BF16), w_ref[...])


def _matmul_residual(y, w, x):
    n = x.shape[0]
    tm = min(n, 512)
    assert n % tm == 0
    return pl.pallas_call(
        _matmul_residual_kernel,
        grid=(n // tm,),
        in_specs=[pl.BlockSpec((tm, y.shape[1]), lambda i: (i, 0)),
                  _resident(w.shape),
                  pl.BlockSpec((tm, D_MODEL), lambda i: (i, 0))],
        out_specs=pl.BlockSpec((tm, D_MODEL), lambda i: (i, 0)),
        out_shape=jax.ShapeDtypeStruct(x.shape, F32),
        compiler_params=_params(("parallel",), 32),
        name="matmul_residual",
    )(y, w, x)


def _rope_halves(x, cos, sin):
    half = cos.shape[-1]
    x1, x2 = x[:, :half], x[:, half:]
    return jnp.concatenate([x1 * cos - x2 * sin, x1 * sin + x2 * cos], axis=-1)


def _ret_chunk(q, k, v, s, dmat, qdec, kdec, sdec):
    vb = v.astype(BF16)
    scores = _dot_nt(q.astype(BF16), k.astype(BF16)) * dmat
    o = _dot(scores.astype(BF16), vb) + _dot((q * qdec).astype(BF16), s.astype(BF16))
    s_new = sdec * s + _dot((k * kdec).T.astype(BF16), vb)
    return o, s_new


def _group_norm_gate(o, gate, gain):
    mu = jnp.mean(o, axis=-1, keepdims=True)
    oc = o - mu
    var = jnp.mean(oc * oc, axis=-1, keepdims=True)
    on = oc * lax.rsqrt(var + NORM_EPS)
    return (gate * jax.nn.sigmoid(gate)) * on * gain


def _ret_prompt_kernel(x_ref, g_ref, win_hbm, cos_ref, sin_ref, dmat_ref, qdec_ref,
                       kdec_ref, sdec_ref, gng_ref, wout_hbm, o_ref, st_ref,
                       s_ref, y_ref, win_ref, wout_ref, stage_in, stage_out, sems):
    j = pl.program_id(1)

    def load_weight(w_hbm, w_ref, stage, sem_row, axis):
        width = stage.shape[1 + axis]
        n = w_hbm.shape[axis] // width

        def window(ref, c):
            return ref.at[pl.ds(c * width, width), :] if axis == 0 else ref.at[:, pl.ds(c * width, width)]

        def copy(c):
            return pltpu.make_async_copy(window(w_hbm, c), stage.at[c % 2], sems.at[sem_row, c % 2])

        copy(0).start()
        for c in range(n):
            if c + 1 < n:
                copy(c + 1).start()
            copy(c).wait()
            window(w_ref, c)[...] = stage[c % 2].astype(BF16)

    @pl.when((pl.program_id(0) == 0) & (j == 0))
    def _():
        load_weight(win_hbm, win_ref, stage_in, 0, 1)
        load_weight(wout_hbm, wout_ref, stage_out, 1, 0)

    @pl.when(j == 0)
    def _():
        s_ref[...] = jnp.zeros_like(s_ref)

    x = x_ref[0]
    tm = x.shape[0]
    h = _rms(x, g_ref[...]).astype(BF16)
    for hh in range(RET_HEADS):
        qs = slice(hh * RET_QK_DIM, (hh + 1) * RET_QK_DIM)
        ks = slice(D_MODEL + hh * RET_QK_DIM, D_MODEL + (hh + 1) * RET_QK_DIM)
        vs = slice(2 * D_MODEL + hh * RET_V_DIM, 2 * D_MODEL + (hh + 1) * RET_V_DIM)
        gs = slice(4 * D_MODEL + hh * RET_V_DIM, 4 * D_MODEL + (hh + 1) * RET_V_DIM)
        q = _rope_halves(_wdot(h, win_ref[:, qs]), cos_ref[...], sin_ref[...]) * (RET_QK_DIM ** -0.5)
        k = _rope_halves(_wdot(h, win_ref[:, ks]), cos_ref[...], sin_ref[...])
        v = _wdot(h, win_ref[:, vs])
        gate = _wdot(h, win_ref[:, gs])
        gain = gng_ref[:, hh * RET_V_DIM:(hh + 1) * RET_V_DIM]
        chunk = dmat_ref.shape[1]
        for c in range(tm // chunk):
            rows = slice(c * chunk, (c + 1) * chunk)
            o, s_new = _ret_chunk(q[rows], k[rows], v[rows], s_ref[hh], dmat_ref[hh],
                                  qdec_ref[hh], kdec_ref[hh], sdec_ref[hh])
            s_ref[hh] = s_new
            y_ref[rows, hh * RET_V_DIM:(hh + 1) * RET_V_DIM] = (
                _group_norm_gate(o, gate[rows], gain).astype(BF16))
    o_ref[0] = x + _wdot(y_ref[...], wout_ref[...])

    @pl.when(j == pl.num_programs(1) - 1)
    def _():
        st_ref[0] = s_ref[...]


def _ret_tables(length, chunk, log_g):
    idx = jnp.arange(chunk, dtype=F32)
    valid = idx < length
    rel = idx[:, None] - idx[None, :]
    ok = (rel >= 0) & valid[:, None] & valid[None, :]
    dmat = jnp.where(ok[None], jnp.exp(jnp.maximum(rel, 0.0)[None] * log_g[:, None, None]), 0.0)
    qd = jnp.where(valid[None], jnp.exp((idx + 1.0)[None, :] * log_g[:, None]), 0.0)
    kd = jnp.where(valid[None], jnp.exp((length - 1.0 - idx)[None, :] * log_g[:, None]), 0.0)
    qdec = jnp.broadcast_to(qd[:, :, None], (RET_HEADS, chunk, RET_QK_DIM))
    kdec = jnp.broadcast_to(kd[:, :, None], (RET_HEADS, chunk, RET_QK_DIM))
    sdec = jnp.broadcast_to(jnp.exp(length * log_g)[:, None, None], (RET_HEADS, 1, RET_V_DIM))
    return dmat, qdec, kdec, sdec


def _ret_prompt(x, gain, w_in, cos, sin, tables, gn_gain, w_out):
    b, t, _ = x.shape
    tm = 512
    assert t % tm == 0
    dmat, qdec, kdec, sdec = tables
    hbm = pl.BlockSpec(memory_space=pl.ANY)
    return pl.pallas_call(
        _ret_prompt_kernel,
        grid=(b, t // tm),
        in_specs=[pl.BlockSpec((1, tm, D_MODEL), lambda i, j: (i, j, 0)),
                  _resident(gain.shape), hbm,
                  pl.BlockSpec((tm, RET_QK_DIM // 2), lambda i, j: (j, 0)),
                  pl.BlockSpec((tm, RET_QK_DIM // 2), lambda i, j: (j, 0)),
                  _resident(dmat.shape), _resident(qdec.shape), _resident(kdec.shape),
                  _resident(sdec.shape), _resident(gn_gain.shape), hbm],
        out_specs=[pl.BlockSpec((1, tm, D_MODEL), lambda i, j: (i, j, 0)),
                   pl.BlockSpec((1, RET_HEADS, RET_QK_DIM, RET_V_DIM), lambda i, j: (i, 0, 0, 0))],
        out_shape=[jax.ShapeDtypeStruct(x.shape, F32),
                   jax.ShapeDtypeStruct((b, RET_HEADS, RET_QK_DIM, RET_V_DIM), F32)],
        scratch_shapes=[pltpu.VMEM((RET_HEADS, RET_QK_DIM, RET_V_DIM), F32),
                        pltpu.VMEM((tm, RET_HEADS * RET_V_DIM), BF16),
                        pltpu.VMEM(w_in.shape, BF16), pltpu.VMEM(w_out.shape, BF16),
                        pltpu.VMEM((2, w_in.shape[0], RET_STAGE_WIDTH), F32),
                        pltpu.VMEM((2, RET_STAGE_WIDTH, w_out.shape[1]), F32),
                        pltpu.SemaphoreType.DMA((2, 2))],
        compiler_params=_params(("arbitrary", "arbitrary"), 56),
        name="ret_prompt",
    )(x, gain, w_in, cos, sin, dmat, qdec, kdec, sdec, gn_gain, w_out)


def _ret_sample_kernel(p_ref, s0_ref, cos_ref, sin_ref, dmat_ref, qdec_ref, kdec_ref,
                       sdec_ref, gng_ref, y_ref, st_ref, qp_ref, kp_ref, vp_ref):
    @pl.when(pl.program_id(0) == 0)
    def _():
        qp_ref[...] = jnp.zeros_like(qp_ref)
        kp_ref[...] = jnp.zeros_like(kp_ref)
        vp_ref[...] = jnp.zeros_like(vp_ref)

    t = cos_ref.shape[0]
    for bb in range(s0_ref.shape[0]):
        rows = slice(bb * t, (bb + 1) * t)
        for hh in range(RET_HEADS):
            qs = slice(hh * RET_QK_DIM, (hh + 1) * RET_QK_DIM)
            ks = slice(D_MODEL + hh * RET_QK_DIM, D_MODEL + (hh + 1) * RET_QK_DIM)
            vs = slice(2 * D_MODEL + hh * RET_V_DIM, 2 * D_MODEL + (hh + 1) * RET_V_DIM)
            gs = slice(4 * D_MODEL + hh * RET_V_DIM, 4 * D_MODEL + (hh + 1) * RET_V_DIM)
            qp_ref[0:t, :] = (_rope_halves(p_ref[rows, qs], cos_ref[...], sin_ref[...])
                              * (RET_QK_DIM ** -0.5))
            kp_ref[0:t, :] = _rope_halves(p_ref[rows, ks], cos_ref[...], sin_ref[...])
            vp_ref[0:t, :] = p_ref[rows, vs]
            o, s_new = _ret_chunk(qp_ref[...], kp_ref[...], vp_ref[...], s0_ref[bb, hh], dmat_ref[hh],
                                  qdec_ref[hh], kdec_ref[hh], sdec_ref[hh])
            st_ref[bb, hh] = s_new
            gain = gng_ref[:, hh * RET_V_DIM:(hh + 1) * RET_V_DIM]
            y_ref[rows, hh * RET_V_DIM:(hh + 1) * RET_V_DIM] = _group_norm_gate(o[0:t], p_ref[rows, gs], gain)


def _ret_sample(proj, s0, cos, sin, tables, gn_gain):
    b = s0.shape[0]
    t = cos.shape[0]
    nb = RET_SAMPLE_ROWS_PER_STEP
    assert b % nb == 0 and proj.shape[0] == b * t
    dmat, qdec, kdec, sdec = tables
    state_spec = pl.BlockSpec((nb, RET_HEADS, RET_QK_DIM, RET_V_DIM), lambda i: (i, 0, 0, 0))
    return pl.pallas_call(
        _ret_sample_kernel,
        grid=(b // nb,),
        in_specs=[pl.BlockSpec((nb * t, proj.shape[1]), lambda i: (i, 0)),
                  state_spec,
                  _resident(cos.shape), _resident(sin.shape),
                  _resident(dmat.shape), _resident(qdec.shape), _resident(kdec.shape),
                  _resident(sdec.shape), _resident(gn_gain.shape)],
        out_specs=[pl.BlockSpec((nb * t, RET_HEADS * RET_V_DIM), lambda i: (i, 0)),
                   state_spec],
        out_shape=[jax.ShapeDtypeStruct((b * t, RET_HEADS * RET_V_DIM), F32),
                   jax.ShapeDtypeStruct(s0.shape, F32)],
        scratch_shapes=[pltpu.VMEM((RET_CHUNK, RET_QK_DIM), F32),
                        pltpu.VMEM((RET_CHUNK, RET_QK_DIM), F32),
                        pltpu.VMEM((RET_CHUNK, RET_V_DIM), F32)],
        compiler_params=_params(("arbitrary",), 48),
        name="ret_sample",
    )(proj, s0, cos, sin, dmat, qdec, kdec, sdec, gn_gain)


def _gelu_tanh(x):
    return x * (0.5 * (1.0 + jnp.tanh(0.7978845608028654 * (x + 0.044715 * (x * x * x)))))


def _layer_norm(v, gain):
    mu = jnp.mean(v, axis=-1, keepdims=True)
    vc = v - mu
    var = jnp.mean(vc * vc, axis=-1, keepdims=True)
    return vc * lax.rsqrt(var + NORM_EPS) * gain


def _cm_prompt_kernel(x_ref, g_ref, win_ref, lng_ref, ws_ref, bias_ref, wout_ref,
                      o_ref, v_ref, z_ref):
    x = x_ref[...]
    tm = x.shape[0]
    h = _rms(x, g_ref[...]).astype(BF16)
    u = _gelu_tanh(_wdot(h, win_ref[:, :D_MODEL]))
    v = _layer_norm(_gelu_tanh(_wdot(h, win_ref[:, D_MODEL:])), lng_ref[...])
    v_ref[...] = v[tm - CM_CHUNK:]
    vb = v.astype(BF16)
    row = lax.broadcasted_iota(jnp.int32, (CM_CHUNK, CM_CHUNK), 0)
    col = lax.broadcasted_iota(jnp.int32, (CM_CHUNK, CM_CHUNK), 1)
    for gi in range(CM_GROUPS):
        lanes = slice(gi * CM_GROUP_DIM, (gi + 1) * CM_GROUP_DIM)
        w = jnp.where(row >= col, ws_ref[gi], 0.0).astype(BF16)
        for c in range(tm // CM_CHUNK):
            rows = slice(c * CM_CHUNK, (c + 1) * CM_CHUNK)
            mixed = _dot(w, vb[rows, lanes]) + bias_ref[:, lanes]
            z_ref[rows, lanes] = (u[rows, lanes] * mixed).astype(BF16)
    o_ref[...] = x + _wdot(z_ref[...], wout_ref[...])


def _cm_prompt(x, gain, w_in, ln_gain, w_s, bias_full, w_out, seq):
    n = x.shape[0]
    tm = 1024
    assert seq % tm == 0 and n % seq == 0
    per_seq = seq // tm
    return pl.pallas_call(
        _cm_prompt_kernel,
        grid=(n // tm,),
        in_specs=[pl.BlockSpec((tm, D_MODEL), lambda i: (i, 0)),
                  _resident(gain.shape), _resident(w_in.shape), _resident(ln_gain.shape),
                  _resident(w_s.shape), _resident(bias_full.shape), _resident(w_out.shape)],
        out_specs=[pl.BlockSpec((tm, D_MODEL), lambda i: (i, 0)),
                   pl.BlockSpec((CM_CHUNK, D_MODEL), lambda i: (i // per_seq, 0))],
        out_shape=[jax.ShapeDtypeStruct(x.shape, F32),
                   jax.ShapeDtypeStruct((n // seq * CM_CHUNK, D_MODEL), F32)],
        scratch_shapes=[pltpu.VMEM((tm, D_MODEL), BF16)],
        compiler_params=_params(("arbitrary",), 40),
        name="cm_prompt",
    )(x, gain, w_in, ln_gain, w_s, bias_full, w_out)


def _cm_sample_kernel(ws_ref, bs_ref, x_ref, g_ref, win_ref, lng_ref, wout_ref, o_ref, v_ref):
    t_len = x_ref.shape[0]
    us, vs = [], []
    for t in range(t_len):
        h = _rms(x_ref[t], g_ref[...]).astype(BF16)
        us.append(_gelu_tanh(_wdot(h, win_ref[:, :D_MODEL])))
        v = _layer_norm(_gelu_tanh(_wdot(h, win_ref[:, D_MODEL:])), lng_ref[...])
        v_ref[t] = v
        vs.append(v)
    for t in range(t_len):
        parts = []
        for gi in range(CM_GROUPS):
            lanes = slice(gi * CM_GROUP_DIM, (gi + 1) * CM_GROUP_DIM)
            mixed = jnp.full_like(vs[t][:, lanes], bs_ref[gi, t])
            for s in range(t + 1):
                mixed = mixed + ws_ref[gi, t * t_len + s] * vs[s][:, lanes]
            parts.append(us[t][:, lanes] * mixed)
        z = jnp.concatenate(parts, axis=-1).astype(BF16)
        o_ref[t] = x_ref[t] + _wdot(z, wout_ref[...])


def _cm_sample(x_t, gain, w_in, ln_gain, w_s_small, b_s_small, w_out):
    smem = pl.BlockSpec(memory_space=pltpu.SMEM)
    vmem = pl.BlockSpec(memory_space=pltpu.VMEM)
    return pl.pallas_call(
        _cm_sample_kernel,
        in_specs=[smem, smem, vmem, vmem, vmem, vmem, vmem],
        out_specs=[vmem, vmem],
        out_shape=[jax.ShapeDtypeStruct(x_t.shape, F32), jax.ShapeDtypeStruct(x_t.shape, F32)],
        compiler_params=_params(None, 32),
        name="cm_sample",
    )(w_s_small, b_s_small, x_t, gain, w_in, ln_gain, w_out)


def _mla_q_kernel(x_ref, g_ref, wd_ref, gq_ref, gkv_ref, wuq_ref, wuk_ref, cos_ref, sin_ref,
                  q_ref, c_ref, krt_ref, kext_ref, vt_ref):
    h = _rms(x_ref[0], g_ref[...]).astype(BF16)
    d = _dot(h, wd_ref[...])
    cq = _rms(d[:, :MLA_Q_LORA], gq_ref[...]).astype(BF16)
    ckv = _rms(d[:, MLA_Q_LORA:MLA_Q_LORA + MLA_KV_LORA], gkv_ref[...])
    cos, sin = cos_ref[...], sin_ref[...]
    base = MLA_Q_LORA + MLA_KV_LORA
    krp = d[:, base:base + 128] * cos + d[:, base + 128:base + 256] * sin
    c_ref[0] = ckv
    krt_ref[0] = krp.T[:MLA_ROPE]
    kext_ref[0] = jnp.concatenate([ckv.astype(BF16), krp.astype(BF16)], axis=-1)
    vt_ref[0, 0] = ckv.T.astype(BF16)
    nope_w = MLA_HEADS * MLA_NOPE
    for hh in range(MLA_HEADS):
        qn = _dot(cq, wuq_ref[:, hh * MLA_NOPE:(hh + 1) * MLA_NOPE])
        raw = _dot(cq, wuq_ref[:, nope_w + hh * 128:nope_w + (hh + 1) * 128])
        rot = _dot(cq, wuq_ref[:, 2 * nope_w + hh * 128:2 * nope_w + (hh + 1) * 128])
        ql = _dot(qn.astype(BF16), wuk_ref[hh]) * MLA_SCALE
        qr = (raw * cos + rot * sin) * MLA_SCALE
        q_ref[0, hh] = jnp.concatenate([ql.astype(BF16), qr.astype(BF16)], axis=-1)


def _mla_q(x, gain, wd_ext, g_q, g_kv, wuq_ext, wuk_t, cos2, sin2, tm):
    nb, t, _ = x.shape
    assert t % tm == 0
    return pl.pallas_call(
        _mla_q_kernel,
        grid=(nb, t // tm),
        in_specs=[pl.BlockSpec((1, tm, D_MODEL), lambda i, j: (i, j, 0)),
                  _resident(gain.shape), _resident(wd_ext.shape), _resident(g_q.shape),
                  _resident(g_kv.shape), _resident(wuq_ext.shape), _resident(wuk_t.shape),
                  pl.BlockSpec((tm, 128), lambda i, j: (j, 0)),
                  pl.BlockSpec((tm, 128), lambda i, j: (j, 0))],
        out_specs=[pl.BlockSpec((1, MLA_HEADS, tm, MLA_KEY_DIM), lambda i, j: (i, 0, j, 0)),
                   pl.BlockSpec((1, tm, MLA_KV_LORA), lambda i, j: (i, j, 0)),
                   pl.BlockSpec((1, MLA_ROPE, tm), lambda i, j: (i, 0, j)),
                   pl.BlockSpec((1, tm, MLA_KEY_DIM), lambda i, j: (i, j, 0)),
                   pl.BlockSpec((1, 1, MLA_KV_LORA, tm), lambda i, j: (i, j, 0, 0))],
        out_shape=[jax.ShapeDtypeStruct((nb, MLA_HEADS, t, MLA_KEY_DIM), BF16),
                   jax.ShapeDtypeStruct((nb, t, MLA_KV_LORA), F32),
                   jax.ShapeDtypeStruct((nb, MLA_ROPE, t), F32),
                   jax.ShapeDtypeStruct((nb, t, MLA_KEY_DIM), BF16),
                   jax.ShapeDtypeStruct((nb, t // tm, MLA_KV_LORA, tm), BF16)],
        compiler_params=_params(("parallel", "parallel"), 40),
        name="mla_q",
    )(x, gain, wd_ext, g_q, g_kv, wuq_ext, wuk_t, cos2, sin2)


def _mla_attn_prompt_kernel(q_ref, k_ref, vt_ref, x_ref, wuv_ref, wout_ref, y_ref,
                            m_ref, l_ref, acc_ref, sa_ref, sb_ref, cat_ref, *, tq, n_tiles):
    tk = 2 * tq
    cols = MLA_HEADS * tq
    score_refs = (sa_ref, sb_ref)

    def scores(step, dst_ref):
        kb, n_keys, first_masked_key = step
        q = q_ref[0].reshape(cols, MLA_KEY_DIM)
        st = _dot_nt(k_ref[0, kb * tk:kb * tk + n_keys, :], q)
        if first_masked_key is not None:
            key_idx = lax.broadcasted_iota(jnp.int32, (n_keys, cols), 0) - first_masked_key
            qry_idx = lax.broadcasted_iota(jnp.int32, (n_keys, cols), 1) & (tq - 1)
            st = jnp.where(key_idx <= qry_idx, st, -jnp.inf)
        dst_ref[0:n_keys, :] = st

    def accumulate(step, src_ref):
        kb, n_keys, _ = step
        st = src_ref[0:n_keys, :]
        m_old = m_ref[...]
        m_new = jnp.maximum(m_old, jnp.max(st, axis=0, keepdims=True))
        p = jnp.exp(st - m_new)
        alpha = jnp.exp(m_old - m_new)
        l_ref[...] = alpha * l_ref[...] + jnp.sum(p, axis=0, keepdims=True)
        acc_ref[...] = alpha * acc_ref[...] + _dot(vt_ref[0, kb, :, :n_keys], p.astype(BF16))
        m_ref[...] = m_new

    def query_tile(qi):
        steps = [(kb, tk, None) for kb in range(qi // 2)]
        steps.append((qi // 2, tk, tq) if qi % 2 else (qi // 2, tq, 0))
        m_ref[...] = jnp.full_like(m_ref, -jnp.inf)
        l_ref[...] = jnp.zeros_like(l_ref)
        acc_ref[...] = jnp.zeros_like(acc_ref)
        scores(steps[0], score_refs[0])
        for i, step in enumerate(steps):
            if i + 1 < len(steps):
                scores(steps[i + 1], score_refs[(i + 1) % 2])
            accumulate(step, score_refs[i % 2])
    for qi in range(n_tiles):
        pl.when(pl.program_id(0) == qi)(functools.partial(query_tile, qi))

    o = (acc_ref[...] / l_ref[...]).T.astype(BF16)
    for hh in range(MLA_HEADS):
        cat_ref[:, hh * MLA_V:(hh + 1) * MLA_V] = _dot(o[hh * tq:(hh + 1) * tq],
                                                       wuv_ref[hh]).astype(BF16)
    y_ref[0] = x_ref[0] + _dot(cat_ref[...], wout_ref[...])


def _mla_attn_prompt(q, kext, vt, x, wuv, w_out):
    b, _, t, _ = q.shape
    tq = vt.shape[-1] // 2
    assert t % (2 * tq) == 0 and tq & (tq - 1) == 0
    return pl.pallas_call(
        functools.partial(_mla_attn_prompt_kernel, tq=tq, n_tiles=t // tq),
        grid=(t // tq, b),
        in_specs=[pl.BlockSpec((1, MLA_HEADS, tq, MLA_KEY_DIM), lambda j, i: (i, 0, j, 0)),
                  pl.BlockSpec((1, t, MLA_KEY_DIM), lambda j, i: (i, 0, 0)),
                  pl.BlockSpec((1, t // (2 * tq), MLA_KV_LORA, 2 * tq), lambda j, i: (i, 0, 0, 0)),
                  pl.BlockSpec((1, tq, D_MODEL), lambda j, i: (i, j, 0)),
                  _resident(wuv.shape), _resident(w_out.shape)],
        out_specs=pl.BlockSpec((1, tq, D_MODEL), lambda j, i: (i, j, 0)),
        out_shape=jax.ShapeDtypeStruct(x.shape, F32),
        scratch_shapes=[pltpu.VMEM((1, MLA_HEADS * tq), F32), pltpu.VMEM((1, MLA_HEADS * tq), F32),
                        pltpu.VMEM((MLA_KV_LORA, MLA_HEADS * tq), F32),
                        pltpu.VMEM((2 * tq, MLA_HEADS * tq), F32),
                        pltpu.VMEM((2 * tq, MLA_HEADS * tq), F32),
                        pltpu.VMEM((tq, MLA_HEADS * MLA_V), BF16)],
        compiler_params=_params(("arbitrary", "parallel"), 40),
        name="mla_attn_prompt",
    )(q, kext, vt, x, wuv, w_out)


def _mla_attn_sample_kernel(pt_ref, q_ref, knew_ref, lat_hbm, krt_hbm, o_ref,
                            cbuf, rbuf, s_ref, sems, *, t_new, n_pages):
    g = pl.program_id(0)
    slot = g % 2
    n_rows = q_ref.shape[0]
    n_keys = n_pages * PAGE_SIZE
    n_chunks = n_keys // MLA_KEY_CHUNK

    def page_copies(group, slot_):
        copies = []
        for r in range(n_rows):
            for p in range(n_pages):
                page = pt_ref[(group * n_rows + r) * n_pages + p]
                keys = pl.ds(p * PAGE_SIZE, PAGE_SIZE)
                copies.append(pltpu.make_async_copy(lat_hbm.at[page], cbuf.at[slot_, r, keys, :],
                                                    sems.at[0, slot_]))
                copies.append(pltpu.make_async_copy(krt_hbm.at[page], rbuf.at[slot_, r, :, keys],
                                                    sems.at[1, slot_]))
        return copies

    @pl.when(g == 0)
    def _():
        for cp in page_copies(g, slot):
            cp.start()

    @pl.when(g + 1 < pl.num_programs(0))
    def _():
        for cp in page_copies(g + 1, 1 - slot):
            cp.start()

    for cp in page_copies(g, slot):
        cp.wait()

    def past_scores(r, j):
        keys = slice(j * MLA_KEY_CHUNK, (j + 1) * MLA_KEY_CHUNK)
        q = q_ref[r]
        c = cbuf[slot, r, keys, :].astype(BF16)
        rt = rbuf[slot, r, :, keys].astype(BF16)
        s_ref[r, :, keys] = (_dot_nt(q[:, :MLA_KV_LORA], c)
                             + _dot(q[:, MLA_KV_LORA:MLA_KV_LORA + MLA_ROPE], rt))

    for j in range(n_chunks):
        past_scores(0, j)
    for r in range(n_rows):
        kn = knew_ref[r]
        sn = _dot_nt(q_ref[r], kn)
        row_t = lax.broadcasted_iota(jnp.int32, sn.shape, 0) % t_new
        key = lax.broadcasted_iota(jnp.int32, sn.shape, 1)
        s_ref[r, :, n_keys:] = jnp.where((key < t_new) & (key <= row_t), sn, -jnp.inf)
        s = s_ref[r]
        e = jnp.exp(s - jnp.max(s, axis=-1, keepdims=True))
        denom = jnp.sum(e, axis=-1, keepdims=True)
        eb = e.astype(BF16)
        acc = _dot(eb[:, n_keys:], kn[:, :MLA_KV_LORA])
        for j in range(n_chunks):
            keys = slice(j * MLA_KEY_CHUNK, (j + 1) * MLA_KEY_CHUNK)
            acc = acc + _dot(eb[:, keys], cbuf[slot, r, keys, :].astype(BF16))
            if r + 1 < n_rows:
                past_scores(r + 1, j)
        o_ref[r] = acc / denom


def _mla_attn_sample(page_table, q, cache_lat, cache_krt, knew, t_new):
    b, rows, _ = q.shape
    n_pages = page_table.shape[1]
    n_keys = n_pages * PAGE_SIZE
    nr = MLA_SAMPLE_ROWS_PER_STEP
    assert n_keys % MLA_KEY_CHUNK == 0 and b % nr == 0
    grid_spec = pltpu.PrefetchScalarGridSpec(
        num_scalar_prefetch=1,
        grid=(b // nr,),
        in_specs=[pl.BlockSpec((nr, rows, MLA_KEY_DIM), lambda i, pt: (i, 0, 0)),
                  pl.BlockSpec((nr, PAGE_SIZE, MLA_KEY_DIM), lambda i, pt: (i, 0, 0)),
                  pl.BlockSpec(memory_space=pl.ANY),
                  pl.BlockSpec(memory_space=pl.ANY)],
        out_specs=pl.BlockSpec((nr, rows, MLA_KV_LORA), lambda i, pt: (i, 0, 0)),
        scratch_shapes=[pltpu.VMEM((2, nr, n_keys, MLA_KV_LORA), F32),
                        pltpu.VMEM((2, nr, MLA_ROPE, n_keys), F32),
                        pltpu.VMEM((nr, rows, n_keys + PAGE_SIZE), F32),
                        pltpu.SemaphoreType.DMA((2, 2))],
    )
    return pl.pallas_call(
        functools.partial(_mla_attn_sample_kernel, t_new=t_new, n_pages=n_pages),
        grid_spec=grid_spec,
        out_shape=jax.ShapeDtypeStruct((b, rows, MLA_KV_LORA), F32),
        compiler_params=_params(("arbitrary",), 56),
        name="mla_attn_sample",
    )(page_table.reshape(-1), q, knew, cache_lat, cache_krt)


def _mla_out_kernel(o_ref, wuv_ref, wout_ref, x_ref, y_ref, cat_ref):
    for hh in range(MLA_HEADS):
        cat_ref[:, hh * MLA_V:(hh + 1) * MLA_V] = _dot(o_ref[0, hh], wuv_ref[hh]).astype(BF16)
    y_ref[0] = x_ref[0] + _dot(cat_ref[...], wout_ref[...])


def _mla_out(o_lat, wuv, w_out, x, tm):
    nb, t, _ = x.shape
    assert t % tm == 0
    return pl.pallas_call(
        _mla_out_kernel,
        grid=(nb, t // tm),
        in_specs=[pl.BlockSpec((1, MLA_HEADS, tm, MLA_KV_LORA), lambda i, j: (i, 0, j, 0)),
                  _resident(wuv.shape), _resident(w_out.shape),
                  pl.BlockSpec((1, tm, D_MODEL), lambda i, j: (i, j, 0))],
        out_specs=pl.BlockSpec((1, tm, D_MODEL), lambda i, j: (i, j, 0)),
        out_shape=jax.ShapeDtypeStruct(x.shape, F32),
        scratch_shapes=[pltpu.VMEM((tm, MLA_HEADS * MLA_V), BF16)],
        compiler_params=_params(("parallel", "parallel"), 32),
        name="mla_out",
    )(o_lat, wuv, w_out, x)


def _pool_prompt_kernel(x_ref, xp_ref, g_ref, w_ref, sc_ref, o_ref, hl_ref, ext_ref, tmp_ref):
    j = pl.program_id(1)
    x = x_ref[0]
    tm = x.shape[0]
    h = _rms(x, g_ref[...])
    hp = _rms(xp_ref[0], g_ref[...])
    ext_ref[0:POOL_PREV, :] = jnp.where(j == 0, 0.0, hp)
    ext_ref[POOL_PREV:, :] = h
    tmp_ref[0:POOL_ALIGN, :] = jnp.zeros((POOL_ALIGN, D_MODEL), F32)
    n = POOL_PREV + tm - POOL_ALIGN
    pos = (j * tm + lax.broadcasted_iota(jnp.int32, (tm, 1), 0)).astype(F32)
    outs = []
    for gi, w in enumerate(POOL_WINDOWS):
        lanes = slice(gi * POOL_GROUP_DIM, (gi + 1) * POOL_GROUP_DIM)
        src, dst = ext_ref, tmp_ref
        shift = 1
        while shift < w:
            dst[POOL_ALIGN:, lanes] = (src[POOL_ALIGN:, lanes]
                                       + src[POOL_ALIGN - shift:POOL_ALIGN - shift + n, lanes])
            src, dst = dst, src
            shift *= 2
        pooled = src[POOL_PREV:, lanes] / jnp.minimum(pos + 1.0, float(w)) - h[:, lanes]
        outs.append(_dot(pooled.astype(BF16), w_ref[gi]))
    o_ref[0] = x + jnp.concatenate(outs, axis=-1) * sc_ref[...]

    @pl.when(j == pl.num_programs(1) - 1)
    def _():
        hl_ref[0] = h[tm - POOL_TAIL:]


def _pool_prompt(x, gain, w_pool, scale):
    b, t, _ = x.shape
    tm = 512
    assert t % tm == 0 and tm % POOL_PREV == 0
    per_tile = tm // POOL_PREV
    return pl.pallas_call(
        _pool_prompt_kernel,
        grid=(b, t // tm),
        in_specs=[pl.BlockSpec((1, tm, D_MODEL), lambda i, j: (i, j, 0)),
                  pl.BlockSpec((1, POOL_PREV, D_MODEL),
                               lambda i, j: (i, jnp.maximum(j * per_tile - 1, 0), 0)),
                  _resident(gain.shape), _resident(w_pool.shape), _resident(scale.shape)],
        out_specs=[pl.BlockSpec((1, tm, D_MODEL), lambda i, j: (i, j, 0)),
                   pl.BlockSpec((1, POOL_TAIL, D_MODEL), lambda i, j: (i, 0, 0))],
        out_shape=[jax.ShapeDtypeStruct(x.shape, F32),
                   jax.ShapeDtypeStruct((b, POOL_TAIL, D_MODEL), F32)],
        scratch_shapes=[pltpu.VMEM((POOL_PREV + tm, D_MODEL), F32),
                        pltpu.VMEM((POOL_PREV + tm, D_MODEL), F32)],
        compiler_params=_params(("parallel", "arbitrary"), 32),
        name="pool_prompt",
    )(x, x, gain, w_pool, scale)


def _pool_sample_kernel(x_ref, pre_ref, g_ref, w_ref, sc_ref, o_ref, h_ref, *, pos0):
    t_len = x_ref.shape[0]
    hs = [_rms(x_ref[t], g_ref[...]) for t in range(t_len)]
    for t in range(t_len):
        h_ref[t] = hs[t]

    def ext(e, lanes):
        return pre_ref[e, :, lanes] if e < POOL_BUF else hs[e - POOL_BUF][:, lanes]

    for t in range(t_len):
        outs = []
        for gi, w in enumerate(POOL_WINDOWS):
            lanes = slice(gi * POOL_GROUP_DIM, (gi + 1) * POOL_GROUP_DIM)
            acc = hs[t][:, lanes]
            for k in range(1, w):
                acc = acc + ext(POOL_BUF + t - k, lanes)
            pooled = acc / min(pos0 + t + 1.0, float(w)) - hs[t][:, lanes]
            outs.append(_dot(pooled.astype(BF16), w_ref[gi]))
        o_ref[t] = x_ref[t] + jnp.concatenate(outs, axis=-1) * sc_ref[...]


def _pool_sample(x_t, prefix_t, gain, w_pool, scale, pos0):
    vmem = pl.BlockSpec(memory_space=pltpu.VMEM)
    return pl.pallas_call(
        functools.partial(_pool_sample_kernel, pos0=pos0),
        in_specs=[vmem] * 5,
        out_specs=[vmem, vmem],
        out_shape=[jax.ShapeDtypeStruct(x_t.shape, F32), jax.ShapeDtypeStruct(x_t.shape, F32)],
        compiler_params=_params(None, 32),
        name="pool_sample",
    )(x_t, prefix_t, gain, w_pool, scale)


def _rope_tables(pos, half):
    inv = ROPE_BASE ** (-jnp.arange(half, dtype=F32) / half)
    ang = pos.astype(F32)[:, None] * inv[None, :]
    return jnp.cos(ang), jnp.sin(ang)


def _mla_rope_tables(pos):
    cos, sin = _rope_tables(pos, MLA_ROPE // 2)
    zeros = jnp.zeros((pos.shape[0], 128 - MLA_ROPE), F32)
    return (jnp.concatenate([cos, cos, zeros], axis=-1),
            jnp.concatenate([-sin, sin, zeros], axis=-1))


def _swap_halves(w):
    half = w.shape[-1] // 2
    return jnp.concatenate([w[..., half:], w[..., :half]], axis=-1)


def _pad_lanes(w, width):
    return jnp.pad(w, [(0, 0)] * (w.ndim - 1) + [(0, width - w.shape[-1])])


def _row(v):
    return v.reshape(1, -1).astype(F32)


def kernel(x_prompt, x_sample, state_ret, cache_mla_latent, cache_mla_krope, page_table, state_pool,
           norm_mix, norm_ffn, norm_final,
           ret_w_in, ret_gn_gain, ret_w_out,
           cm_w_in, cm_ln_gain, cm_w_spatial, cm_b_spatial, cm_w_out,
           mla_w_down, mla_q_norm, mla_kv_norm, mla_w_uq, mla_w_uk, mla_w_uv, mla_w_out,
           pool_w, pool_scale,
           ffn_w_gate_up, ffn_w_down):
    bp, tp, _ = x_prompt.shape
    bs, ts, _ = x_sample.shape
    xp = x_prompt
    xs = x_sample.reshape(bs * ts, D_MODEL)
    pos_p = jnp.arange(tp)
    pos_s = PAST_LEN + jnp.arange(ts)

    def ffn(xp_, xs_, i, final):
        yp, ys = _ffn(xp_.reshape(-1, D_MODEL), xs_, _row(norm_ffn[i]), ffn_w_gate_up, ffn_w_down, i,
                      _row(norm_final), final)
        return yp.reshape(xp_.shape), ys

    log_g = jnp.log1p(-jnp.exp2(-5.0 - jnp.arange(RET_HEADS, dtype=F32)))
    w_in = ret_w_in[0]
    w_out = ret_w_out[0]
    gn_gain = _row(ret_gn_gain[0])
    cos_p, sin_p = _rope_tables(pos_p, RET_QK_DIM // 2)
    cos_s, sin_s = _rope_tables(pos_s, RET_QK_DIM // 2)
    xp, ret_state_p = _ret_prompt(xp, _row(norm_mix[0]), w_in, cos_p, sin_p,
                                  _ret_tables(float(RET_PROMPT_CHUNK), RET_PROMPT_CHUNK, log_g),
                                  gn_gain, w_out)
    proj_s = _norm_matmul(xs, _row(norm_mix[0]), w_in, 1024)
    y_s, ret_state_s = _ret_sample(proj_s, state_ret[0], cos_s, sin_s,
                                   _ret_tables(float(ts), RET_CHUNK, log_g), gn_gain)
    xs = _matmul_residual(y_s, w_out, xs)
    xp, xs = ffn(xp, xs, 0, False)

    cw_in = cm_w_in[0]
    cw_out = cm_w_out[0]
    bias_full = jnp.repeat(jnp.transpose(cm_b_spatial[0]), CM_GROUP_DIM, axis=1)
    xp2, cm_v_p = _cm_prompt(xp.reshape(bp * tp, D_MODEL), _row(norm_mix[1]), cw_in,
                             _row(cm_ln_gain[0]), cm_w_spatial[0], bias_full, cw_out, tp)
    xp = xp2.reshape(bp, tp, D_MODEL)
    xs_t, cm_v_s_t = _cm_sample(xs.reshape(bs, ts, D_MODEL).transpose(1, 0, 2), _row(norm_mix[1]), cw_in,
                                _row(cm_ln_gain[0]),
                                cm_w_spatial[0][:, :ts, :ts].reshape(CM_GROUPS, ts * ts),
                                cm_b_spatial[0][:, :ts], cw_out)
    xs = xs_t.transpose(1, 0, 2).reshape(bs * ts, D_MODEL)
    cm_v_s = cm_v_s_t.transpose(1, 0, 2)
    xp, xs = ffn(xp, xs, 1, False)

    wd = mla_w_down[0]
    kr_w = wd[:, MLA_Q_LORA + MLA_KV_LORA:]
    wd_ext = jnp.concatenate([wd[:, :MLA_Q_LORA + MLA_KV_LORA], _pad_lanes(kr_w, 128),
                              _pad_lanes(_swap_halves(kr_w), 128)], axis=-1).astype(BF16)
    wuq = mla_w_uq[0].reshape(MLA_Q_LORA, MLA_HEADS, MLA_NOPE + MLA_ROPE)
    wuq_rope = wuq[:, :, MLA_NOPE:]
    wuq_ext = jnp.concatenate([
        wuq[:, :, :MLA_NOPE].reshape(MLA_Q_LORA, -1),
        _pad_lanes(wuq_rope, 128).reshape(MLA_Q_LORA, -1),
        _pad_lanes(_swap_halves(wuq_rope), 128).reshape(MLA_Q_LORA, -1)], axis=-1).astype(BF16)
    wuk_t = mla_w_uk[0].transpose(1, 2, 0).astype(BF16)
    wuv = mla_w_uv[0].transpose(1, 0, 2).astype(BF16)
    mw_out = mla_w_out[0].astype(BF16)
    cos2_p, sin2_p = _mla_rope_tables(pos_p)
    cos2_s, sin2_s = _mla_rope_tables(jnp.tile(pos_s, bs))
    gq, gkv = _row(mla_q_norm[0]), _row(mla_kv_norm[0])
    q_p, lat_p, krt_p, kext_p, vt_p = _mla_q(xp, _row(norm_mix[2]), wd_ext, gq, gkv, wuq_ext, wuk_t,
                                             cos2_p, sin2_p, 2 * MLA_ATTN_TQ)
    kr_p = jnp.swapaxes(krt_p, 1, 2)
    xp = _mla_attn_prompt(q_p, kext_p, vt_p, xp, wuv, mw_out)
    ns = bs * ts
    q_s, lat_s, krt_s, kext_s, _ = _mla_q(xs.reshape(1, ns, D_MODEL), _row(norm_mix[2]), wd_ext, gq, gkv,
                                          wuq_ext, wuk_t, cos2_s, sin2_s, ns)
    q_s = q_s.reshape(MLA_HEADS, bs, ts, MLA_KEY_DIM).transpose(1, 0, 2, 3).reshape(bs, MLA_HEADS * ts, MLA_KEY_DIM)
    knew = jnp.pad(kext_s.reshape(bs, ts, MLA_KEY_DIM), ((0, 0), (0, PAGE_SIZE - ts), (0, 0)))
    o_s = _mla_attn_sample(page_table, q_s, cache_mla_latent[0],
                           jnp.swapaxes(cache_mla_krope[0], 1, 2), knew, ts)
    o_s = o_s.reshape(bs, MLA_HEADS, ts, MLA_KV_LORA).transpose(1, 0, 2, 3).reshape(1, MLA_HEADS, ns, MLA_KV_LORA)
    xs = _mla_out(o_s.astype(BF16), wuv, mw_out, xs.reshape(1, ns, D_MODEL), ns).reshape(ns, D_MODEL)
    lat_s = lat_s.reshape(bs, ts, MLA_KV_LORA)
    kr_s = jnp.swapaxes(krt_s, 1, 2).reshape(bs, ts, MLA_ROPE)
    xp, xs = ffn(xp, xs, 2, False)

    pw = pool_w[0].astype(BF16)
    xp, h_last = _pool_prompt(xp, _row(norm_mix[3]), pw, _row(pool_scale[0]))
    pool_state_p = h_last[:, POOL_TAIL - POOL_BUF:]
    xs_t, h_s_t = _pool_sample(xs.reshape(bs, ts, D_MODEL).transpose(1, 0, 2),
                               state_pool[0].transpose(1, 0, 2), _row(norm_mix[3]), pw,
                               _row(pool_scale[0]), float(PAST_LEN))
    xs = xs_t.transpose(1, 0, 2).reshape(bs * ts, D_MODEL)
    pool_state_s = jnp.concatenate([state_pool[0], h_s_t.transpose(1, 0, 2)], axis=1)[:, -POOL_BUF:]
    xp, xs = ffn(xp, xs, 3, True)

    return (xp, xs.reshape(bs, ts, D_MODEL),
            ret_state_p[None], ret_state_s[None],
            cm_v_p.reshape(bp, CM_CHUNK, D_MODEL)[None], cm_v_s[None],
            lat_p[None], kr_p[None], lat_s[None], kr_s[None],
            pool_state_p[None], pool_state_s[None])
```
